```python
import jax, jax.numpy as jnp
from jax import lax
import numpy as np

D_MODEL = 4096
BATCH = 16
SEQ = 256
DEPTH = 2
DEC_BATCH = 2
DEC_SEQ = 4096
PAST_LEN = 512

GRID_W = 64
EPS = 1e-6
N_BRANCH = 4
BRANCH_W = D_MODEL // 4
FOURIER_GROUPS = 4
FOURIER_GW = BRANCH_W // FOURIER_GROUPS
LRU_W = BRANCH_W
LRU_BLOCKS = 8
LRU_BW = LRU_W // LRU_BLOCKS
CONV_W = 4
LRU_C = 8.0
HEAD_DIM = 128
N_HEADS = BRANCH_W // HEAD_DIM
N_KV = 2
GQA_GROUP = N_HEADS // N_KV
KV_W = N_KV * HEAD_DIM
ROPE_AX = HEAD_DIM // 2
ROPE_THETA = 10000.0
Q_BLOCK = 128
POOL_GROUPS = 4
POOL_W = BRANCH_W
POOL_GW = POOL_W // POOL_GROUPS
POOL_WINDOWS = (2, 4, 8, 16)
D_FF = -(-8 * D_MODEL // (3 * 256)) * 256
OFF_A = 0
OFF_BX = OFF_A + BRANCH_W
OFF_BY = OFF_BX + LRU_W
OFF_Q = OFF_BY + LRU_W
OFF_K = OFF_Q + N_HEADS * HEAD_DIM
OFF_V = OFF_K + KV_W
OFF_D = OFF_V + KV_W
OFF_G = OFF_D + POOL_W
IN_W = OFF_G + N_BRANCH * D_MODEL

kernel_name = 'hybrid_diffusion_parallel_mixer_step'


def rms_norm(x, g):
    x32 = x.astype(jnp.float32)
    return x32 * lax.rsqrt(jnp.mean(x32 * x32, axis=-1, keepdims=True) + EPS) * g.astype(jnp.float32)


def fourier_mix(u):
    B, L, _ = u.shape
    ug = u.astype(jnp.float32).reshape(B, L, FOURIER_GROUPS, FOURIER_GW)
    return jnp.fft.fft2(ug, axes=(1, 3), norm='ortho').real.reshape(B, L, BRANCH_W)


def centred_dwconv(u, w, b):
    L = u.shape[1]
    left = CONV_W // 2
    up = jnp.pad(u, ((0, 0), (left, CONV_W - 1 - left), (0, 0)))
    out = up[:, 0:L] * w[0] + b
    for j in range(1, CONV_W):
        out = out + up[:, j:j + L] * w[j]
    return out


def lin_scan(a, bx, h0):
    def comb(l, r):
        return l[0] * r[0], r[0] * l[1] + r[1]
    A, Bc = lax.associative_scan(comb, (a, bx), axis=1)
    return A * h0[:, None] + Bc


def rglru_dir(xc, wa, ba, wx, bx, lam, h0, reverse):
    B, L, _ = xc.shape
    if reverse:
        xc = jnp.flip(xc, axis=1)
    xb = xc.reshape(B, L, LRU_BLOCKS, LRU_BW)
    r = jax.nn.sigmoid(jnp.einsum('blnc,ncd->blnd', xb, wa).reshape(B, L, LRU_W) + ba)
    i = jax.nn.sigmoid(jnp.einsum('blnc,ncd->blnd', xb, wx).reshape(B, L, LRU_W) + bx)
    log_a = -LRU_C * r * jax.nn.softplus(-lam.astype(jnp.float32))
    a = jnp.exp(log_a)
    inp = jnp.sqrt(-jnp.expm1(2.0 * log_a)) * (i * xc)
    h = lin_scan(a, inp, h0)
    h_last = h[:, -1]
    if reverse:
        h = jnp.flip(h, axis=1)
    return h, h_last


def axial_rope(x):
    L = x.shape[1]
    rows = L // GRID_W
    row = jnp.repeat(jnp.arange(rows), GRID_W).astype(jnp.float32)
    col = jnp.tile(jnp.arange(GRID_W), rows).astype(jnp.float32)
    inv = ROPE_THETA ** (-jnp.arange(0, ROPE_AX, 2, dtype=jnp.float32) / ROPE_AX)
    half = ROPE_AX // 2

    def rot(xa, pos):
        ang = pos[:, None] * inv[None, :]
        cos = jnp.cos(ang)[None, :, None, :]
        sin = jnp.sin(ang)[None, :, None, :]
        x1, x2 = xa[..., :half], xa[..., half:]
        return jnp.concatenate([x1 * cos - x2 * sin, x1 * sin + x2 * cos], axis=-1)
    return jnp.concatenate([rot(x[..., :ROPE_AX], row), rot(x[..., ROPE_AX:], col)], axis=-1)


def block_attention(q, k, v):
    B, Lq, H, hd = q.shape
    nb = Lq // Q_BLOCK
    qb = q.reshape(B, nb, Q_BLOCK, N_KV, GQA_GROUP, hd).transpose(1, 0, 2, 3, 4, 5)
    scale = HEAD_DIM ** -0.5

    def one(qblk):
        s = jnp.einsum('bqkgd,bskd->bkgqs', qblk, k) * scale
        p = jax.nn.softmax(s, axis=-1)
        return jnp.einsum('bkgqs,bskd->bqkgd', p, v)
    o = lax.map(one, qb)
    return o.transpose(1, 0, 2, 3, 4, 5).reshape(B, Lq, H * hd)


def pool_mix(u, w_pool, scale):
    B, L, _ = u.shape
    ug = u.astype(jnp.float32).reshape(B, L, POOL_GROUPS, POOL_GW)
    S = jnp.concatenate([jnp.zeros_like(ug[:, :1]), jnp.cumsum(ug, axis=1)], axis=1)
    t = jnp.arange(L)[:, None]
    halfw = jnp.array(POOL_WINDOWS, jnp.int32)[None, :] // 2
    lo = jnp.clip(t - halfw, 0, L)
    hi = jnp.clip(t + halfw, 0, L)
    gi = jnp.arange(POOL_GROUPS)[None, :]
    win_sum = S[:, hi, gi] - S[:, lo, gi]
    cnt = (hi - lo).astype(jnp.float32)[None, :, :, None]
    pooled = win_sum / cnt - ug
    y = jnp.einsum('blgc,gcd->blgd', pooled, w_pool).reshape(B, L, POOL_W)
    return y * scale


def trunk_layer(x, mod, p, ctx):
    B, L, _ = x.shape
    f32 = jnp.float32
    sh_m, sc_m, g_m, sh_f, sc_f, g_f = jnp.split(mod, 6, axis=-1)
    h = rms_norm(x, p['g_pre_mix']) * (1.0 + sc_m) + sh_m
    z = h @ p['w_in']
    o_a = fourier_mix(z[..., OFF_A:OFF_A + BRANCH_W])
    xc = centred_dwconv(z[..., OFF_BX:OFF_BX + LRU_W], p['conv_w'], p['conv_b'])
    if ctx is None:
        h0f = jnp.zeros((B, LRU_W), f32)
        h0b = h0f
    else:
        st = ctx[2].astype(f32)
        h0f, h0b = st[:, 0], st[:, 1]
    hf, lf = rglru_dir(xc, p['lru_wa'][0], p['lru_ba'][0], p['lru_wx'][0], p['lru_bx'][0], p['lru_lambda'][0], h0f, False)
    hb, lb = rglru_dir(xc, p['lru_wa'][1], p['lru_ba'][1], p['lru_wx'][1], p['lru_bx'][1], p['lru_lambda'][1], h0b, True)
    o_b = (hf + hb) * jax.nn.gelu(z[..., OFF_BY:OFF_BY + LRU_W])
    q = rms_norm(z[..., OFF_Q:OFF_K].reshape(B, L, N_HEADS, HEAD_DIM), p['q_norm'])
    k = rms_norm(z[..., OFF_K:OFF_V].reshape(B, L, N_KV, HEAD_DIM), p['k_norm'])
    v = z[..., OFF_V:OFF_D].reshape(B, L, N_KV, HEAD_DIM)
    if ctx is None:
        k_all, v_all = k, v
    else:
        q = axial_rope(q)
        k_all = jnp.concatenate([axial_rope(k), ctx[0].astype(f32)], axis=1)
        v_all = jnp.concatenate([v, ctx[1].astype(f32)], axis=1)
    o_c = block_attention(q, k_all, v_all)
    o_d = pool_mix(z[..., OFF_D:OFF_G], p['pool_w'], p['pool_scale'])
    gates = jax.nn.sigmoid(z[..., OFF_G:].reshape(B, L, N_BRANCH, D_MODEL))
    branches = (o_a, o_b, o_c, o_d)
    merged = gates[:, :, 0] * (branches[0] @ p['w_branch'][0])
    for n in range(1, N_BRANCH):
        merged = merged + gates[:, :, n] * (branches[n] @ p['w_branch'][n])
    mix_out = merged @ p['w_out']
    x = x + g_m * rms_norm(mix_out, p['g_post_mix'])
    h = rms_norm(x, p['g_pre_ffn']) * (1.0 + sc_f) + sh_f
    f = (jax.nn.silu(h @ p['w_ffn_gate']) * (h @ p['w_ffn_up'])) @ p['w_ffn_down']
    x = x + g_f * rms_norm(f, p['g_post_ffn'])
    new_ctx = (k, v, jnp.stack([lf, lb], axis=1)) if ctx is None else None
    return x, new_ctx


def setup_inputs(seed: int = 0) -> dict:
    key = jax.random.key(seed)
    ks = jax.random.split(key, 32)
    f32 = jnp.float32
    D = D_MODEL

    def nrm(k, shape, scale):
        return jax.random.normal(k, shape, f32) * scale

    def gain(k, shape):
        return 1.0 + 0.05 * jax.random.normal(k, shape, f32)

    u = jax.random.uniform(ks[15], (DEPTH, 2, LRU_W), f32, 0.9, 0.999)
    a0 = u ** (1.0 / LRU_C)
    lru_lambda = jnp.log(a0) - jnp.log1p(-a0)
    return {
        'x_prompt': nrm(ks[0], (BATCH, SEQ, D), 1.0),
        'x_sample': nrm(ks[1], (DEC_BATCH, DEC_SEQ, D), 1.0),
        'c': nrm(ks[2], (DEC_BATCH, D), 1.0),
        'cache_k': nrm(ks[3], (DEC_BATCH, DEPTH, PAST_LEN, N_KV, HEAD_DIM), 1.0),
        'cache_v': nrm(ks[4], (DEC_BATCH, DEPTH, PAST_LEN, N_KV, HEAD_DIM), 1.0),
        'state_lru': nrm(ks[5], (DEC_BATCH, DEPTH, 2, LRU_W), 0.5),
        'c_ctx': nrm(ks[6], (D,), 1.0),
        'w_mod': nrm(ks[7], (DEPTH, D, 6 * D), D ** -0.5),
        'b_mod': nrm(ks[8], (DEPTH, 6 * D), 0.02),
        'g_pre_mix': gain(ks[9], (DEPTH, D)),
        'g_post_mix': gain(ks[10], (DEPTH, D)),
        'g_pre_ffn': gain(ks[11], (DEPTH, D)),
        'g_post_ffn': gain(ks[12], (DEPTH, D)),
        'w_in': nrm(ks[13], (DEPTH, D, IN_W), D ** -0.5),
        'conv_w': nrm(ks[14], (DEPTH, CONV_W, LRU_W), CONV_W ** -0.5),
        'conv_b': nrm(ks[16], (DEPTH, LRU_W), 0.02),
        'lru_wa': nrm(ks[17], (DEPTH, 2, LRU_BLOCKS, LRU_BW, LRU_BW), LRU_BW ** -0.5),
        'lru_ba': nrm(ks[18], (DEPTH, 2, LRU_W), 0.02),
        'lru_wx': nrm(ks[19], (DEPTH, 2, LRU_BLOCKS, LRU_BW, LRU_BW), LRU_BW ** -0.5),
        'lru_bx': nrm(ks[20], (DEPTH, 2, LRU_W), 0.02),
        'lru_lambda': lru_lambda,
        'q_norm': gain(ks[21], (DEPTH, HEAD_DIM)),
        'k_norm': gain(ks[22], (DEPTH, HEAD_DIM)),
        'pool_w': nrm(ks[23], (DEPTH, POOL_GROUPS, POOL_GW, POOL_GW), POOL_GW ** -0.5),
        'pool_scale': 0.5 + 0.1 * jax.random.normal(ks[24], (DEPTH, POOL_W), f32),
        'w_branch': nrm(ks[25], (DEPTH, N_BRANCH, BRANCH_W, D), BRANCH_W ** -0.5),
        'w_out': nrm(ks[26], (DEPTH, D, D), D ** -0.5),
        'w_ffn_gate': nrm(ks[27], (DEPTH, D, D_FF), D ** -0.5),
        'w_ffn_up': nrm(ks[28], (DEPTH, D, D_FF), D ** -0.5),
        'w_ffn_down': nrm(ks[29], (DEPTH, D_FF, D), D_FF ** -0.5),
    }


def reference(x_prompt, x_sample, c, cache_k, cache_v, state_lru, c_ctx, w_mod, b_mod,
              g_pre_mix, g_post_mix, g_pre_ffn, g_post_ffn, w_in, conv_w, conv_b,
              lru_wa, lru_ba, lru_wx, lru_bx, lru_lambda, q_norm, k_norm, pool_w, pool_scale,
              w_branch, w_out, w_ffn_gate, w_ffn_up, w_ffn_down):
    f32 = jnp.float32
    yp = x_prompt.astype(f32)
    ys = x_sample.astype(f32)
    k_list, v_list, s_list = [], [], []
    for l in range(DEPTH):
        p = {
            'g_pre_mix': g_pre_mix[l], 'g_post_mix': g_post_mix[l],
            'g_pre_ffn': g_pre_ffn[l], 'g_post_ffn': g_post_ffn[l],
            'w_in': w_in[l], 'conv_w': conv_w[l], 'conv_b': conv_b[l],
            'lru_wa': lru_wa[l], 'lru_ba': lru_ba[l], 'lru_wx': lru_wx[l], 'lru_bx': lru_bx[l],
            'lru_lambda': lru_lambda[l], 'q_norm': q_norm[l], 'k_norm': k_norm[l],
            'pool_w': pool_w[l], 'pool_scale': pool_scale[l],
            'w_branch': w_branch[l], 'w_out': w_out[l],
            'w_ffn_gate': w_ffn_gate[l], 'w_ffn_up': w_ffn_up[l], 'w_ffn_down': w_ffn_down[l],
        }
        mod_ctx = (jax.nn.silu(c_ctx.astype(f32)) @ w_mod[l] + b_mod[l])[None, None, :]
        mod_lat = (jax.nn.silu(c.astype(f32)) @ w_mod[l] + b_mod[l])[:, None, :]
        yp, (k_l, v_l, s_l) = trunk_layer(yp, mod_ctx, p, None)
        k_list.append(k_l)
        v_list.append(v_l)
        s_list.append(s_l)
        ys, _ = trunk_layer(ys, mod_lat, p, (cache_k[:, l], cache_v[:, l], state_lru[:, l]))
    new_cache_k = jnp.stack(k_list, axis=1).astype(cache_k.dtype)
    new_cache_v = jnp.stack(v_list, axis=1).astype(cache_v.dtype)
    new_state_lru = jnp.stack(s_list, axis=1).astype(state_lru.dtype)
    return (yp.astype(x_prompt.dtype), ys.astype(x_sample.dtype), new_cache_k, new_cache_v, new_state_lru)
```

```python
import functools
import math

import jax
import jax.numpy as jnp
from jax import lax
from jax.experimental import pallas as pl
from jax.experimental.pallas import tpu as pltpu

F32 = jnp.float32
BF16 = jnp.bfloat16
MIB = 1024 * 1024

EPS = 1e-6
GRID_W = 64
ROPE_THETA = 10000.0
LRU_C = 8.0
FOURIER_GROUPS = 4
POOL_WINDOWS = (2, 4, 8, 16)
N_BRANCH = 4
POOL_HALO = 8
CONV_LEFT = 2


def _params(semantics, vmem_mib):
    return pltpu.CompilerParams(dimension_semantics=semantics,
                                vmem_limit_bytes=vmem_mib * MIB)


def _rms(x, gain):
    ms = jnp.mean(x * x, axis=-1, keepdims=True)
    return x * lax.rsqrt(ms + EPS) * gain


def _softplus(x):
    return jnp.maximum(x, 0.0) + jnp.log1p(jnp.exp(-jnp.abs(x)))


def _group_of_tile(i, tm, p_rows, s_rows):
    r = i * tm
    return jnp.where(r < p_rows, 0, 1 + (r - p_rows) // s_rows)


def _mod_spec(k, d, tm, p_rows, s_rows):
    return pl.BlockSpec((None, 1, d),
                        lambda i, *_: (_group_of_tile(i, tm, p_rows, s_rows), 0, k))


def _mod_kernel(c_ref, w_ref, b_ref, o_ref):
    c = c_ref[...]
    s = (c * jax.nn.sigmoid(c)).astype(BF16)
    o_ref[...] = jnp.dot(s, w_ref[...].astype(BF16),
                         preferred_element_type=F32) + b_ref[...]


def _mod_all(cvec, w_mod, b_mod, tn=512):
    depth, d, n = w_mod.shape
    return pl.pallas_call(
        _mod_kernel,
        out_shape=jax.ShapeDtypeStruct((depth, 8, n), F32),
        grid=(depth, n // tn),
        in_specs=[pl.BlockSpec((8, d), lambda l, j: (0, 0)),
                  pl.BlockSpec((None, d, tn), lambda l, j: (l, 0, j)),
                  pl.BlockSpec((None, 1, tn), lambda l, j: (l, 0, j))],
        out_specs=pl.BlockSpec((None, 8, tn), lambda l, j: (l, 0, j)),
        compiler_params=_params(("arbitrary", "arbitrary"), 40),
        name="mod",
    )(cvec, w_mod, b_mod.reshape(depth, 1, n))


def _prenorm_kernel(x_ref, g_ref, sh_ref, sc_ref, o_ref):
    h = _rms(x_ref[...], g_ref[...])
    o_ref[...] = (h * (1.0 + sc_ref[...]) + sh_ref[...]).astype(o_ref.dtype)


def _prenorm(x, gain, mod_l, k_shift, p_rows, s_rows, tm=256):
    r, d = x.shape
    return pl.pallas_call(
        _prenorm_kernel,
        out_shape=jax.ShapeDtypeStruct((r, d), BF16),
        grid=(r // tm,),
        in_specs=[pl.BlockSpec((tm, d), lambda i: (i, 0)),
                  pl.BlockSpec((1, d), lambda i: (0, 0)),
                  _mod_spec(k_shift, d, tm, p_rows, s_rows),
                  _mod_spec(k_shift + 1, d, tm, p_rows, s_rows)],
        out_specs=pl.BlockSpec((tm, d), lambda i: (i, 0)),
        compiler_params=_params(("arbitrary",), 40),
        name="prenorm",
    )(x, gain.reshape(1, d), mod_l, mod_l)


def _mm_kernel(a_ref, b_ref, o_ref):
    o_ref[...] = jnp.dot(a_ref[...], b_ref[...],
                         preferred_element_type=F32).astype(o_ref.dtype)


def _matmul_layer(a, w, layer, n_cols, tm, tn, out_dtype, vmem_mib, name):
    m, k = a.shape
    tm, tn = min(tm, m), min(tn, n_cols)
    return pl.pallas_call(
        _mm_kernel,
        out_shape=jax.ShapeDtypeStruct((m, n_cols), out_dtype),
        grid=(m // tm, n_cols // tn),
        in_specs=[pl.BlockSpec((tm, k), lambda i, j: (i, 0)),
                  pl.BlockSpec((None, k, tn), lambda i, j: (layer, 0, j))],
        out_specs=pl.BlockSpec((tm, tn), lambda i, j: (i, j)),
        compiler_params=_params(("arbitrary", "arbitrary"), vmem_mib),
        name=name,
    )(a, w)


def _matmul_batched_rhs(a, b, tm, tn, out_dtype, vmem_mib, name):
    m, k = a.shape
    nb, _, n = b.shape
    tm, tn = min(tm, m), min(tn, n)
    return pl.pallas_call(
        _mm_kernel,
        out_shape=jax.ShapeDtypeStruct((nb, m, n), out_dtype),
        grid=(nb, m // tm, n // tn),
        in_specs=[pl.BlockSpec((tm, k), lambda bi, i, j: (i, 0)),
                  pl.BlockSpec((None, k, tn), lambda bi, i, j: (bi, 0, j))],
        out_specs=pl.BlockSpec((None, tm, tn), lambda bi, i, j: (bi, i, j)),
        compiler_params=_params(("arbitrary", "arbitrary", "arbitrary"), vmem_mib),
        name=name,
    )(a, b)


def _dft_tables(n):
    j = jnp.arange(n, dtype=jnp.int32)
    jk = (j[:, None] * j[None, :]) % n
    ang = jk.astype(F32) * (2.0 * math.pi / n)
    s = 1.0 / math.sqrt(n)
    return jnp.cos(ang) * s, jnp.sin(ang) * s


def _four1_kernel(u_ref, t_ref, o_ref, *, seq, gw):
    u = u_ref[...].astype(BF16)
    pq = jnp.dot(u, t_ref[...], preferred_element_type=F32)
    o_ref[0:seq, :] = pq[:, :gw].astype(o_ref.dtype)
    o_ref[seq:2 * seq, :] = pq[:, gw:].astype(o_ref.dtype)


def _fourier_stage1(z, tab, row0, nb, seq, bw, col0):
    gw = bw // FOURIER_GROUPS
    rb0, cb0 = row0 // seq, col0 // gw
    return pl.pallas_call(
        functools.partial(_four1_kernel, seq=seq, gw=gw),
        out_shape=jax.ShapeDtypeStruct((nb, 2 * seq, bw), BF16),
        grid=(nb, FOURIER_GROUPS),
        in_specs=[pl.BlockSpec((seq, gw), lambda b, g: (rb0 + b, cb0 + g)),
                  pl.BlockSpec((gw, 2 * gw), lambda b, g: (0, 0))],
        out_specs=pl.BlockSpec((None, 2 * seq, gw), lambda b, g: (b, 0, g)),
        compiler_params=_params(("arbitrary", "arbitrary"), 40),
        name="fourier1",
    )(z, tab)


def _lru_kernel(*refs, seq, rows, nseg, seg_len, chained, chunk):
    if chained:
        (zx_ref, zy_ref, cw_ref, cb_ref, wa_ref, wx_ref, ba_ref, bx_ref, lam_ref,
         h0_ref, o_ref, pad_ref, a0_ref, a1_ref, b0_ref, b1_ref) = refs
        st_ref = None
    else:
        (zx_ref, zy_ref, cw_ref, cb_ref, wa_ref, wx_ref, ba_ref, bx_ref, lam_ref,
         o_ref, st_ref, pad_ref, a0_ref, a1_ref, b0_ref, b1_ref) = refs
        h0_ref = None
    a_refs, b_refs = (a0_ref, a1_ref), (b0_ref, b1_ref)
    cw_ch = zx_ref.shape[1]
    halo = POOL_HALO
    ext = chunk + 2 * halo
    n_taps = cw_ref.shape[0]

    pad_ref[0:halo, :] = jnp.zeros((halo, cw_ch), F32)
    pad_ref[halo + rows:, :] = jnp.zeros((halo, cw_ch), F32)
    pad_ref[halo:halo + rows, :] = zx_ref[...]

    neg_sp = [-LRU_C * _softplus(-lam_ref[d:d + 1, :]) for d in range(2)]

    def gates(ci, carry):
        r0 = pl.multiple_of(ci * chunk, chunk)
        xe = pad_ref[pl.ds(r0, ext), :]
        t = (r0 + lax.broadcasted_iota(jnp.int32, (chunk, cw_ch), 0)) & (seq - 1)
        xc = jnp.zeros((chunk, cw_ch), F32) + cb_ref[...]
        for j in range(n_taps):
            d = j - CONV_LEFT
            if d == 0:
                xs = xe[halo:halo + chunk]
            else:
                xs = pltpu.roll(xe, (-d) % ext, axis=0)[halo:halo + chunk]
                xs = jnp.where((t >= -d) if d < 0 else (t < seq - d), xs, 0.0)
            xc = xc + xs * cw_ref[j:j + 1, :]
        xcb = xc.astype(BF16)
        for d in range(2):
            r = jax.nn.sigmoid(jnp.dot(xcb, wa_ref[d], preferred_element_type=F32)
                               + ba_ref[d:d + 1, :])
            i = jax.nn.sigmoid(jnp.dot(xcb, wx_ref[d], preferred_element_type=F32)
                               + bx_ref[d:d + 1, :])
            log_a = r * neg_sp[d]
            th = jnp.tanh(log_a)
            mult = jnp.sqrt(-2.0 * th / (1.0 - th))
            a_refs[d][pl.ds(r0, chunk), :] = jnp.exp(log_a)
            b_refs[d][pl.ds(r0, chunk), :] = mult * (i * xc)
        return carry

    lax.fori_loop(0, rows // chunk, gates, 0)

    def scan(direction):
        def body(s, carry):
            h, p = carry
            pos = s if direction == 0 else seg_len - 1 - s
            idx = pl.ds(pos, nseg, stride=seg_len)
            a = a_refs[direction][idx, :]
            h = a * h + b_refs[direction][idx, :]
            b_refs[direction][idx, :] = h
            if chained:
                p = a * p
                a_refs[direction][idx, :] = p
            return h, p
        init = (jnp.zeros((nseg, cw_ch), F32), jnp.ones((nseg, cw_ch), F32))
        lax.fori_loop(0, seg_len, body, init, unroll=8)

    scan(0)
    scan(1)

    if chained:
        carry = h0_ref[0:1, :]
        for q in range(nseg):
            sl = slice(q * seg_len, (q + 1) * seg_len)
            hq = b_refs[0][sl, :] + a_refs[0][sl, :] * carry
            b_refs[0][sl, :] = hq
            carry = hq[seg_len - 1:seg_len, :]
        carry = h0_ref[1:2, :]
        for q in reversed(range(nseg)):
            sl = slice(q * seg_len, (q + 1) * seg_len)
            hq = b_refs[1][sl, :] + a_refs[1][sl, :] * carry
            b_refs[1][sl, :] = hq
            carry = hq[0:1, :]
    else:
        st_ref[0] = b_refs[0][pl.ds(seq - 1, rows // seq, stride=seq), :]
        st_ref[1] = b_refs[1][pl.ds(0, rows // seq, stride=seq), :]

    def emit(ci, carry):
        r0 = pl.multiple_of(ci * chunk, chunk)
        sl = pl.ds(r0, chunk)
        o_ref[sl, :] = ((b_refs[0][sl, :] + b_refs[1][sl, :])
                        * jax.nn.gelu(zy_ref[sl, :])).astype(o_ref.dtype)
        return carry

    lax.fori_loop(0, rows // chunk, emit, 0)


def _lru(z, layer, conv_w, conv_b, wa, wx, ba, bx, lam, h0, *, row0, rows, ngroups,
         seq, nseg, seg_len, col_x, col_y, bw):
    cb = wa.shape[-1]
    ncb = bw // cb
    chained = h0 is not None
    chunk = min(256, rows)
    rb0 = row0 // rows
    cx0, cy0 = col_x // cb, col_y // cb
    in_specs = [
        pl.BlockSpec((rows, cb), lambda g, c: (rb0 + g, cx0 + c)),
        pl.BlockSpec((rows, cb), lambda g, c: (rb0 + g, cy0 + c)),
        pl.BlockSpec((None, conv_w.shape[1], cb), lambda g, c: (layer, 0, c)),
        pl.BlockSpec((None, 1, cb), lambda g, c: (layer, 0, c)),
        pl.BlockSpec((None, 2, None, cb, cb), lambda g, c: (layer, 0, c, 0, 0)),
        pl.BlockSpec((None, 2, None, cb, cb), lambda g, c: (layer, 0, c, 0, 0)),
        pl.BlockSpec((None, 2, cb), lambda g, c: (layer, 0, c)),
        pl.BlockSpec((None, 2, cb), lambda g, c: (layer, 0, c)),
        pl.BlockSpec((None, 2, cb), lambda g, c: (layer, 0, c)),
    ]
    args = [z, z, conv_w, conv_b.reshape(conv_b.shape[0], 1, bw), wa, wx, ba, bx, lam]
    out_shape = [jax.ShapeDtypeStruct((ngroups * rows, bw), BF16)]
    out_specs = [pl.BlockSpec((rows, cb), lambda g, c: (g, c))]
    if chained:
        in_specs.append(pl.BlockSpec((None, 2, cb), lambda g, c: (g, 0, c)))
        args.append(h0)
    else:
        assert ngroups == 1
        out_shape.append(jax.ShapeDtypeStruct((2, rows // seq, bw), F32))
        out_specs.append(pl.BlockSpec((2, rows // seq, cb), lambda g, c: (0, 0, c)))
    res = pl.pallas_call(
        functools.partial(_lru_kernel, seq=seq, rows=rows, nseg=nseg, seg_len=seg_len,
                          chained=chained, chunk=chunk),
        out_shape=out_shape,
        grid=(ngroups, ncb),
        in_specs=in_specs,
        out_specs=out_specs,
        scratch_shapes=[pltpu.VMEM((rows + 2 * POOL_HALO, cb), F32)]
                       + [pltpu.VMEM((rows, cb), F32)] * 4,
        compiler_params=_params(("arbitrary", "arbitrary"), 48),
        name="rglru",
    )(*args)
    return res if not chained else res[0]


def _rope_tables(seq, hd):
    rows = seq // GRID_W
    row = jnp.repeat(jnp.arange(rows), GRID_W).astype(F32)
    col = jnp.tile(jnp.arange(GRID_W), rows).astype(F32)
    ax = hd // 2
    inv = ROPE_THETA ** (-jnp.arange(0, ax, 2, dtype=F32) / ax)
    ang_r = row[:, None] * inv[None, :]
    ang_c = col[:, None] * inv[None, :]
    cos = jnp.concatenate([jnp.cos(ang_r)] * 2 + [jnp.cos(ang_c)] * 2, axis=-1)
    sin = jnp.concatenate([-jnp.sin(ang_r), jnp.sin(ang_r),
                           -jnp.sin(ang_c), jnp.sin(ang_c)], axis=-1)
    return cos, sin


def _rotate(x, cos, sin):
    hd = x.shape[1]
    q = hd // 4
    lane = lax.broadcasted_iota(jnp.int32, x.shape, 1)
    ahead = pltpu.roll(x, hd - q, axis=1)
    behind = pltpu.roll(x, q, axis=1)
    partner = jnp.where((lane & q) == 0, ahead, behind)
    return x * cos + partner * sin


def _attn_kernel(*refs, seq, cache_len, tq, tk, grp, hd, rope, emit_kv):
    refs = list(refs)
    zq_ref, zk_ref, zv_ref, qn_ref, kn_ref = refs[:5]
    pos = 5
    if rope:
        cq_ref, sq_ref, ck_ref, sk_ref = refs[pos:pos + 4]
        pos += 4
    if cache_len:
        kc_ref, vc_ref = refs[pos:pos + 2]
        pos += 2
    o_ref = refs[pos]
    pos += 1
    if emit_kv:
        ko_ref, vo_ref = refs[pos:pos + 2]
        pos += 2
    ks_ref, vs_ref = refs[pos:pos + 2]

    @pl.when(pl.program_id(2) == 0)
    def _():
        k = _rms(zk_ref[...], kn_ref[...])
        v = zv_ref[...]
        if emit_kv:
            ko_ref[...] = k
            vo_ref[...] = v
        if rope:
            k = _rotate(k, ck_ref[...], sk_ref[...])
        ks_ref[0:seq, :] = k.astype(BF16)
        vs_ref[0:seq, :] = v.astype(BF16)
        if cache_len:
            ks_ref[seq:seq + cache_len, :] = kc_ref[...].astype(BF16)
            vs_ref[seq:seq + cache_len, :] = vc_ref[...].astype(BF16)

    qs = []
    for hh in range(grp):
        q = _rms(zq_ref[:, hh * hd:(hh + 1) * hd], qn_ref[...])
        if rope:
            q = _rotate(q, cq_ref[...], sq_ref[...])
        qs.append(q.astype(BF16))
    q = jnp.concatenate(qs, axis=0)
    n = grp * tq
    scale = hd ** -0.5
    m = jnp.full((n, 1), -jnp.inf, F32)
    l = jnp.zeros((n, 1), F32)
    acc = jnp.zeros((n, hd), F32)
    for ci in range((seq + cache_len) // tk):
        kc = ks_ref[ci * tk:(ci + 1) * tk, :]
        s = lax.dot_general(q, kc, (((1,), (1,)), ((), ())),
                            preferred_element_type=F32) * scale
        m_new = jnp.maximum(m, jnp.max(s, axis=-1, keepdims=True))
        alpha = jnp.exp(m - m_new)
        p = jnp.exp(s - m_new)
        l = alpha * l + jnp.sum(p, axis=-1, keepdims=True)
        acc = alpha * acc + jnp.dot(p.astype(BF16), vs_ref[ci * tk:(ci + 1) * tk, :],
                                    preferred_element_type=F32)
        m = m_new
    out = acc / l
    for hh in range(grp):
        o_ref[:, hh * hd:(hh + 1) * hd] = out[hh * tq:(hh + 1) * tq].astype(o_ref.dtype)


def _attention(z, layer, q_norm, k_norm, *, row0, nb, seq, n_kv, grp, hd, col_q, col_k, col_v,
               tq, tk, rope_tabs=None, cache=None, emit_kv=False):
    qw = grp * hd
    cache_len = 0 if cache is None else cache[0].shape[2]
    nq = seq // tq
    rbq0, rbk0 = row0 // tq, row0 // seq
    cq0, ck0, cv0 = col_q // qw, col_k // hd, col_v // hd
    in_specs = [
        pl.BlockSpec((tq, qw), lambda b, h, i: (rbq0 + b * nq + i, cq0 + h)),
        pl.BlockSpec((seq, hd), lambda b, h, i: (rbk0 + b, ck0 + h)),
        pl.BlockSpec((seq, hd), lambda b, h, i: (rbk0 + b, cv0 + h)),
        pl.BlockSpec((None, 1, hd), lambda b, h, i: (layer, 0, 0)),
        pl.BlockSpec((None, 1, hd), lambda b, h, i: (layer, 0, 0)),
    ]
    args = [z, z, z, q_norm.reshape(-1, 1, hd), k_norm.reshape(-1, 1, hd)]
    if rope_tabs is not None:
        cos, sin = rope_tabs
        in_specs += [pl.BlockSpec((tq, hd), lambda b, h, i: (i, 0)),
                     pl.BlockSpec((tq, hd), lambda b, h, i: (i, 0)),
                     pl.BlockSpec((seq, hd), lambda b, h, i: (0, 0)),
                     pl.BlockSpec((seq, hd), lambda b, h, i: (0, 0))]
        args += [cos, sin, cos, sin]
    if cache is not None:
        in_specs += [pl.BlockSpec((None, None, cache_len, hd), lambda b, h, i: (b, h, 0, 0)),
                     pl.BlockSpec((None, None, cache_len, hd), lambda b, h, i: (b, h, 0, 0))]
        args += list(cache)
    out_shape = [jax.ShapeDtypeStruct((nb * seq, n_kv * qw), BF16)]
    out_specs = [pl.BlockSpec((tq, qw), lambda b, h, i: (b * nq + i, h))]
    if emit_kv:
        out_shape += [jax.ShapeDtypeStruct((nb * seq, n_kv * hd), F32)] * 2
        out_specs += [pl.BlockSpec((seq, hd), lambda b, h, i: (b, h))] * 2
    res = pl.pallas_call(
        functools.partial(_attn_kernel, seq=seq, cache_len=cache_len, tq=tq, tk=tk, grp=grp,
                          hd=hd, rope=rope_tabs is not None, emit_kv=emit_kv),
        out_shape=out_shape,
        grid=(nb, n_kv, nq),
        in_specs=in_specs,
        out_specs=out_specs,
        scratch_shapes=[pltpu.VMEM((seq + cache_len, hd), BF16),
                        pltpu.VMEM((seq + cache_len, hd), BF16)],
        compiler_params=_params(("arbitrary", "arbitrary", "arbitrary"), 48),
        name="attention",
    )(*args)
    return res if emit_kv else res[0]


def _pool_kernel(u_ref, w_ref, sc_ref, o_ref, pad_ref, *, rows, chunk, seq_p, seq_s, groups_p):
    gw = u_ref.shape[1]
    halo = POOL_HALO
    ext = chunk + 2 * halo
    seq = jnp.where(pl.program_id(0) < groups_p, seq_p, seq_s)
    pad_ref[0:halo, :] = jnp.zeros((halo, gw), F32)
    pad_ref[halo + rows:, :] = jnp.zeros((halo, gw), F32)
    pad_ref[halo:halo + rows, :] = u_ref[...]

    for gi, win in enumerate(POOL_WINDOWS):
        hw = win // 2

        @pl.when(pl.program_id(1) == gi)
        def _(hw=hw):
            def body(ci, carry):
                r0 = pl.multiple_of(ci * chunk, chunk)
                xe = pad_ref[pl.ds(r0, ext), :]
                t = (r0 + lax.broadcasted_iota(jnp.int32, (chunk, gw), 0)) & (seq - 1)
                x0 = xe[halo:halo + chunk]
                acc = x0
                for d in range(-hw, hw):
                    if d == 0:
                        continue
                    xs = pltpu.roll(xe, (-d) % ext, axis=0)[halo:halo + chunk]
                    acc = acc + jnp.where((t >= -d) if d < 0 else (t < seq - d), xs, 0.0)
                cnt = (jnp.minimum(t + hw, seq) - jnp.maximum(t - hw, 0)).astype(F32)
                pooled = acc / cnt - x0
                y = jnp.dot(pooled.astype(BF16), w_ref[...], preferred_element_type=F32)
                o_ref[pl.ds(r0, chunk), :] = (y * sc_ref[...]).astype(o_ref.dtype)
                return carry

            lax.fori_loop(0, rows // chunk, body, 0)


def _pool(z, layer, pool_w, pool_scale, *, rows, groups_p, seq_p, seq_s, col0, bw):
    r = z.shape[0]
    g = pool_w.shape[1]
    gw = bw // g
    cb0 = col0 // gw
    chunk = min(256, rows)
    return pl.pallas_call(
        functools.partial(_pool_kernel, rows=rows, chunk=chunk, seq_p=seq_p, seq_s=seq_s,
                          groups_p=groups_p),
        out_shape=jax.ShapeDtypeStruct((r, bw), BF16),
        grid=(r // rows, g),
        in_specs=[pl.BlockSpec((rows, gw), lambda i, j: (i, cb0 + j)),
                  pl.BlockSpec((None, None, gw, gw), lambda i, j: (layer, j, 0, 0)),
                  pl.BlockSpec((None, 1, gw), lambda i, j: (layer, 0, j))],
        out_specs=pl.BlockSpec((rows, gw), lambda i, j: (i, j)),
        scratch_shapes=[pltpu.VMEM((rows + 2 * POOL_HALO, gw), F32)],
        compiler_params=_params(("arbitrary", "arbitrary"), 40),
        name="pool",
    )(z, pool_w, pool_scale.reshape(pool_scale.shape[0], 1, bw))


def _merge_kernel(h_ref, g0, g1, g2, g3, o0, o1, o2, o3, wb_ref, out_ref):
    h = h_ref[...]
    acc = None
    for n, (g_ref, o_ref) in enumerate(((g0, o0), (g1, o1), (g2, o2), (g3, o3))):
        gate = jax.nn.sigmoid(jnp.dot(h, g_ref[...], preferred_element_type=F32))
        y = jnp.dot(o_ref[...], wb_ref[n], preferred_element_type=F32)
        acc = gate * y if acc is None else acc + gate * y
    out_ref[...] = acc.astype(out_ref.dtype)


def _gate_merge(h, w_in, layer, col_g, branches, w_branch, tm=512, tn=256):
    m, d = h.shape
    bw = w_branch.shape[2]
    dm = w_branch.shape[3]
    tm, tn = min(tm, m), min(tn, dm)
    gspecs = [pl.BlockSpec((None, d, tn),
                           functools.partial(lambda j, i, n: (layer, 0, (col_g + n * dm) // tn + j), n=n))
              for n in range(N_BRANCH)]
    ospecs = [pl.BlockSpec((tm, bw), lambda j, i: (i, 0)) for _ in range(N_BRANCH)]
    return pl.pallas_call(
        _merge_kernel,
        out_shape=jax.ShapeDtypeStruct((m, dm), BF16),
        grid=(dm // tn, m // tm),
        in_specs=[pl.BlockSpec((tm, d), lambda j, i: (i, 0))] + gspecs + ospecs
                 + [pl.BlockSpec((None, N_BRANCH, bw, tn), lambda j, i: (layer, 0, 0, j))],
        out_specs=pl.BlockSpec((tm, tn), lambda j, i: (i, j)),
        compiler_params=_params(("arbitrary", "arbitrary"), 52),
        name="gate_merge",
    )(h, w_in, w_in, w_in, w_in, *branches, w_branch)


def _postnorm_kernel(x_ref, y_ref, g_ref, gate_ref, o_ref):
    o_ref[...] = x_ref[...] + gate_ref[...] * _rms(y_ref[...], g_ref[...])


def _postnorm(x, y, gain, mod_l, k_gate, p_rows, s_rows, tm=256):
    r, d = x.shape
    return pl.pallas_call(
        _postnorm_kernel,
        out_shape=jax.ShapeDtypeStruct((r, d), F32),
        grid=(r // tm,),
        in_specs=[pl.BlockSpec((tm, d), lambda i: (i, 0)),
                  pl.BlockSpec((tm, d), lambda i: (i, 0)),
                  pl.BlockSpec((1, d), lambda i: (0, 0)),
                  _mod_spec(k_gate, d, tm, p_rows, s_rows)],
        out_specs=pl.BlockSpec((tm, d), lambda i: (i, 0)),
        compiler_params=_params(("arbitrary",), 40),
        name="postnorm",
    )(x, y, gain.reshape(1, d), mod_l)


def _ffn_kernel(x_ref, gpre_ref, sh_ref, sc_ref, gate_ref, gpost_ref, wg_ref, wu_ref, wd_ref,
                o_ref, h_ref, acc_ref):
    k = pl.program_id(1)

    @pl.when(k == 0)
    def _():
        h = _rms(x_ref[...], gpre_ref[...])
        h_ref[...] = (h * (1.0 + sc_ref[...]) + sh_ref[...]).astype(BF16)
        acc_ref[...] = jnp.zeros_like(acc_ref)

    h = h_ref[...]
    a = jnp.dot(h, wg_ref[...], preferred_element_type=F32)
    b = jnp.dot(h, wu_ref[...], preferred_element_type=F32)
    act = (a * jax.nn.sigmoid(a) * b).astype(BF16)
    acc_ref[...] += jnp.dot(act, wd_ref[...], preferred_element_type=F32)

    @pl.when(k == pl.num_programs(1) - 1)
    def _():
        o_ref[...] = x_ref[...] + gate_ref[...] * _rms(acc_ref[...], gpost_ref[...])


def _ffn(x, layer, g_pre, g_post, mod_l, wg, wu, wd, p_rows, s_rows, tm=512, tc=256):
    r, d = x.shape
    dff = wg.shape[2]
    tm, tc = min(tm, r), min(tc, dff)
    return pl.pallas_call(
        _ffn_kernel,
        out_shape=jax.ShapeDtypeStruct((r, d), F32),
        grid=(r // tm, dff // tc),
        in_specs=[pl.BlockSpec((tm, d), lambda i, k: (i, 0), pipeline_mode=pl.Buffered(1)),
                  pl.BlockSpec((1, d), lambda i, k: (0, 0)),
                  _mod_spec(3, d, tm, p_rows, s_rows),
                  _mod_spec(4, d, tm, p_rows, s_rows),
                  _mod_spec(5, d, tm, p_rows, s_rows),
                  pl.BlockSpec((1, d), lambda i, k: (0, 0)),
                  pl.BlockSpec((None, d, tc), lambda i, k: (layer, 0, k)),
                  pl.BlockSpec((None, d, tc), lambda i, k: (layer, 0, k)),
                  pl.BlockSpec((None, tc, d), lambda i, k: (layer, k, 0))],
        out_specs=pl.BlockSpec((tm, d), lambda i, k: (i, 0)),
        scratch_shapes=[pltpu.VMEM((tm, d), BF16), pltpu.VMEM((tm, d), F32)],
        compiler_params=_params(("arbitrary", "arbitrary"), 60),
        name="ffn",
    )(x, g_pre.reshape(1, d), mod_l, mod_l, mod_l, g_post.reshape(1, d), wg, wu, wd)


def kernel(x_prompt, x_sample, c, cache_k, cache_v, state_lru, c_ctx, w_mod, b_mod, g_pre_mix, g_post_mix, g_pre_ffn, g_post_ffn, w_in, conv_w, conv_b, lru_wa, lru_ba, lru_wx, lru_bx, lru_lambda, q_norm, k_norm, pool_w, pool_scale, w_branch, w_out, w_ffn_gate, w_ffn_up, w_ffn_down):
    nbp, seq_p, d = x_prompt.shape
    nbs, seq_s, _ = x_sample.shape
    depth = w_in.shape[0]
    bw = w_branch.shape[2]
    cache_len, n_kv, hd = cache_k.shape[2:]
    grp = bw // hd // n_kv
    kvw = n_kv * hd
    p_rows, s_rows = nbp * seq_p, nbs * seq_s
    assert seq_p & (seq_p - 1) == 0 and seq_s & (seq_s - 1) == 0
    assert p_rows % seq_s == 0, "row groups of seq_s rows must tile the prompt rows"
    col_a, col_bx, col_by, col_q = 0, bw, 2 * bw, 3 * bw
    col_k = col_q + bw
    col_v = col_k + kvw
    col_d = col_v + kvw
    col_g = col_d + bw
    zw = col_g

    def rows_tile(pref):
        return math.gcd(pref, math.gcd(p_rows, seq_s))

    w_in_b = w_in.astype(BF16)
    w_branch_b = w_branch.astype(BF16)
    w_out_b = w_out.astype(BF16)
    wg_b = w_ffn_gate.astype(BF16)
    wu_b = w_ffn_up.astype(BF16)
    wd_b = w_ffn_down.astype(BF16)
    wa_b = lru_wa.astype(BF16)
    wx_b = lru_wx.astype(BF16)
    pool_w_b = pool_w.astype(BF16)

    gw = bw // FOURIER_GROUPS
    cc, sc = _dft_tables(gw)
    tab_c = jnp.concatenate([cc, sc], axis=1).astype(BF16)
    cp, sp = _dft_tables(seq_p)
    dft_p = jnp.concatenate([cp, -sp], axis=1).astype(BF16)
    cs, ss = _dft_tables(seq_s)
    dft_s = jnp.concatenate([cs, -ss], axis=1).astype(BF16)
    rope_tabs = _rope_tables(seq_s, hd)

    cvec = jnp.concatenate([c_ctx[None, :], c, jnp.zeros((8 - 1 - nbs, d), c.dtype)], axis=0)
    mod = _mod_all(cvec.astype(F32), w_mod, b_mod)

    cache_kt = jnp.transpose(cache_k, (0, 1, 3, 2, 4)).astype(F32)
    cache_vt = jnp.transpose(cache_v, (0, 1, 3, 2, 4)).astype(F32)

    x = jnp.concatenate([x_prompt.reshape(p_rows, d), x_sample.reshape(s_rows, d)],
                        axis=0).astype(F32)
    k_list, v_list, s_list = [], [], []
    for l in range(depth):
        mod_l = mod[l].reshape(8, 1, 6 * d)
        h = _prenorm(x, g_pre_mix[l], mod_l, 0, p_rows, seq_s, tm=rows_tile(256))
        z = _matmul_layer(h, w_in_b, l, zw, rows_tile(1024), 512, F32, 48, "in_proj")

        pq_p = _fourier_stage1(z, tab_c, 0, nbp, seq_p, bw, col_a)
        oa_p = _matmul_batched_rhs(dft_p, pq_p, 512, 1024, BF16, 48, "fourier2p")
        pq_s = _fourier_stage1(z, tab_c, p_rows, nbs, seq_s, bw, col_a)
        oa_s = _matmul_batched_rhs(dft_s, pq_s, 512, 512, BF16, 48, "fourier2s")
        o_a = jnp.concatenate([oa_p.reshape(p_rows, bw), oa_s.reshape(s_rows, bw)], axis=0)

        lru_args = (conv_w, conv_b, wa_b, wx_b, lru_ba, lru_bx, lru_lambda)
        ob_p, st = _lru(z, l, *lru_args, None, row0=0, rows=p_rows, ngroups=1, seq=seq_p,
                        nseg=nbp, seg_len=seq_p, col_x=col_bx, col_y=col_by, bw=bw)
        seg = min(128, seq_s)
        ob_s = _lru(z, l, *lru_args, state_lru[:, l].astype(F32), row0=p_rows, rows=seq_s,
                    ngroups=nbs, seq=seq_s, nseg=seq_s // seg, seg_len=seg,
                    col_x=col_bx, col_y=col_by, bw=bw)
        o_b = jnp.concatenate([ob_p, ob_s], axis=0)
        s_list.append(jnp.transpose(st, (1, 0, 2)))

        att = dict(n_kv=n_kv, grp=grp, hd=hd, col_q=col_q, col_k=col_k, col_v=col_v)
        oc_p, k_p, v_p = _attention(z, l, q_norm, k_norm, row0=0, nb=nbp, seq=seq_p, tq=seq_p,
                                    tk=seq_p, emit_kv=True, **att)
        lk = seq_s + cache_len
        tk = next(t for t in (1152, 1024, 512, 256, 128, lk) if lk % t == 0)
        oc_s = _attention(z, l, q_norm, k_norm, row0=p_rows, nb=nbs, seq=seq_s,
                          tq=min(128, seq_s), tk=tk, rope_tabs=rope_tabs,
                          cache=(cache_kt[:, l], cache_vt[:, l]), **att)
        o_c = jnp.concatenate([oc_p, oc_s], axis=0)
        k_list.append(k_p.reshape(nbp, seq_p, n_kv, hd))
        v_list.append(v_p.reshape(nbp, seq_p, n_kv, hd))

        o_d = _pool(z, l, pool_w_b, pool_scale, rows=seq_s, groups_p=p_rows // seq_s,
                    seq_p=seq_p, seq_s=seq_s, col0=col_d, bw=bw)

        merged = _gate_merge(h, w_in_b, l, col_g, (o_a, o_b, o_c, o_d), w_branch_b,
                             tm=rows_tile(512))
        mix = _matmul_layer(merged, w_out_b, l, d, rows_tile(1024), 1024, F32, 52, "out_proj")
        x = _postnorm(x, mix, g_post_mix[l], mod_l, 2, p_rows, seq_s, tm=rows_tile(256))
        x = _ffn(x, l, g_pre_ffn[l], g_post_ffn[l], mod_l, wg_b, wu_b, wd_b, p_rows, seq_s,
                 tm=rows_tile(512))

    y_p = x[:p_rows].reshape(nbp, seq_p, d).astype(x_prompt.dtype)
    y_s = x[p_rows:].reshape(nbs, seq_s, d).astype(x_sample.dtype)
    new_k = jnp.stack(k_list, axis=1).astype(cache_k.dtype)
    new_v = jnp.stack(v_list, axis=1).astype(cache_v.dtype)
    new_s = jnp.stack(s_list, axis=1).astype(state_lru.dtype)
    return y_p, y_s, new_k, new_v, new_s
```

```python
import functools
import math

import jax
import jax.numpy as jnp
from jax import lax
from jax.experimental import pallas as pl
from jax.experimental.pallas import tpu as pltpu

F32 = jnp.float32
BF16 = jnp.bfloat16
MIB = 1024 * 1024

EPS = 1e-6
GRID_W = 64
ROPE_THETA = 10000.0
LRU_C = 8.0
FOURIER_GROUPS = 4
POOL_WINDOWS = (2, 4, 8, 16)
N_BRANCH = 4
HALO = 8
CONV_LEFT = 2
SEG_PAD = 8


def _params(semantics, vmem_mib):
    return pltpu.CompilerParams(dimension_semantics=semantics,
                                vmem_limit_bytes=vmem_mib * MIB)


def _rms(x, gain):
    ms = jnp.mean(x * x, axis=-1, keepdims=True)
    return x * lax.rsqrt(ms + EPS) * gain


def _softplus(x):
    return jnp.maximum(x, 0.0) + jnp.log1p(jnp.exp(-jnp.abs(x)))


def _group_of_tile(i, tm, p_rows, s_rows):
    r = i * tm
    return jnp.where(r < p_rows, 0, 1 + (r - p_rows) // s_rows)


def _mod_spec(k, d, tm, p_rows, s_rows, tile0=0):
    return pl.BlockSpec((None, 1, d),
                        lambda i, *_: (_group_of_tile(tile0 + i, tm, p_rows, s_rows), 0, k))


def _fill(kernel, in_specs, args, into):
    if into is None:
        return kernel, {}
    n = len(args)
    in_specs.append(pl.BlockSpec(memory_space=pl.ANY))
    args.append(into)

    def body(*refs):
        return kernel(*refs[:n], *refs[n + 1:])
    return body, {n: 0}


def _mod_kernel(c_ref, w_ref, b_ref, o_ref):
    c = c_ref[...]
    s = (c * jax.nn.sigmoid(c)).astype(BF16)
    o_ref[...] = jnp.dot(s, w_ref[...].astype(BF16),
                         preferred_element_type=F32) + b_ref[...]


def _mod_all(cvec, w_mod, b_mod, tn=512):
    depth, d, n = w_mod.shape
    return pl.pallas_call(
        _mod_kernel,
        out_shape=jax.ShapeDtypeStruct((depth, 8, n), F32),
        grid=(depth, n // tn),
        in_specs=[pl.BlockSpec((8, d), lambda l, j: (0, 0)),
                  pl.BlockSpec((None, d, tn), lambda l, j: (l, 0, j)),
                  pl.BlockSpec((None, 1, tn), lambda l, j: (l, 0, j))],
        out_specs=pl.BlockSpec((None, 8, tn), lambda l, j: (l, 0, j)),
        compiler_params=_params(("arbitrary", "arbitrary"), 40),
        name="mod",
    )(cvec, w_mod, b_mod.reshape(depth, 1, n))


def _prenorm_kernel(x_ref, g_ref, sh_ref, sc_ref, o_ref):
    h = _rms(x_ref[...], g_ref[...])
    o_ref[...] = (h * (1.0 + sc_ref[...]) + sh_ref[...]).astype(o_ref.dtype)


def _prenorm(rng, total_rows, gain, mod_l, k_shift, p_rows, s_rows, tm, into=None):
    src, src_row0, uni_row0, nrows = rng
    d = src.shape[1]
    st0, ut0 = src_row0 // tm, uni_row0 // tm
    in_specs = [pl.BlockSpec((tm, d), lambda i: (st0 + i, 0)),
                pl.BlockSpec((1, d), lambda i: (0, 0)),
                _mod_spec(k_shift, d, tm, p_rows, s_rows, ut0),
                _mod_spec(k_shift + 1, d, tm, p_rows, s_rows, ut0)]
    args = [src, gain.reshape(1, d), mod_l, mod_l]
    body, aliases = _fill(_prenorm_kernel, in_specs, args, into)
    return pl.pallas_call(
        body,
        out_shape=jax.ShapeDtypeStruct((total_rows, d), BF16),
        grid=(nrows // tm,),
        in_specs=in_specs,
        out_specs=pl.BlockSpec((tm, d), lambda i: (ut0 + i, 0)),
        input_output_aliases=aliases,
        compiler_params=_params(("arbitrary",), 40),
        name="prenorm",
    )(*args)


def _mm_kernel(a_ref, b_ref, o_ref):
    o_ref[...] = jnp.dot(a_ref[...], b_ref[...],
                         preferred_element_type=F32).astype(o_ref.dtype)


def _matmul_layer(a, w, layer, n_cols, tm, tn, out_dtype, vmem_mib, name):
    m, k = a.shape
    tm, tn = min(tm, m), min(tn, n_cols)
    return pl.pallas_call(
        _mm_kernel,
        out_shape=jax.ShapeDtypeStruct((m, n_cols), out_dtype),
        grid=(m // tm, n_cols // tn),
        in_specs=[pl.BlockSpec((tm, k), lambda i, j: (i, 0)),
                  pl.BlockSpec((None, k, tn), lambda i, j: (layer, 0, j))],
        out_specs=pl.BlockSpec((tm, tn), lambda i, j: (i, j)),
        compiler_params=_params(("arbitrary", "arbitrary"), vmem_mib),
        name=name,
    )(a, w)


def _matmul_batched_rhs(a, b, tm, tn, out_dtype, vmem_mib, name, total_rows, row0, into=None):
    m, k = a.shape
    nb, _, n = b.shape
    tm, tn = min(tm, m), min(tn, n)
    t0, nt = row0 // tm, m // tm
    in_specs = [pl.BlockSpec((tm, k), lambda bi, i, j: (i, 0)),
                pl.BlockSpec((None, k, tn), lambda bi, i, j: (bi, 0, j))]
    args = [a, b]
    body, aliases = _fill(_mm_kernel, in_specs, args, into)
    return pl.pallas_call(
        body,
        out_shape=jax.ShapeDtypeStruct((total_rows, n), out_dtype),
        grid=(nb, nt, n // tn),
        in_specs=in_specs,
        out_specs=pl.BlockSpec((tm, tn), lambda bi, i, j: (t0 + bi * nt + i, j)),
        input_output_aliases=aliases,
        compiler_params=_params(("arbitrary", "arbitrary", "arbitrary"), vmem_mib),
        name=name,
    )(*args)


def _dft_tables(n, split=64):
    s = 1.0 / math.sqrt(n)
    k = jnp.arange(n, dtype=jnp.int32)

    def thin(rows, period):
        r = jnp.arange(rows, dtype=jnp.int32)
        ang = ((r[:, None] * k[None, :]) % period).astype(F32) * (2.0 * math.pi / period)
        return jnp.cos(ang), jnp.sin(ang)

    if n <= split or n % split:
        c, sn = thin(n, n)
        return c * s, sn * s
    ca, sa = thin(n // split, n // split)
    cb, sb = thin(split, n)
    ca, sa = ca[:, None, :] * s, sa[:, None, :] * s
    cos = (ca * cb[None] - sa * sb[None]).reshape(n, n)
    sin = (sa * cb[None] + ca * sb[None]).reshape(n, n)
    return cos, sin


def _four1_kernel(u_ref, t_ref, o_ref, *, seq, gw):
    u = u_ref[...].astype(BF16)
    pq = jnp.dot(u, t_ref[...], preferred_element_type=F32)
    o_ref[0:seq, :] = pq[:, :gw].astype(o_ref.dtype)
    o_ref[seq:2 * seq, :] = pq[:, gw:].astype(o_ref.dtype)


def _fourier_stage1(z, tab, row0, nb, seq, bw, col0):
    gw = bw // FOURIER_GROUPS
    rb0, cb0 = row0 // seq, col0 // gw
    return pl.pallas_call(
        functools.partial(_four1_kernel, seq=seq, gw=gw),
        out_shape=jax.ShapeDtypeStruct((nb, 2 * seq, bw), BF16),
        grid=(nb, FOURIER_GROUPS),
        in_specs=[pl.BlockSpec((seq, gw), lambda b, g: (rb0 + b, cb0 + g)),
                  pl.BlockSpec((gw, 2 * gw), lambda b, g: (0, 0))],
        out_specs=pl.BlockSpec((None, 2 * seq, gw), lambda b, g: (b, 0, g)),
        compiler_params=_params(("arbitrary", "arbitrary"), 40),
        name="fourier1",
    )(z, tab)


def _lru_kernel(*refs, seq, rows, nseg, seg_len, chained, chunk):
    if chained:
        (zx_ref, zy_ref, cw_ref, cb_ref, wa_ref, wx_ref, ba_ref, bx_ref, lam_ref,
         h0_ref, o_ref, pad_ref, a0_ref, a1_ref, b0_ref, b1_ref) = refs
        st_ref = None
    else:
        (zx_ref, zy_ref, cw_ref, cb_ref, wa_ref, wx_ref, ba_ref, bx_ref, lam_ref,
         o_ref, st_ref, pad_ref, a0_ref, a1_ref, b0_ref, b1_ref) = refs
        h0_ref = None
    a_refs, b_refs = (a0_ref, a1_ref), (b0_ref, b1_ref)
    cw_ch = zx_ref.shape[1]
    ext = chunk + 2 * HALO
    n_taps = cw_ref.shape[0]
    pitch = seg_len + SEG_PAD
    segs = chunk // seg_len

    pad_ref[0:HALO, :] = jnp.zeros((HALO, cw_ch), F32)
    pad_ref[HALO + rows:, :] = jnp.zeros((HALO, cw_ch), F32)
    pad_ref[HALO:HALO + rows, :] = zx_ref[...]

    neg_sp = [-LRU_C * _softplus(-lam_ref[d:d + 1, :]) for d in range(2)]

    def seg_rows(ci, j):
        return pl.ds(pl.multiple_of(ci * (segs * pitch), 8) + j * pitch, seg_len)

    def gates(ci, carry):
        r0 = pl.multiple_of(ci * chunk, chunk)
        xe = pad_ref[pl.ds(r0, ext), :]
        t = (r0 + lax.broadcasted_iota(jnp.int32, (chunk, cw_ch), 0)) & (seq - 1)
        xc = jnp.zeros((chunk, cw_ch), F32) + cb_ref[...]
        for j in range(n_taps):
            d = j - CONV_LEFT
            if d == 0:
                xs = xe[HALO:HALO + chunk]
            else:
                xs = pltpu.roll(xe, (-d) % ext, axis=0)[HALO:HALO + chunk]
                xs = jnp.where((t >= -d) if d < 0 else (t < seq - d), xs, 0.0)
            xc = xc + xs * cw_ref[j:j + 1, :]
        xcb = xc.astype(BF16)
        for d in range(2):
            r = jax.nn.sigmoid(jnp.dot(xcb, wa_ref[d], preferred_element_type=F32)
                               + ba_ref[d:d + 1, :])
            i = jax.nn.sigmoid(jnp.dot(xcb, wx_ref[d], preferred_element_type=F32)
                               + bx_ref[d:d + 1, :])
            a = jnp.exp(r * neg_sp[d])
            bterm = jnp.sqrt(1.0 - a * a) * (i * xc)
            for j in range(segs):
                a_refs[d][seg_rows(ci, j), :] = a[j * seg_len:(j + 1) * seg_len]
                b_refs[d][seg_rows(ci, j), :] = bterm[j * seg_len:(j + 1) * seg_len]
        return carry

    lax.fori_loop(0, rows // chunk, gates, 0)

    def scan(direction):
        def body(s, carry):
            h, p = carry
            pos = s if direction == 0 else seg_len - 1 - s
            idx = pl.ds(pos, nseg, stride=pitch)
            a = a_refs[direction][idx, :]
            h = a * h + b_refs[direction][idx, :]
            b_refs[direction][idx, :] = h
            if chained:
                p = a * p
                a_refs[direction][idx, :] = p
            return h, p
        init = (jnp.zeros((nseg, cw_ch), F32), jnp.ones((nseg, cw_ch), F32))
        lax.fori_loop(0, seg_len, body, init, unroll=8)

    scan(0)
    scan(1)

    if chained:
        carry = h0_ref[0:1, :]
        for q in range(nseg):
            sl = slice(q * pitch, q * pitch + seg_len)
            hq = b_refs[0][sl, :] + a_refs[0][sl, :] * carry
            b_refs[0][sl, :] = hq
            carry = hq[seg_len - 1:seg_len, :]
        carry = h0_ref[1:2, :]
        for q in reversed(range(nseg)):
            sl = slice(q * pitch, q * pitch + seg_len)
            hq = b_refs[1][sl, :] + a_refs[1][sl, :] * carry
            b_refs[1][sl, :] = hq
            carry = hq[0:1, :]
    else:
        st_ref[0] = b_refs[0][pl.ds(seq - 1, nseg, stride=pitch), :]
        st_ref[1] = b_refs[1][pl.ds(0, nseg, stride=pitch), :]

    def emit(ci, carry):
        r0 = pl.multiple_of(ci * chunk, chunk)
        for j in range(segs):
            sl = pl.ds(r0 + j * seg_len, seg_len)
            hsum = b_refs[0][seg_rows(ci, j), :] + b_refs[1][seg_rows(ci, j), :]
            o_ref[sl, :] = (hsum * jax.nn.gelu(zy_ref[sl, :])).astype(o_ref.dtype)
        return carry

    lax.fori_loop(0, rows // chunk, emit, 0)


def _lru(z, layer, conv_w, conv_b, wa, wx, ba, bx, lam, h0, *, row0, rows, ngroups,
         seq, nseg, seg_len, col_x, col_y, bw, into=None):
    cb = wa.shape[-1]
    ncb = bw // cb
    chained = h0 is not None
    chunk = max(min(256, rows), seg_len)
    assert chunk % seg_len == 0 and rows % chunk == 0 and seg_len % 8 == 0
    assert chained or seg_len == seq
    rb0 = row0 // rows
    cx0, cy0 = col_x // cb, col_y // cb
    in_specs = [
        pl.BlockSpec((rows, cb), lambda g, c: (rb0 + g, cx0 + c)),
        pl.BlockSpec((rows, cb), lambda g, c: (rb0 + g, cy0 + c)),
        pl.BlockSpec((None, conv_w.shape[1], cb), lambda g, c: (layer, 0, c)),
        pl.BlockSpec((None, 1, cb), lambda g, c: (layer, 0, c)),
        pl.BlockSpec((None, 2, None, cb, cb), lambda g, c: (layer, 0, c, 0, 0)),
        pl.BlockSpec((None, 2, None, cb, cb), lambda g, c: (layer, 0, c, 0, 0)),
        pl.BlockSpec((None, 2, cb), lambda g, c: (layer, 0, c)),
        pl.BlockSpec((None, 2, cb), lambda g, c: (layer, 0, c)),
        pl.BlockSpec((None, 2, cb), lambda g, c: (layer, 0, c)),
    ]
    args = [z, z, conv_w, conv_b.reshape(conv_b.shape[0], 1, bw), wa, wx, ba, bx, lam]
    out_shape = [jax.ShapeDtypeStruct((z.shape[0], bw), BF16)]
    out_specs = [pl.BlockSpec((rows, cb), lambda g, c: (rb0 + g, c))]
    if chained:
        in_specs.append(pl.BlockSpec((None, 2, cb), lambda g, c: (g, 0, c)))
        args.append(h0)
    else:
        assert ngroups == 1
        out_shape.append(jax.ShapeDtypeStruct((2, rows // seq, bw), F32))
        out_specs.append(pl.BlockSpec((2, rows // seq, cb), lambda g, c: (0, 0, c)))
    kern = functools.partial(_lru_kernel, seq=seq, rows=rows, nseg=nseg, seg_len=seg_len,
                             chained=chained, chunk=chunk)
    body, aliases = _fill(kern, in_specs, args, into)
    res = pl.pallas_call(
        body,
        out_shape=out_shape,
        grid=(ngroups, ncb),
        in_specs=in_specs,
        out_specs=out_specs,
        scratch_shapes=[pltpu.VMEM((rows + 2 * HALO, cb), F32)]
                       + [pltpu.VMEM((nseg * (seg_len + SEG_PAD), cb), F32)] * 4,
        input_output_aliases=aliases,
        compiler_params=_params(("arbitrary", "arbitrary"), 48),
        name="rglru",
    )(*args)
    return res if not chained else res[0]


def _rope_tables(seq, hd):
    rows = seq // GRID_W
    row = jnp.repeat(jnp.arange(rows), GRID_W).astype(F32)
    col = jnp.tile(jnp.arange(GRID_W), rows).astype(F32)
    ax = hd // 2
    inv = ROPE_THETA ** (-jnp.arange(0, ax, 2, dtype=F32) / ax)
    ang_r = row[:, None] * inv[None, :]
    ang_c = col[:, None] * inv[None, :]
    cos = jnp.concatenate([jnp.cos(ang_r)] * 2 + [jnp.cos(ang_c)] * 2, axis=-1)
    sin = jnp.concatenate([-jnp.sin(ang_r), jnp.sin(ang_r),
                           -jnp.sin(ang_c), jnp.sin(ang_c)], axis=-1)
    return cos, sin


def _rotate(x, cos, sin):
    hd = x.shape[1]
    q = hd // 4
    lane = lax.broadcasted_iota(jnp.int32, x.shape, 1)
    ahead = pltpu.roll(x, hd - q, axis=1)
    behind = pltpu.roll(x, q, axis=1)
    partner = jnp.where((lane & q) == 0, ahead, behind)
    return x * cos + partner * sin


def _attn_kernel(*refs, seq, cache_len, tq, tk, grp, hd, rope, emit_kv):
    refs = list(refs)
    zq_ref, zk_ref, zv_ref, qn_ref, kn_ref = refs[:5]
    pos = 5
    if rope:
        cq_ref, sq_ref, ck_ref, sk_ref = refs[pos:pos + 4]
        pos += 4
    if cache_len:
        kc_ref, vc_ref = refs[pos:pos + 2]
        pos += 2
    o_ref = refs[pos]
    pos += 1
    if emit_kv:
        ko_ref, vo_ref = refs[pos:pos + 2]
        pos += 2
    ks_ref, vt_ref = refs[pos:pos + 2]

    @pl.when(pl.program_id(2) == 0)
    def _():
        k = _rms(zk_ref[...], kn_ref[...])
        v = zv_ref[...]
        if emit_kv:
            ko_ref[...] = k
            vo_ref[...] = v
        if rope:
            k = _rotate(k, ck_ref[...], sk_ref[...])
        ks_ref[0:seq, :] = k.astype(BF16)
        vt_ref[:, 0:seq] = v.T.astype(BF16)
        if cache_len:
            ks_ref[seq:seq + cache_len, :] = kc_ref[...].astype(BF16)
            vt_ref[:, seq:seq + cache_len] = vc_ref[...].T.astype(BF16)

    scale = hd ** -0.5
    qs = []
    for hh in range(grp):
        q = _rms(zq_ref[:, hh * hd:(hh + 1) * hd], qn_ref[...])
        if rope:
            q = _rotate(q, cq_ref[...], sq_ref[...])
        qs.append((q * scale).astype(BF16))
    q = jnp.concatenate(qs, axis=0)
    n = grp * tq
    m = jnp.full((1, n), -jnp.inf, F32)
    l = jnp.zeros((1, n), F32)
    acc = jnp.zeros((hd, n), F32)
    for ci in range((seq + cache_len) // tk):
        kc = ks_ref[ci * tk:(ci + 1) * tk, :]
        s = lax.dot_general(kc, q, (((1,), (1,)), ((), ())),
                            preferred_element_type=F32)
        m_new = jnp.maximum(m, jnp.max(s, axis=0, keepdims=True))
        alpha = jnp.exp(m - m_new)
        p = jnp.exp(s - m_new)
        l = alpha * l + jnp.sum(p, axis=0, keepdims=True)
        acc = alpha * acc + jnp.dot(vt_ref[:, ci * tk:(ci + 1) * tk], p.astype(BF16),
                                    preferred_element_type=F32)
        m = m_new
    out = (acc / l).T
    for hh in range(grp):
        o_ref[:, hh * hd:(hh + 1) * hd] = out[hh * tq:(hh + 1) * tq].astype(o_ref.dtype)


def _attention(z, layer, q_norm, k_norm, *, row0, nb, seq, n_kv, grp, hd, col_q, col_k, col_v,
               tq, tk, rope_tabs=None, cache=None, emit_kv=False, into=None):
    qw = grp * hd
    cache_len = 0 if cache is None else cache[0].shape[2]
    nq = seq // tq
    rbq0, rbk0 = row0 // tq, row0 // seq
    cq0, ck0, cv0 = col_q // qw, col_k // hd, col_v // hd
    in_specs = [
        pl.BlockSpec((tq, qw), lambda b, h, i: (rbq0 + b * nq + i, cq0 + h)),
        pl.BlockSpec((seq, hd), lambda b, h, i: (rbk0 + b, ck0 + h)),
        pl.BlockSpec((seq, hd), lambda b, h, i: (rbk0 + b, cv0 + h)),
        pl.BlockSpec((None, 1, hd), lambda b, h, i: (layer, 0, 0)),
        pl.BlockSpec((None, 1, hd), lambda b, h, i: (layer, 0, 0)),
    ]
    args = [z, z, z, q_norm.reshape(-1, 1, hd), k_norm.reshape(-1, 1, hd)]
    if rope_tabs is not None:
        cos, sin = rope_tabs
        in_specs += [pl.BlockSpec((tq, hd), lambda b, h, i: (i, 0)),
                     pl.BlockSpec((tq, hd), lambda b, h, i: (i, 0)),
                     pl.BlockSpec((seq, hd), lambda b, h, i: (0, 0)),
                     pl.BlockSpec((seq, hd), lambda b, h, i: (0, 0))]
        args += [cos, sin, cos, sin]
    if cache is not None:
        in_specs += [pl.BlockSpec((None, None, cache_len, hd), lambda b, h, i: (b, h, 0, 0)),
                     pl.BlockSpec((None, None, cache_len, hd), lambda b, h, i: (b, h, 0, 0))]
        args += list(cache)
    out_shape = [jax.ShapeDtypeStruct((z.shape[0], n_kv * qw), BF16)]
    out_specs = [pl.BlockSpec((tq, qw), lambda b, h, i: (rbq0 + b * nq + i, h))]
    if emit_kv:
        out_shape += [jax.ShapeDtypeStruct((nb * seq, n_kv * hd), F32)] * 2
        out_specs += [pl.BlockSpec((seq, hd), lambda b, h, i: (b, h))] * 2
    kern = functools.partial(_attn_kernel, seq=seq, cache_len=cache_len, tq=tq, tk=tk, grp=grp,
                             hd=hd, rope=rope_tabs is not None, emit_kv=emit_kv)
    body, aliases = _fill(kern, in_specs, args, into)
    res = pl.pallas_call(
        body,
        out_shape=out_shape,
        grid=(nb, n_kv, nq),
        in_specs=in_specs,
        out_specs=out_specs,
        scratch_shapes=[pltpu.VMEM((seq + cache_len, hd), BF16),
                        pltpu.VMEM((hd, seq + cache_len), BF16)],
        input_output_aliases=aliases,
        compiler_params=_params(("arbitrary", "arbitrary", "arbitrary"), 48),
        name="attention",
    )(*args)
    return res if emit_kv else res[0]


def _pool_kernel(u_ref, w_ref, sc_ref, o_ref, pad_ref, *, rows, chunk, seq_p, seq_s, groups_p):
    gw = u_ref.shape[1]
    ext = chunk + 2 * HALO
    seq = jnp.where(pl.program_id(0) < groups_p, seq_p, seq_s)
    pad_ref[0:HALO, :] = jnp.zeros((HALO, gw), F32)
    pad_ref[HALO + rows:, :] = jnp.zeros((HALO, gw), F32)
    pad_ref[HALO:HALO + rows, :] = u_ref[...]

    for gi, win in enumerate(POOL_WINDOWS):
        hw = win // 2

        @pl.when(pl.program_id(1) == gi)
        def _(hw=hw):
            def body(ci, carry):
                r0 = pl.multiple_of(ci * chunk, chunk)
                xe = pad_ref[pl.ds(r0, ext), :]
                t = (r0 + lax.broadcasted_iota(jnp.int32, (chunk, gw), 0)) & (seq - 1)
                x0 = xe[HALO:HALO + chunk]
                acc = x0
                for d in range(-hw, hw):
                    if d == 0:
                        continue
                    xs = pltpu.roll(xe, (-d) % ext, axis=0)[HALO:HALO + chunk]
                    acc = acc + jnp.where((t >= -d) if d < 0 else (t < seq - d), xs, 0.0)
                cnt = (jnp.minimum(t + hw, seq) - jnp.maximum(t - hw, 0)).astype(F32)
                pooled = acc / cnt - x0
                y = jnp.dot(pooled.astype(BF16), w_ref[...], preferred_element_type=F32)
                o_ref[pl.ds(r0, chunk), :] = (y * sc_ref[...]).astype(o_ref.dtype)
                return carry

            lax.fori_loop(0, rows // chunk, body, 0)


def _pool(z, layer, pool_w, pool_scale, *, rows, groups_p, seq_p, seq_s, col0, bw):
    r = z.shape[0]
    g = pool_w.shape[1]
    gw = bw // g
    cb0 = col0 // gw
    chunk = min(256, rows)
    return pl.pallas_call(
        functools.partial(_pool_kernel, rows=rows, chunk=chunk, seq_p=seq_p, seq_s=seq_s,
                          groups_p=groups_p),
        out_shape=jax.ShapeDtypeStruct((r, bw), BF16),
        grid=(r // rows, g),
        in_specs=[pl.BlockSpec((rows, gw), lambda i, j: (i, cb0 + j)),
                  pl.BlockSpec((None, None, gw, gw), lambda i, j: (layer, j, 0, 0)),
                  pl.BlockSpec((None, 1, gw), lambda i, j: (layer, 0, j))],
        out_specs=pl.BlockSpec((rows, gw), lambda i, j: (i, j)),
        scratch_shapes=[pltpu.VMEM((rows + 2 * HALO, gw), F32)],
        compiler_params=_params(("arbitrary", "arbitrary"), 40),
        name="pool",
    )(z, pool_w, pool_scale.reshape(pool_scale.shape[0], 1, bw))


def _merge_kernel(h_ref, g0, g1, g2, g3, o0, o1, o2, o3, wb_ref, out_ref):
    h = h_ref[...]
    acc = None
    for n, (g_ref, o_ref) in enumerate(((g0, o0), (g1, o1), (g2, o2), (g3, o3))):
        gate = jax.nn.sigmoid(jnp.dot(h, g_ref[...], preferred_element_type=F32))
        y = jnp.dot(o_ref[...], wb_ref[n], preferred_element_type=F32)
        acc = gate * y if acc is None else acc + gate * y
    out_ref[...] = acc.astype(out_ref.dtype)


def _gate_merge(h, w_in, layer, col_g, branches, w_branch, tm=512, tn=256):
    m, d = h.shape
    bw = w_branch.shape[2]
    dm = w_branch.shape[3]
    tm, tn = min(tm, m), min(tn, dm)
    gspecs = [pl.BlockSpec((None, d, tn),
                           functools.partial(lambda j, i, n: (layer, 0, (col_g + n * dm) // tn + j), n=n))
              for n in range(N_BRANCH)]
    ospecs = [pl.BlockSpec((tm, bw), lambda j, i: (i, 0)) for _ in range(N_BRANCH)]
    return pl.pallas_call(
        _merge_kernel,
        out_shape=jax.ShapeDtypeStruct((m, dm), BF16),
        grid=(dm // tn, m // tm),
        in_specs=[pl.BlockSpec((tm, d), lambda j, i: (i, 0))] + gspecs + ospecs
                 + [pl.BlockSpec((None, N_BRANCH, bw, tn), lambda j, i: (layer, 0, 0, j))],
        out_specs=pl.BlockSpec((tm, tn), lambda j, i: (i, j)),
        compiler_params=_params(("arbitrary", "arbitrary"), 52),
        name="gate_merge",
    )(h, w_in, w_in, w_in, w_in, *branches, w_branch)


def _postnorm_kernel(x_ref, y_ref, g_ref, gate_ref, o_ref):
    o_ref[...] = x_ref[...] + gate_ref[...] * _rms(y_ref[...], g_ref[...])


def _postnorm(rng, y, gain, mod_l, k_gate, p_rows, s_rows, tm, into=None):
    src, src_row0, uni_row0, nrows = rng
    r, d = y.shape
    st0, ut0 = src_row0 // tm, uni_row0 // tm
    in_specs = [pl.BlockSpec((tm, d), lambda i: (st0 + i, 0)),
                pl.BlockSpec((tm, d), lambda i: (ut0 + i, 0)),
                pl.BlockSpec((1, d), lambda i: (0, 0)),
                _mod_spec(k_gate, d, tm, p_rows, s_rows, ut0)]
    args = [src, y, gain.reshape(1, d), mod_l]
    body, aliases = _fill(_postnorm_kernel, in_specs, args, into)
    return pl.pallas_call(
        body,
        out_shape=jax.ShapeDtypeStruct((r, d), F32),
        grid=(nrows // tm,),
        in_specs=in_specs,
        out_specs=pl.BlockSpec((tm, d), lambda i: (ut0 + i, 0)),
        input_output_aliases=aliases,
        compiler_params=_params(("arbitrary",), 40),
        name="postnorm",
    )(*args)


def _ffn_kernel(x_ref, gpre_ref, sh_ref, sc_ref, gate_ref, gpost_ref, wg_ref, wu_ref, wd_ref,
                o_ref, h_ref, acc_ref):
    k = pl.program_id(1)

    @pl.when(k == 0)
    def _():
        h = _rms(x_ref[...], gpre_ref[...])
        h_ref[...] = (h * (1.0 + sc_ref[...]) + sh_ref[...]).astype(BF16)
        acc_ref[...] = jnp.zeros_like(acc_ref)

    h = h_ref[...]
    a = jnp.dot(h, wg_ref[...], preferred_element_type=F32)
    b = jnp.dot(h, wu_ref[...], preferred_element_type=F32)
    act = (a * jax.nn.sigmoid(a) * b).astype(BF16)
    acc_ref[...] += jnp.dot(act, wd_ref[...], preferred_element_type=F32)

    @pl.when(k == pl.num_programs(1) - 1)
    def _():
        o_ref[...] = x_ref[...] + gate_ref[...] * _rms(acc_ref[...], gpost_ref[...])


def _ffn(x, row0, nrows, layer, g_pre, g_post, mod_l, wg, wu, wd, p_rows, s_rows, tm=512, tc=256):
    d = x.shape[1]
    dff = wg.shape[2]
    tm, tc = min(tm, nrows), min(tc, dff)
    t0 = row0 // tm
    return pl.pallas_call(
        _ffn_kernel,
        out_shape=jax.ShapeDtypeStruct((nrows, d), F32),
        grid=(nrows // tm, dff // tc),
        in_specs=[pl.BlockSpec((tm, d), lambda i, k: (t0 + i, 0), pipeline_mode=pl.Buffered(1)),
                  pl.BlockSpec((1, d), lambda i, k: (0, 0)),
                  _mod_spec(3, d, tm, p_rows, s_rows, t0),
                  _mod_spec(4, d, tm, p_rows, s_rows, t0),
                  _mod_spec(5, d, tm, p_rows, s_rows, t0),
                  pl.BlockSpec((1, d), lambda i, k: (0, 0)),
                  pl.BlockSpec((None, d, tc), lambda i, k: (layer, 0, k)),
                  pl.BlockSpec((None, d, tc), lambda i, k: (layer, 0, k)),
                  pl.BlockSpec((None, tc, d), lambda i, k: (layer, k, 0))],
        out_specs=pl.BlockSpec((tm, d), lambda i, k: (i, 0)),
        scratch_shapes=[pltpu.VMEM((tm, d), BF16), pltpu.VMEM((tm, d), F32)],
        compiler_params=_params(("arbitrary", "arbitrary"), 60),
        name="ffn",
    )(x, g_pre.reshape(1, d), mod_l, mod_l, mod_l, g_post.reshape(1, d), wg, wu, wd)


def kernel(x_prompt, x_sample, c, cache_k, cache_v, state_lru, c_ctx, w_mod, b_mod, g_pre_mix, g_post_mix, g_pre_ffn, g_post_ffn, w_in, conv_w, conv_b, lru_wa, lru_ba, lru_wx, lru_bx, lru_lambda, q_norm, k_norm, pool_w, pool_scale, w_branch, w_out, w_ffn_gate, w_ffn_up, w_ffn_down):
    nbp, seq_p, d = x_prompt.shape
    nbs, seq_s, _ = x_sample.shape
    depth = w_in.shape[0]
    bw = w_branch.shape[2]
    cache_len, n_kv, hd = cache_k.shape[2:]
    grp = bw // hd // n_kv
    kvw = n_kv * hd
    p_rows, s_rows = nbp * seq_p, nbs * seq_s
    rows = p_rows + s_rows
    assert seq_p & (seq_p - 1) == 0 and seq_s & (seq_s - 1) == 0
    assert p_rows % seq_s == 0, "row groups of seq_s rows must tile the prompt rows"
    col_a, col_bx, col_by, col_q = 0, bw, 2 * bw, 3 * bw
    col_k = col_q + bw
    col_v = col_k + kvw
    col_d = col_v + kvw
    col_g = col_d + bw
    zw = col_g

    def rows_tile(pref):
        return math.gcd(pref, math.gcd(p_rows, seq_s))

    w_in_b = w_in.astype(BF16)
    w_branch_b = w_branch.astype(BF16)
    w_out_b = w_out.astype(BF16)
    wg_b = w_ffn_gate.astype(BF16)
    wu_b = w_ffn_up.astype(BF16)
    wd_b = w_ffn_down.astype(BF16)
    wa_b = lru_wa.astype(BF16)
    wx_b = lru_wx.astype(BF16)
    pool_w_b = pool_w.astype(BF16)

    gw = bw // FOURIER_GROUPS
    cc, sc = _dft_tables(gw)
    tab_c = jnp.concatenate([cc, sc], axis=1).astype(BF16)
    cp, sp = _dft_tables(seq_p)
    dft_p = jnp.concatenate([cp, -sp], axis=1).astype(BF16)
    cs, ss = _dft_tables(seq_s)
    dft_s = jnp.concatenate([cs, -ss], axis=1).astype(BF16)
    rope_tabs = _rope_tables(seq_s, hd)

    cvec = jnp.concatenate([c_ctx[None, :], c, jnp.zeros((8 - 1 - nbs, d), c.dtype)], axis=0)
    mod = _mod_all(cvec.astype(F32), w_mod, b_mod)

    cache_kt = jnp.transpose(cache_k, (0, 1, 3, 2, 4)).astype(F32)
    cache_vt = jnp.transpose(cache_v, (0, 1, 3, 2, 4)).astype(F32)

    x_ranges = [(x_prompt.reshape(p_rows, d).astype(F32), 0, 0, p_rows),
                (x_sample.reshape(s_rows, d).astype(F32), 0, p_rows, s_rows)]
    tm_norm = rows_tile(256)
    k_list, v_list, s_list = [], [], []
    y_p = y_s = None
    for l in range(depth):
        mod_l = mod[l].reshape(8, 1, 6 * d)
        h = None
        for rng in x_ranges:
            h = _prenorm(rng, rows, g_pre_mix[l], mod_l, 0, p_rows, seq_s, tm_norm, into=h)
        z = _matmul_layer(h, w_in_b, l, zw, rows_tile(1024), 512, F32, 48, "in_proj")

        pq_p = _fourier_stage1(z, tab_c, 0, nbp, seq_p, bw, col_a)
        o_a = _matmul_batched_rhs(dft_p, pq_p, 512, 1024, BF16, 48, "fourier2p", rows, 0)
        pq_s = _fourier_stage1(z, tab_c, p_rows, nbs, seq_s, bw, col_a)
        o_a = _matmul_batched_rhs(dft_s, pq_s, 512, 512, BF16, 48, "fourier2s", rows, p_rows,
                                  into=o_a)

        lru_args = (conv_w, conv_b, wa_b, wx_b, lru_ba, lru_bx, lru_lambda)
        o_b, st = _lru(z, l, *lru_args, None, row0=0, rows=p_rows, ngroups=1, seq=seq_p,
                       nseg=nbp, seg_len=seq_p, col_x=col_bx, col_y=col_by, bw=bw)
        seg = min(128, seq_s)
        o_b = _lru(z, l, *lru_args, state_lru[:, l].astype(F32), row0=p_rows, rows=seq_s,
                   ngroups=nbs, seq=seq_s, nseg=seq_s // seg, seg_len=seg,
                   col_x=col_bx, col_y=col_by, bw=bw, into=o_b)
        s_list.append(jnp.transpose(st, (1, 0, 2)))

        att = dict(n_kv=n_kv, grp=grp, hd=hd, col_q=col_q, col_k=col_k, col_v=col_v)
        o_c, k_p, v_p = _attention(z, l, q_norm, k_norm, row0=0, nb=nbp, seq=seq_p, tq=seq_p,
                                   tk=seq_p, emit_kv=True, **att)
        lk = seq_s + cache_len
        tk = next(t for t in (1152, 1024, 512, 256, 128, lk) if lk % t == 0)
        o_c = _attention(z, l, q_norm, k_norm, row0=p_rows, nb=nbs, seq=seq_s,
                         tq=min(128, seq_s), tk=tk, rope_tabs=rope_tabs,
                         cache=(cache_kt[:, l], cache_vt[:, l]), into=o_c, **att)
        k_list.append(k_p.reshape(nbp, seq_p, n_kv, hd))
        v_list.append(v_p.reshape(nbp, seq_p, n_kv, hd))

        o_d = _pool(z, l, pool_w_b, pool_scale, rows=seq_s, groups_p=p_rows // seq_s,
                    seq_p=seq_p, seq_s=seq_s, col0=col_d, bw=bw)

        merged = _gate_merge(h, w_in_b, l, col_g, (o_a, o_b, o_c, o_d), w_branch_b,
                             tm=rows_tile(512))
        mix = _matmul_layer(merged, w_out_b, l, d, rows_tile(1024), 1024, F32, 52, "out_proj")
        x = None
        for rng in x_ranges:
            x = _postnorm(rng, mix, g_post_mix[l], mod_l, 2, p_rows, seq_s, tm_norm, into=x)
        ffn = functools.partial(_ffn, x, layer=l, g_pre=g_pre_ffn[l], g_post=g_post_ffn[l],
                                mod_l=mod_l, wg=wg_b, wu=wu_b, wd=wd_b, p_rows=p_rows,
                                s_rows=seq_s, tm=rows_tile(512))
        if l + 1 < depth:
            x_ranges = [(ffn(0, rows), 0, 0, rows)]
        else:
            y_p, y_s = ffn(0, p_rows), ffn(p_rows, s_rows)

    y_p = y_p.reshape(nbp, seq_p, d).astype(x_prompt.dtype)
    y_s = y_s.reshape(nbs, seq_s, d).astype(x_sample.dtype)
    new_k = jnp.stack(k_list, axis=1).astype(cache_k.dtype)
    new_v = jnp.stack(v_list, axis=1).astype(cache_v.dtype)
    new_s = jnp.stack(s_list, axis=1).astype(state_lru.dtype)
    return y_p, y_s, new_k, new_v, new_s
```

```python
import functools
import math

import jax
import jax.numpy as jnp
from jax import lax
from jax.experimental import pallas as pl
from jax.experimental.pallas import tpu as pltpu

F32 = jnp.float32
BF16 = jnp.bfloat16
MIB = 1024 * 1024

EPS = 1e-6
GRID_W = 64
ROPE_THETA = 10000.0
LRU_C = 8.0
FOURIER_GROUPS = 4
POOL_WINDOWS = (2, 4, 8, 16)
N_BRANCH = 4
HALO = 8
CONV_LEFT = 2
FFN_CHUNK = 512
SEG_PAD = 8


def _params(semantics, vmem_mib):
    return pltpu.CompilerParams(dimension_semantics=semantics,
                                vmem_limit_bytes=vmem_mib * MIB)


def _rms(x, gain):
    ms = jnp.mean(x * x, axis=-1, keepdims=True)
    return x * lax.rsqrt(ms + EPS) * gain


def _sigmoid(x):
    return 0.5 * jnp.tanh(0.5 * x) + 0.5


def _softplus(x):
    return jnp.maximum(x, 0.0) + jnp.log1p(jnp.exp(-jnp.abs(x)))


def _group_of_tile(i, tm, p_rows, s_rows):
    r = i * tm
    return jnp.where(r < p_rows, 0, 1 + (r - p_rows) // s_rows)


def _mod_spec(k, d, tm, p_rows, s_rows, tile0=0):
    return pl.BlockSpec((None, 1, d),
                        lambda i, *_: (_group_of_tile(tile0 + i, tm, p_rows, s_rows), 0, k))


def _fill(kernel, in_specs, args, into):
    if into is None:
        return kernel, {}
    n = len(args)
    in_specs.append(pl.BlockSpec(memory_space=pl.ANY))
    args.append(into)

    def body(*refs):
        return kernel(*refs[:n], *refs[n + 1:])
    return body, {n: 0}


def _mod_kernel(c_ref, w_ref, b_ref, o_ref):
    c = c_ref[...]
    s = (c * jax.nn.sigmoid(c)).astype(BF16)
    o_ref[...] = jnp.dot(s, w_ref[...].astype(BF16),
                         preferred_element_type=F32) + b_ref[...]


def _mod_all(cvec, w_mod, b_mod, tn=512):
    depth, d, n = w_mod.shape
    return pl.pallas_call(
        _mod_kernel,
        out_shape=jax.ShapeDtypeStruct((depth, 8, n), F32),
        grid=(depth, n // tn),
        in_specs=[pl.BlockSpec((8, d), lambda l, j: (0, 0)),
                  pl.BlockSpec((None, d, tn), lambda l, j: (l, 0, j)),
                  pl.BlockSpec((None, 1, tn), lambda l, j: (l, 0, j))],
        out_specs=pl.BlockSpec((None, 8, tn), lambda l, j: (l, 0, j)),
        compiler_params=_params(("arbitrary", "arbitrary"), 40),
        name="mod",
    )(cvec, w_mod, b_mod.reshape(depth, 1, n))


def _prenorm_kernel(x_ref, g_ref, sh_ref, sc_ref, o_ref):
    h = _rms(x_ref[...], g_ref[...])
    o_ref[...] = (h * (1.0 + sc_ref[...]) + sh_ref[...]).astype(o_ref.dtype)


def _prenorm(rng, total_rows, gain, mod_l, k_shift, p_rows, s_rows, tm, into=None):
    src, src_row0, uni_row0, nrows = rng
    d = src.shape[1]
    st0, ut0 = src_row0 // tm, uni_row0 // tm
    in_specs = [pl.BlockSpec((tm, d), lambda i: (st0 + i, 0)),
                pl.BlockSpec((1, d), lambda i: (0, 0)),
                _mod_spec(k_shift, d, tm, p_rows, s_rows, ut0),
                _mod_spec(k_shift + 1, d, tm, p_rows, s_rows, ut0)]
    args = [src, gain.reshape(1, d), mod_l, mod_l]
    body, aliases = _fill(_prenorm_kernel, in_specs, args, into)
    return pl.pallas_call(
        body,
        out_shape=jax.ShapeDtypeStruct((total_rows, d), BF16),
        grid=(nrows // tm,),
        in_specs=in_specs,
        out_specs=pl.BlockSpec((tm, d), lambda i: (ut0 + i, 0)),
        input_output_aliases=aliases,
        compiler_params=_params(("arbitrary",), 40),
        name="prenorm",
    )(*args)


def _mm_kernel(a_ref, b_ref, o_ref):
    o_ref[...] = jnp.dot(a_ref[...], b_ref[...],
                         preferred_element_type=F32).astype(o_ref.dtype)


def _matmul_layer(a, w, layer, n_cols, tm, tn, out_dtype, vmem_mib, name):
    m, k = a.shape
    tm, tn = min(tm, m), min(tn, n_cols)
    return pl.pallas_call(
        _mm_kernel,
        out_shape=jax.ShapeDtypeStruct((m, n_cols), out_dtype),
        grid=(m // tm, n_cols // tn),
        in_specs=[pl.BlockSpec((tm, k), lambda i, j: (i, 0)),
                  pl.BlockSpec((None, k, tn), lambda i, j: (layer, 0, j))],
        out_specs=pl.BlockSpec((tm, tn), lambda i, j: (i, j)),
        compiler_params=_params(("arbitrary", "arbitrary"), vmem_mib),
        name=name,
    )(a, w)


def _matmul_batched_rhs(a, b, tm, tn, out_dtype, vmem_mib, name, total_rows, row0, into=None):
    m, k = a.shape
    nb, _, n = b.shape
    tm, tn = min(tm, m), min(tn, n)
    t0, nt = row0 // tm, m // tm
    in_specs = [pl.BlockSpec((tm, k), lambda bi, i, j: (i, 0)),
                pl.BlockSpec((None, k, tn), lambda bi, i, j: (bi, 0, j))]
    args = [a, b]
    body, aliases = _fill(_mm_kernel, in_specs, args, into)
    return pl.pallas_call(
        body,
        out_shape=jax.ShapeDtypeStruct((total_rows, n), out_dtype),
        grid=(nb, nt, n // tn),
        in_specs=in_specs,
        out_specs=pl.BlockSpec((tm, tn), lambda bi, i, j: (t0 + bi * nt + i, j)),
        input_output_aliases=aliases,
        compiler_params=_params(("arbitrary", "arbitrary", "arbitrary"), vmem_mib),
        name=name,
    )(*args)


def _dft_tables(n, split=64):
    s = 1.0 / math.sqrt(n)
    k = jnp.arange(n, dtype=jnp.int32)

    def thin(rows, period):
        r = jnp.arange(rows, dtype=jnp.int32)
        ang = ((r[:, None] * k[None, :]) % period).astype(F32) * (2.0 * math.pi / period)
        return jnp.cos(ang), jnp.sin(ang)

    if n <= split or n % split:
        c, sn = thin(n, n)
        return c * s, sn * s
    ca, sa = thin(n // split, n // split)
    cb, sb = thin(split, n)
    ca, sa = ca[:, None, :] * s, sa[:, None, :] * s
    cos = (ca * cb[None] - sa * sb[None]).reshape(n, n)
    sin = (sa * cb[None] + ca * sb[None]).reshape(n, n)
    return cos, sin


def _four1_kernel(u_ref, t_ref, o_ref, *, seq, gw):
    u = u_ref[...].astype(BF16)
    pq = jnp.dot(u, t_ref[...], preferred_element_type=F32)
    o_ref[0:seq, :] = pq[:, :gw].astype(o_ref.dtype)
    o_ref[seq:2 * seq, :] = pq[:, gw:].astype(o_ref.dtype)


def _fourier_stage1(z, tab, row0, nb, seq, bw, col0):
    gw = bw // FOURIER_GROUPS
    rb0, cb0 = row0 // seq, col0 // gw
    return pl.pallas_call(
        functools.partial(_four1_kernel, seq=seq, gw=gw),
        out_shape=jax.ShapeDtypeStruct((nb, 2 * seq, bw), BF16),
        grid=(nb, FOURIER_GROUPS),
        in_specs=[pl.BlockSpec((seq, gw), lambda b, g: (rb0 + b, cb0 + g)),
                  pl.BlockSpec((gw, 2 * gw), lambda b, g: (0, 0))],
        out_specs=pl.BlockSpec((None, 2 * seq, gw), lambda b, g: (b, 0, g)),
        compiler_params=_params(("arbitrary", "arbitrary"), 40),
        name="fourier1",
    )(z, tab)


def _lru_kernel(*refs, seq, rows, nseg, seg_len, chained, chunk):
    if chained:
        (zx_ref, zy_ref, cw_ref, cb_ref, wa_ref, wx_ref, ba_ref, bx_ref, lam_ref,
         h0_ref, o_ref, pad_ref, a0_ref, a1_ref, b0_ref, b1_ref) = refs
        st_ref = None
    else:
        (zx_ref, zy_ref, cw_ref, cb_ref, wa_ref, wx_ref, ba_ref, bx_ref, lam_ref,
         o_ref, st_ref, pad_ref, a0_ref, a1_ref, b0_ref, b1_ref) = refs
        h0_ref = None
    a_refs, b_refs = (a0_ref, a1_ref), (b0_ref, b1_ref)
    cw_ch = zx_ref.shape[1]
    ext = chunk + 2 * HALO
    n_taps = cw_ref.shape[0]
    pitch = seg_len + SEG_PAD
    segs = chunk // seg_len

    pad_ref[0:HALO, :] = jnp.zeros((HALO, cw_ch), F32)
    pad_ref[HALO + rows:, :] = jnp.zeros((HALO, cw_ch), F32)
    pad_ref[HALO:HALO + rows, :] = zx_ref[...]

    neg_sp = [-LRU_C * _softplus(-lam_ref[d:d + 1, :]) for d in range(2)]

    def seg_rows(ci, j):
        return pl.ds(pl.multiple_of(ci * (segs * pitch), 8) + j * pitch, seg_len)

    def gates(ci, carry):
        r0 = pl.multiple_of(ci * chunk, chunk)
        xe = pad_ref[pl.ds(r0, ext), :]
        t = (r0 + lax.broadcasted_iota(jnp.int32, (chunk, cw_ch), 0)) & (seq - 1)
        xc = jnp.zeros((chunk, cw_ch), F32) + cb_ref[...]
        for j in range(n_taps):
            d = j - CONV_LEFT
            if d == 0:
                xs = xe[HALO:HALO + chunk]
            else:
                xs = pltpu.roll(xe, (-d) % ext, axis=0)[HALO:HALO + chunk]
                xs = jnp.where((t >= -d) if d < 0 else (t < seq - d), xs, 0.0)
            xc = xc + xs * cw_ref[j:j + 1, :]
        xcb = xc.astype(BF16)
        for d in range(2):
            r = _sigmoid(jnp.dot(xcb, wa_ref[d], preferred_element_type=F32)
                         + ba_ref[d:d + 1, :])
            i = _sigmoid(jnp.dot(xcb, wx_ref[d], preferred_element_type=F32)
                         + bx_ref[d:d + 1, :])
            a = jnp.exp(r * neg_sp[d])
            bterm = jnp.sqrt(1.0 - a * a) * (i * xc)
            for j in range(segs):
                a_refs[d][seg_rows(ci, j), :] = a[j * seg_len:(j + 1) * seg_len]
                b_refs[d][seg_rows(ci, j), :] = bterm[j * seg_len:(j + 1) * seg_len]
        return carry

    lax.fori_loop(0, rows // chunk, gates, 0)

    def scan(direction):
        def body(s, carry):
            h, p = carry
            pos = s if direction == 0 else seg_len - 1 - s
            idx = pl.ds(pos, nseg, stride=pitch)
            a = a_refs[direction][idx, :]
            h = a * h + b_refs[direction][idx, :]
            b_refs[direction][idx, :] = h
            if chained:
                p = a * p
                a_refs[direction][idx, :] = p
            return h, p
        init = (jnp.zeros((nseg, cw_ch), F32), jnp.ones((nseg, cw_ch), F32))
        lax.fori_loop(0, seg_len, body, init, unroll=8)

    scan(0)
    scan(1)

    if chained:
        carry = h0_ref[0:1, :]
        for q in range(nseg):
            sl = slice(q * pitch, q * pitch + seg_len)
            hq = b_refs[0][sl, :] + a_refs[0][sl, :] * carry
            b_refs[0][sl, :] = hq
            carry = hq[seg_len - 1:seg_len, :]
        carry = h0_ref[1:2, :]
        for q in reversed(range(nseg)):
            sl = slice(q * pitch, q * pitch + seg_len)
            hq = b_refs[1][sl, :] + a_refs[1][sl, :] * carry
            b_refs[1][sl, :] = hq
            carry = hq[0:1, :]
    else:
        st_ref[0] = b_refs[0][pl.ds(seq - 1, nseg, stride=pitch), :]
        st_ref[1] = b_refs[1][pl.ds(0, nseg, stride=pitch), :]

    def emit(ci, carry):
        r0 = pl.multiple_of(ci * chunk, chunk)
        for j in range(segs):
            sl = pl.ds(r0 + j * seg_len, seg_len)
            hsum = b_refs[0][seg_rows(ci, j), :] + b_refs[1][seg_rows(ci, j), :]
            o_ref[sl, :] = (hsum * jax.nn.gelu(zy_ref[sl, :])).astype(o_ref.dtype)
        return carry

    lax.fori_loop(0, rows // chunk, emit, 0)


def _lru(z, layer, conv_w, conv_b, wa, wx, ba, bx, lam, h0, *, row0, rows, ngroups,
         seq, nseg, seg_len, col_x, col_y, bw, into=None):
    cb = wa.shape[-1]
    ncb = bw // cb
    chained = h0 is not None
    chunk = max(min(256, rows), seg_len)
    assert chunk % seg_len == 0 and rows % chunk == 0 and seg_len % 8 == 0
    assert chained or seg_len == seq
    rb0 = row0 // rows
    cx0, cy0 = col_x // cb, col_y // cb
    in_specs = [
        pl.BlockSpec((rows, cb), lambda g, c: (rb0 + g, cx0 + c)),
        pl.BlockSpec((rows, cb), lambda g, c: (rb0 + g, cy0 + c)),
        pl.BlockSpec((None, conv_w.shape[1], cb), lambda g, c: (layer, 0, c)),
        pl.BlockSpec((None, 1, cb), lambda g, c: (layer, 0, c)),
        pl.BlockSpec((None, 2, None, cb, cb), lambda g, c: (layer, 0, c, 0, 0)),
        pl.BlockSpec((None, 2, None, cb, cb), lambda g, c: (layer, 0, c, 0, 0)),
        pl.BlockSpec((None, 2, cb), lambda g, c: (layer, 0, c)),
        pl.BlockSpec((None, 2, cb), lambda g, c: (layer, 0, c)),
        pl.BlockSpec((None, 2, cb), lambda g, c: (layer, 0, c)),
    ]
    args = [z, z, conv_w, conv_b.reshape(conv_b.shape[0], 1, bw), wa, wx, ba, bx, lam]
    out_shape = [jax.ShapeDtypeStruct((z.shape[0], bw), BF16)]
    out_specs = [pl.BlockSpec((rows, cb), lambda g, c: (rb0 + g, c))]
    if chained:
        in_specs.append(pl.BlockSpec((None, 2, cb), lambda g, c: (g, 0, c)))
        args.append(h0)
    else:
        assert ngroups == 1
        out_shape.append(jax.ShapeDtypeStruct((2, rows // seq, bw), F32))
        out_specs.append(pl.BlockSpec((2, rows // seq, cb), lambda g, c: (0, 0, c)))
    kern = functools.partial(_lru_kernel, seq=seq, rows=rows, nseg=nseg, seg_len=seg_len,
                             chained=chained, chunk=chunk)
    body, aliases = _fill(kern, in_specs, args, into)
    res = pl.pallas_call(
        body,
        out_shape=out_shape,
        grid=(ngroups, ncb),
        in_specs=in_specs,
        out_specs=out_specs,
        scratch_shapes=[pltpu.VMEM((rows + 2 * HALO, cb), F32)]
                       + [pltpu.VMEM((nseg * (seg_len + SEG_PAD), cb), F32)] * 4,
        input_output_aliases=aliases,
        compiler_params=_params(("arbitrary", "arbitrary"), 48),
        name="rglru",
    )(*args)
    return res if not chained else res[0]


def _rope_tables(seq, hd):
    rows = seq // GRID_W
    row = jnp.repeat(jnp.arange(rows), GRID_W).astype(F32)
    col = jnp.tile(jnp.arange(GRID_W), rows).astype(F32)
    ax = hd // 2
    inv = ROPE_THETA ** (-jnp.arange(0, ax, 2, dtype=F32) / ax)
    ang_r = row[:, None] * inv[None, :]
    ang_c = col[:, None] * inv[None, :]
    cos = jnp.concatenate([jnp.cos(ang_r)] * 2 + [jnp.cos(ang_c)] * 2, axis=-1)
    sin = jnp.concatenate([-jnp.sin(ang_r), jnp.sin(ang_r),
                           -jnp.sin(ang_c), jnp.sin(ang_c)], axis=-1)
    return cos, sin


def _rotate(x, cos, sin):
    hd = x.shape[1]
    q = hd // 4
    lane = lax.broadcasted_iota(jnp.int32, x.shape, 1)
    ahead = pltpu.roll(x, hd - q, axis=1)
    behind = pltpu.roll(x, q, axis=1)
    partner = jnp.where((lane & q) == 0, ahead, behind)
    return x * cos + partner * sin


def _attn_kernel(*refs, seq, cache_len, tq, tk, grp, hd, rope, emit_kv):
    refs = list(refs)
    zq_ref, zk_ref, zv_ref, qn_ref, kn_ref = refs[:5]
    pos = 5
    if rope:
        cq_ref, sq_ref, ck_ref, sk_ref = refs[pos:pos + 4]
        pos += 4
    if cache_len:
        kc_ref, vc_ref = refs[pos:pos + 2]
        pos += 2
    o_ref = refs[pos]
    pos += 1
    if emit_kv:
        ko_ref, vo_ref = refs[pos:pos + 2]
        pos += 2
    ks_ref, vt_ref = refs[pos:pos + 2]

    @pl.when(pl.program_id(2) == 0)
    def _():
        k = _rms(zk_ref[...], kn_ref[...])
        v = zv_ref[...]
        if emit_kv:
            ko_ref[...] = k
            vo_ref[...] = v
        if rope:
            k = _rotate(k, ck_ref[...], sk_ref[...])
        ks_ref[0:seq, :] = k.astype(BF16)
        vt_ref[:, 0:seq] = v.T.astype(BF16)
        if cache_len:
            ks_ref[seq:seq + cache_len, :] = kc_ref[...].astype(BF16)
            vt_ref[:, seq:seq + cache_len] = vc_ref[...].T.astype(BF16)

    scale = hd ** -0.5 * math.log2(math.e)
    qs = []
    for hh in range(grp):
        q = _rms(zq_ref[:, hh * hd:(hh + 1) * hd], qn_ref[...])
        if rope:
            q = _rotate(q, cq_ref[...], sq_ref[...])
        qs.append((q * scale).astype(BF16))
    q = jnp.concatenate(qs, axis=0)
    n = grp * tq
    m = jnp.full((1, n), -jnp.inf, F32)
    l = jnp.zeros((1, n), F32)
    acc = jnp.zeros((hd, n), F32)
    for ci in range((seq + cache_len) // tk):
        kc = ks_ref[ci * tk:(ci + 1) * tk, :]
        s = lax.dot_general(kc, q, (((1,), (1,)), ((), ())),
                            preferred_element_type=F32)
        m_new = jnp.maximum(m, jnp.max(s, axis=0, keepdims=True))
        alpha = jnp.exp2(m - m_new)
        p = jnp.exp2(s - m_new)
        l = alpha * l + jnp.sum(p, axis=0, keepdims=True)
        acc = alpha * acc + jnp.dot(vt_ref[:, ci * tk:(ci + 1) * tk], p.astype(BF16),
                                    preferred_element_type=F32)
        m = m_new
    out = (acc / l).T
    for hh in range(grp):
        o_ref[:, hh * hd:(hh + 1) * hd] = out[hh * tq:(hh + 1) * tq].astype(o_ref.dtype)


def _attention(z, layer, q_norm, k_norm, *, row0, nb, seq, n_kv, grp, hd, col_q, col_k, col_v,
               tq, tk, rope_tabs=None, cache=None, emit_kv=False, into=None):
    qw = grp * hd
    cache_len = 0 if cache is None else cache[0].shape[2]
    nq = seq // tq
    rbq0, rbk0 = row0 // tq, row0 // seq
    cq0, ck0, cv0 = col_q // qw, col_k // hd, col_v // hd
    in_specs = [
        pl.BlockSpec((tq, qw), lambda b, h, i: (rbq0 + b * nq + i, cq0 + h)),
        pl.BlockSpec((seq, hd), lambda b, h, i: (rbk0 + b, ck0 + h)),
        pl.BlockSpec((seq, hd), lambda b, h, i: (rbk0 + b, cv0 + h)),
        pl.BlockSpec((None, 1, hd), lambda b, h, i: (layer, 0, 0)),
        pl.BlockSpec((None, 1, hd), lambda b, h, i: (layer, 0, 0)),
    ]
    args = [z, z, z, q_norm.reshape(-1, 1, hd), k_norm.reshape(-1, 1, hd)]
    if rope_tabs is not None:
        cos, sin = rope_tabs
        in_specs += [pl.BlockSpec((tq, hd), lambda b, h, i: (i, 0)),
                     pl.BlockSpec((tq, hd), lambda b, h, i: (i, 0)),
                     pl.BlockSpec((seq, hd), lambda b, h, i: (0, 0)),
                     pl.BlockSpec((seq, hd), lambda b, h, i: (0, 0))]
        args += [cos, sin, cos, sin]
    if cache is not None:
        in_specs += [pl.BlockSpec((None, None, cache_len, hd), lambda b, h, i: (b, h, 0, 0)),
                     pl.BlockSpec((None, None, cache_len, hd), lambda b, h, i: (b, h, 0, 0))]
        args += list(cache)
    out_shape = [jax.ShapeDtypeStruct((z.shape[0], n_kv * qw), BF16)]
    out_specs = [pl.BlockSpec((tq, qw), lambda b, h, i: (rbq0 + b * nq + i, h))]
    if emit_kv:
        out_shape += [jax.ShapeDtypeStruct((nb * seq, n_kv * hd), F32)] * 2
        out_specs += [pl.BlockSpec((seq, hd), lambda b, h, i: (b, h))] * 2
    kern = functools.partial(_attn_kernel, seq=seq, cache_len=cache_len, tq=tq, tk=tk, grp=grp,
                             hd=hd, rope=rope_tabs is not None, emit_kv=emit_kv)
    body, aliases = _fill(kern, in_specs, args, into)
    res = pl.pallas_call(
        body,
        out_shape=out_shape,
        grid=(nb, n_kv, nq),
        in_specs=in_specs,
        out_specs=out_specs,
        scratch_shapes=[pltpu.VMEM((seq + cache_len, hd), BF16),
                        pltpu.VMEM((hd, seq + cache_len), BF16)],
        input_output_aliases=aliases,
        compiler_params=_params(("arbitrary", "arbitrary", "arbitrary"), 48),
        name="attention",
    )(*args)
    return res if emit_kv else res[0]


def _pool_kernel(u_ref, w_ref, sc_ref, o_ref, pad_ref, *, rows, chunk, seq_p, seq_s, groups_p):
    gw = u_ref.shape[1]
    ext = chunk + 2 * HALO
    seq = jnp.where(pl.program_id(0) < groups_p, seq_p, seq_s)
    pad_ref[0:HALO, :] = jnp.zeros((HALO, gw), F32)
    pad_ref[HALO + rows:, :] = jnp.zeros((HALO, gw), F32)
    pad_ref[HALO:HALO + rows, :] = u_ref[...]

    for gi, win in enumerate(POOL_WINDOWS):
        hw = win // 2

        @pl.when(pl.program_id(1) == gi)
        def _(hw=hw):
            def body(ci, carry):
                r0 = pl.multiple_of(ci * chunk, chunk)
                xe = pad_ref[pl.ds(r0, ext), :]
                t = (r0 + lax.broadcasted_iota(jnp.int32, (chunk, gw), 0)) & (seq - 1)
                x0 = xe[HALO:HALO + chunk]
                acc = x0
                for d in range(-hw, hw):
                    if d == 0:
                        continue
                    xs = pltpu.roll(xe, (-d) % ext, axis=0)[HALO:HALO + chunk]
                    acc = acc + jnp.where((t >= -d) if d < 0 else (t < seq - d), xs, 0.0)
                cnt = (jnp.minimum(t + hw, seq) - jnp.maximum(t - hw, 0)).astype(F32)
                pooled = acc / cnt - x0
                y = jnp.dot(pooled.astype(BF16), w_ref[...], preferred_element_type=F32)
                o_ref[pl.ds(r0, chunk), :] = (y * sc_ref[...]).astype(o_ref.dtype)
                return carry

            lax.fori_loop(0, rows // chunk, body, 0)


def _pool(z, layer, pool_w, pool_scale, *, rows, groups_p, seq_p, seq_s, col0, bw):
    r = z.shape[0]
    g = pool_w.shape[1]
    gw = bw // g
    cb0 = col0 // gw
    chunk = min(256, rows)
    return pl.pallas_call(
        functools.partial(_pool_kernel, rows=rows, chunk=chunk, seq_p=seq_p, seq_s=seq_s,
                          groups_p=groups_p),
        out_shape=jax.ShapeDtypeStruct((r, bw), BF16),
        grid=(r // rows, g),
        in_specs=[pl.BlockSpec((rows, gw), lambda i, j: (i, cb0 + j)),
                  pl.BlockSpec((None, None, gw, gw), lambda i, j: (layer, j, 0, 0)),
                  pl.BlockSpec((None, 1, gw), lambda i, j: (layer, 0, j))],
        out_specs=pl.BlockSpec((rows, gw), lambda i, j: (i, j)),
        scratch_shapes=[pltpu.VMEM((rows + 2 * HALO, gw), F32)],
        compiler_params=_params(("arbitrary", "arbitrary"), 40),
        name="pool",
    )(z, pool_w, pool_scale.reshape(pool_scale.shape[0], 1, bw))


def _merge_kernel(h_ref, g0, g1, g2, g3, o0, o1, o2, o3, wb_ref, out_ref):
    h = h_ref[...]
    acc = None
    for n, (g_ref, o_ref) in enumerate(((g0, o0), (g1, o1), (g2, o2), (g3, o3))):
        gate = _sigmoid(jnp.dot(h, g_ref[...], preferred_element_type=F32))
        y = jnp.dot(o_ref[...], wb_ref[n], preferred_element_type=F32)
        acc = gate * y if acc is None else acc + gate * y
    out_ref[...] = acc.astype(out_ref.dtype)


def _gate_merge(h, w_in, layer, col_g, branches, w_branch, tm=512, tn=256):
    m, d = h.shape
    bw = w_branch.shape[2]
    dm = w_branch.shape[3]
    tm, tn = min(tm, m), min(tn, dm)
    gspecs = [pl.BlockSpec((None, d, tn),
                           functools.partial(lambda j, i, n: (layer, 0, (col_g + n * dm) // tn + j), n=n))
              for n in range(N_BRANCH)]
    ospecs = [pl.BlockSpec((tm, bw), lambda j, i: (i, 0)) for _ in range(N_BRANCH)]
    return pl.pallas_call(
        _merge_kernel,
        out_shape=jax.ShapeDtypeStruct((m, dm), BF16),
        grid=(dm // tn, m // tm),
        in_specs=[pl.BlockSpec((tm, d), lambda j, i: (i, 0))] + gspecs + ospecs
                 + [pl.BlockSpec((None, N_BRANCH, bw, tn), lambda j, i: (layer, 0, 0, j))],
        out_specs=pl.BlockSpec((tm, tn), lambda j, i: (i, j)),
        compiler_params=_params(("arbitrary", "arbitrary"), 52),
        name="gate_merge",
    )(h, w_in, w_in, w_in, w_in, *branches, w_branch)


def _out_residual_kernel(m_ref, w_ref, x_ref, g_ref, gate_ref, o_ref):
    y = jnp.dot(m_ref[...], w_ref[...], preferred_element_type=F32)
    o_ref[...] = x_ref[...] + gate_ref[...] * _rms(y, g_ref[...])


def _out_residual(rng, merged, w_out, layer, gain, mod_l, k_gate, p_rows, s_rows, tm, into=None):
    src, src_row0, uni_row0, nrows = rng
    r, d = merged.shape
    st0, ut0 = src_row0 // tm, uni_row0 // tm
    in_specs = [pl.BlockSpec((tm, d), lambda i: (ut0 + i, 0)),
                pl.BlockSpec((None, d, d), lambda i: (layer, 0, 0), pipeline_mode=pl.Buffered(1)),
                pl.BlockSpec((tm, d), lambda i: (st0 + i, 0)),
                pl.BlockSpec((1, d), lambda i: (0, 0)),
                _mod_spec(k_gate, d, tm, p_rows, s_rows, ut0)]
    args = [merged, w_out, src, gain.reshape(1, d), mod_l]
    body, aliases = _fill(_out_residual_kernel, in_specs, args, into)
    return pl.pallas_call(
        body,
        out_shape=jax.ShapeDtypeStruct((r, d), F32),
        grid=(nrows // tm,),
        in_specs=in_specs,
        out_specs=pl.BlockSpec((tm, d), lambda i: (ut0 + i, 0)),
        input_output_aliases=aliases,
        compiler_params=_params(("arbitrary",), 60),
        name="out_residual",
    )(*args)


def _ffn_kernel(x_ref, gpre_ref, sh_ref, sc_ref, gate_ref, gpost_ref, wg_ref, wu_ref, wd_ref,
                o_ref, h_ref, *, slab):
    k = pl.program_id(1)
    nslab = x_ref.shape[0] // slab

    def rows(i):
        return pl.ds(pl.multiple_of(i * slab, slab), slab)

    @pl.when(k == 0)
    def _():
        def body(i, carry):
            h = _rms(x_ref[rows(i), :], gpre_ref[...])
            h_ref[rows(i), :] = (h * (1.0 + sc_ref[...]) + sh_ref[...]).astype(BF16)
            o_ref[rows(i), :] = jnp.zeros((slab, o_ref.shape[1]), F32)
            return carry
        lax.fori_loop(0, nslab, body, 0)

    h = h_ref[...]
    a = jnp.dot(h, wg_ref[...], preferred_element_type=F32)
    b = jnp.dot(h, wu_ref[...], preferred_element_type=F32)
    act = (a * _sigmoid(a) * b).astype(BF16)
    o_ref[...] += jnp.dot(act, wd_ref[...], preferred_element_type=F32)

    @pl.when(k == pl.num_programs(1) - 1)
    def _():
        def body(i, carry):
            o_ref[rows(i), :] = (x_ref[rows(i), :]
                                 + gate_ref[...] * _rms(o_ref[rows(i), :], gpost_ref[...]))
            return carry
        lax.fori_loop(0, nslab, body, 0)


def _ffn(x, row0, nrows, layer, g_pre, g_post, mod_l, wg, wu, wd, p_rows, s_rows, tm=512):
    d = x.shape[1]
    dff = wg.shape[2]
    tm, tc = min(tm, nrows), FFN_CHUNK
    assert dff % tc == 0
    t0 = row0 // tm
    return pl.pallas_call(
        functools.partial(_ffn_kernel, slab=math.gcd(32, tm)),
        out_shape=jax.ShapeDtypeStruct((nrows, d), F32),
        grid=(nrows // tm, dff // tc),
        in_specs=[pl.BlockSpec((tm, d), lambda i, k: (t0 + i, 0), pipeline_mode=pl.Buffered(1)),
                  pl.BlockSpec((1, d), lambda i, k: (0, 0)),
                  _mod_spec(3, d, tm, p_rows, s_rows, t0),
                  _mod_spec(4, d, tm, p_rows, s_rows, t0),
                  _mod_spec(5, d, tm, p_rows, s_rows, t0),
                  pl.BlockSpec((1, d), lambda i, k: (0, 0)),
                  pl.BlockSpec((None, d, tc), lambda i, k: (layer, 0, k)),
                  pl.BlockSpec((None, d, tc), lambda i, k: (layer, 0, k)),
                  pl.BlockSpec((None, tc, d), lambda i, k: (layer, k, 0))],
        out_specs=pl.BlockSpec((tm, d), lambda i, k: (i, 0)),
        scratch_shapes=[pltpu.VMEM((tm, d), BF16)],
        compiler_params=_params(("arbitrary", "arbitrary"), 60),
        name="ffn",
    )(x, g_pre.reshape(1, d), mod_l, mod_l, mod_l, g_post.reshape(1, d), wg, wu, wd)


def kernel(x_prompt, x_sample, c, cache_k, cache_v, state_lru, c_ctx, w_mod, b_mod, g_pre_mix, g_post_mix, g_pre_ffn, g_post_ffn, w_in, conv_w, conv_b, lru_wa, lru_ba, lru_wx, lru_bx, lru_lambda, q_norm, k_norm, pool_w, pool_scale, w_branch, w_out, w_ffn_gate, w_ffn_up, w_ffn_down):
    nbp, seq_p, d = x_prompt.shape
    nbs, seq_s, _ = x_sample.shape
    depth = w_in.shape[0]
    bw = w_branch.shape[2]
    cache_len, n_kv, hd = cache_k.shape[2:]
    grp = bw // hd // n_kv
    kvw = n_kv * hd
    p_rows, s_rows = nbp * seq_p, nbs * seq_s
    rows = p_rows + s_rows
    assert seq_p & (seq_p - 1) == 0 and seq_s & (seq_s - 1) == 0
    assert p_rows % seq_s == 0, "row groups of seq_s rows must tile the prompt rows"
    col_a, col_bx, col_by, col_q = 0, bw, 2 * bw, 3 * bw
    col_k = col_q + bw
    col_v = col_k + kvw
    col_d = col_v + kvw
    col_g = col_d + bw
    zw = col_g

    def rows_tile(pref):
        return math.gcd(pref, math.gcd(p_rows, seq_s))

    w_in_b = w_in.astype(BF16)
    w_branch_b = w_branch.astype(BF16)
    w_out_b = w_out.astype(BF16)
    ff_pad = -w_ffn_gate.shape[2] % FFN_CHUNK
    wg_b = jnp.pad(w_ffn_gate.astype(BF16), ((0, 0), (0, 0), (0, ff_pad)))
    wu_b = jnp.pad(w_ffn_up.astype(BF16), ((0, 0), (0, 0), (0, ff_pad)))
    wd_b = jnp.pad(w_ffn_down.astype(BF16), ((0, 0), (0, ff_pad), (0, 0)))
    wa_b = lru_wa.astype(BF16)
    wx_b = lru_wx.astype(BF16)
    pool_w_b = pool_w.astype(BF16)

    gw = bw // FOURIER_GROUPS
    cc, sc = _dft_tables(gw)
    tab_c = jnp.concatenate([cc, sc], axis=1).astype(BF16)
    cp, sp = _dft_tables(seq_p)
    dft_p = jnp.concatenate([cp, -sp], axis=1).astype(BF16)
    cs, ss = _dft_tables(seq_s)
    dft_s = jnp.concatenate([cs, -ss], axis=1).astype(BF16)
    rope_tabs = _rope_tables(seq_s, hd)

    cvec = jnp.concatenate([c_ctx[None, :], c, jnp.zeros((8 - 1 - nbs, d), c.dtype)], axis=0)
    mod = _mod_all(cvec.astype(F32), w_mod, b_mod)

    cache_kt = jnp.transpose(cache_k, (0, 1, 3, 2, 4)).astype(F32)
    cache_vt = jnp.transpose(cache_v, (0, 1, 3, 2, 4)).astype(F32)

    x_ranges = [(x_prompt.reshape(p_rows, d).astype(F32), 0, 0, p_rows),
                (x_sample.reshape(s_rows, d).astype(F32), 0, p_rows, s_rows)]
    tm_norm = rows_tile(256)
    k_list, v_list, s_list = [], [], []
    y_p = y_s = None
    for l in range(depth):
        mod_l = mod[l].reshape(8, 1, 6 * d)
        h = None
        for rng in x_ranges:
            h = _prenorm(rng, rows, g_pre_mix[l], mod_l, 0, p_rows, seq_s, tm_norm, into=h)
        z = _matmul_layer(h, w_in_b, l, zw, rows_tile(1024), 512, F32, 48, "in_proj")

        pq_p = _fourier_stage1(z, tab_c, 0, nbp, seq_p, bw, col_a)
        o_a = _matmul_batched_rhs(dft_p, pq_p, 512, 1024, BF16, 48, "fourier2p", rows, 0)
        pq_s = _fourier_stage1(z, tab_c, p_rows, nbs, seq_s, bw, col_a)
        o_a = _matmul_batched_rhs(dft_s, pq_s, 512, 512, BF16, 48, "fourier2s", rows, p_rows,
                                  into=o_a)

        lru_args = (conv_w, conv_b, wa_b, wx_b, lru_ba, lru_bx, lru_lambda)
        o_b, st = _lru(z, l, *lru_args, None, row0=0, rows=p_rows, ngroups=1, seq=seq_p,
                       nseg=nbp, seg_len=seq_p, col_x=col_bx, col_y=col_by, bw=bw)
        seg = min(128, seq_s)
        o_b = _lru(z, l, *lru_args, state_lru[:, l].astype(F32), row0=p_rows, rows=seq_s,
                   ngroups=nbs, seq=seq_s, nseg=seq_s // seg, seg_len=seg,
                   col_x=col_bx, col_y=col_by, bw=bw, into=o_b)
        s_list.append(jnp.transpose(st, (1, 0, 2)))

        att = dict(n_kv=n_kv, grp=grp, hd=hd, col_q=col_q, col_k=col_k, col_v=col_v)
        o_c, k_p, v_p = _attention(z, l, q_norm, k_norm, row0=0, nb=nbp, seq=seq_p, tq=seq_p,
                                   tk=seq_p, emit_kv=True, **att)
        lk = seq_s + cache_len
        tk = lk
        o_c = _attention(z, l, q_norm, k_norm, row0=p_rows, nb=nbs, seq=seq_s,
                         tq=min(256, seq_s), tk=tk, rope_tabs=rope_tabs,
                         cache=(cache_kt[:, l], cache_vt[:, l]), into=o_c, **att)
        k_list.append(k_p.reshape(nbp, seq_p, n_kv, hd))
        v_list.append(v_p.reshape(nbp, seq_p, n_kv, hd))

        o_d = _pool(z, l, pool_w_b, pool_scale, rows=seq_s, groups_p=p_rows // seq_s,
                    seq_p=seq_p, seq_s=seq_s, col0=col_d, bw=bw)

        merged = _gate_merge(h, w_in_b, l, col_g, (o_a, o_b, o_c, o_d), w_branch_b,
                             tm=rows_tile(512))
        x = None
        for rng in x_ranges:
            x = _out_residual(rng, merged, w_out_b, l, g_post_mix[l], mod_l, 2, p_rows, seq_s,
                              tm_norm, into=x)
        ffn = functools.partial(_ffn, x, layer=l, g_pre=g_pre_ffn[l], g_post=g_post_ffn[l],
                                mod_l=mod_l, wg=wg_b, wu=wu_b, wd=wd_b, p_rows=p_rows,
                                s_rows=seq_s, tm=rows_tile(512))
        if l + 1 < depth:
            x_ranges = [(ffn(0, rows), 0, 0, rows)]
        else:
            y_p, y_s = ffn(0, p_rows), ffn(p_rows, s_rows)

    y_p = y_p.reshape(nbp, seq_p, d).astype(x_prompt.dtype)
    y_s = y_s.reshape(nbs, seq_s, d).astype(x_sample.dtype)
    new_k = jnp.stack(k_list, axis=1).astype(cache_k.dtype)
    new_v = jnp.stack(v_list, axis=1).astype(cache_v.dtype)
    new_s = jnp.stack(s_list, axis=1).astype(state_lru.dtype)
    return y_p, y_s, new_k, new_v, new_s
```

```python
import functools
import math

import jax
import jax.numpy as jnp
from jax import lax
from jax.experimental import pallas as pl
from jax.experimental.pallas import tpu as pltpu

F32 = jnp.float32
BF16 = jnp.bfloat16
MIB = 1024 * 1024

EPS = 1e-6
GRID_W = 64
ROPE_THETA = 10000.0
LRU_C = 8.0
FOURIER_GROUPS = 4
POOL_WINDOWS = (2, 4, 8, 16)
N_BRANCH = 4
HALO = 8
CONV_LEFT = 2
FFN_CHUNK = 512
SEG_PAD = 8


def _params(semantics, vmem_mib):
    return pltpu.CompilerParams(dimension_semantics=semantics,
                                vmem_limit_bytes=vmem_mib * MIB)


def _rms(x, gain):
    ms = jnp.mean(x * x, axis=-1, keepdims=True)
    return x * lax.rsqrt(ms + EPS) * gain


def _sigmoid(x):
    return 0.5 * jnp.tanh(0.5 * x) + 0.5


def _softplus(x):
    return jnp.maximum(x, 0.0) + jnp.log1p(jnp.exp(-jnp.abs(x)))


def _group_of_tile(i, tm, p_rows, s_rows):
    r = i * tm
    return jnp.where(r < p_rows, 0, 1 + (r - p_rows) // s_rows)


def _mod_spec(k, d, tm, p_rows, s_rows, tile0=0):
    return pl.BlockSpec((None, 1, d),
                        lambda i, *_: (_group_of_tile(tile0 + i, tm, p_rows, s_rows), 0, k))


def _fill(kernel, in_specs, args, into):
    if into is None:
        return kernel, {}
    n = len(args)
    in_specs.append(pl.BlockSpec(memory_space=pl.ANY))
    args.append(into)

    def body(*refs):
        return kernel(*refs[:n], *refs[n + 1:])
    return body, {n: 0}


def _side_casts(weights, nsteps, step_of):
    in_specs, args, out_specs, out_shapes = [], [], [], []
    for w, layer in weights:
        _, rows, cols = w.shape
        br = next(b for b in range(16, rows + 1, 16) if rows % b == 0 and rows // b <= nsteps)
        last = rows // br - 1
        in_specs.append(pl.BlockSpec(
            (None, br, cols),
            functools.partial(lambda *g, layer, last: (layer, jnp.minimum(step_of(*g), last), 0),
                              layer=layer, last=last)))
        out_specs.append(pl.BlockSpec(
            (br, cols),
            functools.partial(lambda *g, last: (jnp.minimum(step_of(*g), last), 0), last=last)))
        args.append(w)
        out_shapes.append(jax.ShapeDtypeStruct((rows, cols), BF16))
    return in_specs, args, out_specs, out_shapes


def _run_side_casts(src_refs, dst_refs):
    for src, dst in zip(src_refs, dst_refs):
        dst[...] = src[...].astype(BF16)


def _mod_kernel(c_ref, w_ref, b_ref, o_ref):
    c = c_ref[...]
    s = (c * jax.nn.sigmoid(c)).astype(BF16)
    o_ref[...] = jnp.dot(s, w_ref[...].astype(BF16),
                         preferred_element_type=F32) + b_ref[...]


def _mod_all(cvec, w_mod, b_mod, tn=512):
    depth, d, n = w_mod.shape
    return pl.pallas_call(
        _mod_kernel,
        out_shape=jax.ShapeDtypeStruct((depth, 8, n), F32),
        grid=(depth, n // tn),
        in_specs=[pl.BlockSpec((8, d), lambda l, j: (0, 0)),
                  pl.BlockSpec((None, d, tn), lambda l, j: (l, 0, j)),
                  pl.BlockSpec((None, 1, tn), lambda l, j: (l, 0, j))],
        out_specs=pl.BlockSpec((None, 8, tn), lambda l, j: (l, 0, j)),
        compiler_params=_params(("arbitrary", "arbitrary"), 40),
        name="mod",
    )(cvec, w_mod, b_mod.reshape(depth, 1, n))


def _prenorm_kernel(x_ref, g_ref, sh_ref, sc_ref, o_ref):
    h = _rms(x_ref[...], g_ref[...])
    o_ref[...] = (h * (1.0 + sc_ref[...]) + sh_ref[...]).astype(o_ref.dtype)


def _prenorm(rng, total_rows, gain, mod_l, k_shift, p_rows, s_rows, tm, into=None):
    src, src_row0, uni_row0, nrows = rng
    d = src.shape[1]
    st0, ut0 = src_row0 // tm, uni_row0 // tm
    in_specs = [pl.BlockSpec((tm, d), lambda i: (st0 + i, 0)),
                pl.BlockSpec((1, d), lambda i: (0, 0)),
                _mod_spec(k_shift, d, tm, p_rows, s_rows, ut0),
                _mod_spec(k_shift + 1, d, tm, p_rows, s_rows, ut0)]
    args = [src, gain.reshape(1, d), mod_l, mod_l]
    body, aliases = _fill(_prenorm_kernel, in_specs, args, into)
    return pl.pallas_call(
        body,
        out_shape=jax.ShapeDtypeStruct((total_rows, d), BF16),
        grid=(nrows // tm,),
        in_specs=in_specs,
        out_specs=pl.BlockSpec((tm, d), lambda i: (ut0 + i, 0)),
        input_output_aliases=aliases,
        compiler_params=_params(("arbitrary",), 40),
        name="prenorm",
    )(*args)


def _mm_kernel(a_ref, b_ref, o_ref):
    o_ref[...] = jnp.dot(a_ref[...], b_ref[...],
                         preferred_element_type=F32).astype(o_ref.dtype)


def _matmul_cols(a, w, n_cols, tm, tn, out_dtype, vmem_mib, name):
    m, k = a.shape
    tm, tn = min(tm, m), min(tn, n_cols)
    return pl.pallas_call(
        _mm_kernel,
        out_shape=jax.ShapeDtypeStruct((m, n_cols), out_dtype),
        grid=(m // tm, n_cols // tn),
        in_specs=[pl.BlockSpec((tm, k), lambda i, j: (i, 0)),
                  pl.BlockSpec((k, tn), lambda i, j: (0, j))],
        out_specs=pl.BlockSpec((tm, tn), lambda i, j: (i, j)),
        compiler_params=_params(("arbitrary", "arbitrary"), vmem_mib),
        name=name,
    )(a, w)


def _matmul_batched_rhs(a, b, tm, tn, out_dtype, vmem_mib, name, total_rows, row0, into=None):
    m, k = a.shape
    nb, _, n = b.shape
    tm, tn = min(tm, m), min(tn, n)
    t0, nt = row0 // tm, m // tm
    in_specs = [pl.BlockSpec((tm, k), lambda bi, i, j: (i, 0)),
                pl.BlockSpec((None, k, tn), lambda bi, i, j: (bi, 0, j))]
    args = [a, b]
    body, aliases = _fill(_mm_kernel, in_specs, args, into)
    return pl.pallas_call(
        body,
        out_shape=jax.ShapeDtypeStruct((total_rows, n), out_dtype),
        grid=(nb, nt, n // tn),
        in_specs=in_specs,
        out_specs=pl.BlockSpec((tm, tn), lambda bi, i, j: (t0 + bi * nt + i, j)),
        input_output_aliases=aliases,
        compiler_params=_params(("arbitrary", "arbitrary", "arbitrary"), vmem_mib),
        name=name,
    )(*args)


def _dft_tables(n, split=64):
    s = 1.0 / math.sqrt(n)
    k = jnp.arange(n, dtype=jnp.int32)

    def thin(rows, period):
        r = jnp.arange(rows, dtype=jnp.int32)
        ang = ((r[:, None] * k[None, :]) % period).astype(F32) * (2.0 * math.pi / period)
        return jnp.cos(ang), jnp.sin(ang)

    if n <= split or n % split:
        c, sn = thin(n, n)
        return c * s, sn * s
    ca, sa = thin(n // split, n // split)
    cb, sb = thin(split, n)
    ca, sa = ca[:, None, :] * s, sa[:, None, :] * s
    cos = (ca * cb[None] - sa * sb[None]).reshape(n, n)
    sin = (sa * cb[None] + ca * sb[None]).reshape(n, n)
    return cos, sin


def _four1_kernel(u_ref, t_ref, o_ref, *, seq, gw):
    u = u_ref[...].astype(BF16)
    pq = jnp.dot(u, t_ref[...], preferred_element_type=F32)
    o_ref[0:seq, :] = pq[:, :gw].astype(o_ref.dtype)
    o_ref[seq:2 * seq, :] = pq[:, gw:].astype(o_ref.dtype)


def _fourier_stage1(z, tab, row0, nb, seq, bw, col0):
    gw = bw // FOURIER_GROUPS
    rb0, cb0 = row0 // seq, col0 // gw
    return pl.pallas_call(
        functools.partial(_four1_kernel, seq=seq, gw=gw),
        out_shape=jax.ShapeDtypeStruct((nb, 2 * seq, bw), BF16),
        grid=(nb, FOURIER_GROUPS),
        in_specs=[pl.BlockSpec((seq, gw), lambda b, g: (rb0 + b, cb0 + g)),
                  pl.BlockSpec((gw, 2 * gw), lambda b, g: (0, 0))],
        out_specs=pl.BlockSpec((None, 2 * seq, gw), lambda b, g: (b, 0, g)),
        compiler_params=_params(("arbitrary", "arbitrary"), 40),
        name="fourier1",
    )(z, tab)


def _lru_kernel(*refs, seq, rows, nseg, seg_len, chained, chunk):
    if chained:
        (zx_ref, zy_ref, cw_ref, cb_ref, wa_ref, wx_ref, ba_ref, bx_ref, lam_ref,
         h0_ref, o_ref, pad_ref, a0_ref, a1_ref, b0_ref, b1_ref) = refs
        st_ref = None
    else:
        (zx_ref, zy_ref, cw_ref, cb_ref, wa_ref, wx_ref, ba_ref, bx_ref, lam_ref,
         o_ref, st_ref, pad_ref, a0_ref, a1_ref, b0_ref, b1_ref) = refs
        h0_ref = None
    a_refs, b_refs = (a0_ref, a1_ref), (b0_ref, b1_ref)
    cw_ch = zx_ref.shape[1]
    ext = chunk + 2 * HALO
    n_taps = cw_ref.shape[0]
    pitch = seg_len + SEG_PAD
    segs = chunk // seg_len

    pad_ref[0:HALO, :] = jnp.zeros((HALO, cw_ch), F32)
    pad_ref[HALO + rows:, :] = jnp.zeros((HALO, cw_ch), F32)
    pad_ref[HALO:HALO + rows, :] = zx_ref[...]

    neg_sp = [-LRU_C * _softplus(-lam_ref[d:d + 1, :]) for d in range(2)]

    def seg_rows(ci, j):
        return pl.ds(pl.multiple_of(ci * (segs * pitch), 8) + j * pitch, seg_len)

    def gates(ci, carry):
        r0 = pl.multiple_of(ci * chunk, chunk)
        xe = pad_ref[pl.ds(r0, ext), :]
        t = (r0 + lax.broadcasted_iota(jnp.int32, (chunk, cw_ch), 0)) & (seq - 1)
        xc = jnp.zeros((chunk, cw_ch), F32) + cb_ref[...]
        for j in range(n_taps):
            d = j - CONV_LEFT
            if d == 0:
                xs = xe[HALO:HALO + chunk]
            else:
                xs = pltpu.roll(xe, (-d) % ext, axis=0)[HALO:HALO + chunk]
                xs = jnp.where((t >= -d) if d < 0 else (t < seq - d), xs, 0.0)
            xc = xc + xs * cw_ref[j:j + 1, :]
        xcb = xc.astype(BF16)
        for d in range(2):
            r = _sigmoid(jnp.dot(xcb, wa_ref[d], preferred_element_type=F32)
                         + ba_ref[d:d + 1, :])
            i = _sigmoid(jnp.dot(xcb, wx_ref[d], preferred_element_type=F32)
                         + bx_ref[d:d + 1, :])
            a = jnp.exp(r * neg_sp[d])
            bterm = jnp.sqrt(1.0 - a * a) * (i * xc)
            for j in range(segs):
                a_refs[d][seg_rows(ci, j), :] = a[j * seg_len:(j + 1) * seg_len]
                b_refs[d][seg_rows(ci, j), :] = bterm[j * seg_len:(j + 1) * seg_len]
        return carry

    lax.fori_loop(0, rows // chunk, gates, 0)

    def scan(direction):
        def body(s, carry):
            h, p = carry
            pos = s if direction == 0 else seg_len - 1 - s
            idx = pl.ds(pos, nseg, stride=pitch)
            a = a_refs[direction][idx, :]
            h = a * h + b_refs[direction][idx, :]
            b_refs[direction][idx, :] = h
            if chained:
                p = a * p
                a_refs[direction][idx, :] = p
            return h, p
        init = (jnp.zeros((nseg, cw_ch), F32), jnp.ones((nseg, cw_ch), F32))
        lax.fori_loop(0, seg_len, body, init, unroll=8)

    scan(0)
    scan(1)

    if chained:
        carry = h0_ref[0:1, :]
        for q in range(nseg):
            sl = slice(q * pitch, q * pitch + seg_len)
            hq = b_refs[0][sl, :] + a_refs[0][sl, :] * carry
            b_refs[0][sl, :] = hq
            carry = hq[seg_len - 1:seg_len, :]
        carry = h0_ref[1:2, :]
        for q in reversed(range(nseg)):
            sl = slice(q * pitch, q * pitch + seg_len)
            hq = b_refs[1][sl, :] + a_refs[1][sl, :] * carry
            b_refs[1][sl, :] = hq
            carry = hq[0:1, :]
    else:
        st_ref[0] = b_refs[0][pl.ds(seq - 1, nseg, stride=pitch), :]
        st_ref[1] = b_refs[1][pl.ds(0, nseg, stride=pitch), :]

    def emit(ci, carry):
        r0 = pl.multiple_of(ci * chunk, chunk)
        for j in range(segs):
            sl = pl.ds(r0 + j * seg_len, seg_len)
            hsum = b_refs[0][seg_rows(ci, j), :] + b_refs[1][seg_rows(ci, j), :]
            o_ref[sl, :] = (hsum * jax.nn.gelu(zy_ref[sl, :])).astype(o_ref.dtype)
        return carry

    lax.fori_loop(0, rows // chunk, emit, 0)


def _lru(z, layer, conv_w, conv_b, wa, wx, ba, bx, lam, h0, *, row0, rows, ngroups,
         seq, nseg, seg_len, col_x, col_y, bw, into=None):
    cb = wa.shape[-1]
    ncb = bw // cb
    chained = h0 is not None
    chunk = max(min(256, rows), seg_len)
    assert chunk % seg_len == 0 and rows % chunk == 0 and seg_len % 8 == 0
    assert chained or seg_len == seq
    rb0 = row0 // rows
    cx0, cy0 = col_x // cb, col_y // cb
    in_specs = [
        pl.BlockSpec((rows, cb), lambda g, c: (rb0 + g, cx0 + c)),
        pl.BlockSpec((rows, cb), lambda g, c: (rb0 + g, cy0 + c)),
        pl.BlockSpec((None, conv_w.shape[1], cb), lambda g, c: (layer, 0, c)),
        pl.BlockSpec((None, 1, cb), lambda g, c: (layer, 0, c)),
        pl.BlockSpec((None, 2, None, cb, cb), lambda g, c: (layer, 0, c, 0, 0)),
        pl.BlockSpec((None, 2, None, cb, cb), lambda g, c: (layer, 0, c, 0, 0)),
        pl.BlockSpec((None, 2, cb), lambda g, c: (layer, 0, c)),
        pl.BlockSpec((None, 2, cb), lambda g, c: (layer, 0, c)),
        pl.BlockSpec((None, 2, cb), lambda g, c: (layer, 0, c)),
    ]
    args = [z, z, conv_w, conv_b.reshape(conv_b.shape[0], 1, bw), wa, wx, ba, bx, lam]
    out_shape = [jax.ShapeDtypeStruct((z.shape[0], bw), BF16)]
    out_specs = [pl.BlockSpec((rows, cb), lambda g, c: (rb0 + g, c))]
    if chained:
        in_specs.append(pl.BlockSpec((None, 2, cb), lambda g, c: (g, 0, c)))
        args.append(h0)
    else:
        assert ngroups == 1
        out_shape.append(jax.ShapeDtypeStruct((2, rows // seq, bw), F32))
        out_specs.append(pl.BlockSpec((2, rows // seq, cb), lambda g, c: (0, 0, c)))
    kern = functools.partial(_lru_kernel, seq=seq, rows=rows, nseg=nseg, seg_len=seg_len,
                             chained=chained, chunk=chunk)
    body, aliases = _fill(kern, in_specs, args, into)
    res = pl.pallas_call(
        body,
        out_shape=out_shape,
        grid=(ngroups, ncb),
        in_specs=in_specs,
        out_specs=out_specs,
        scratch_shapes=[pltpu.VMEM((rows + 2 * HALO, cb), F32)]
                       + [pltpu.VMEM((nseg * (seg_len + SEG_PAD), cb), F32)] * 4,
        input_output_aliases=aliases,
        compiler_params=_params(("arbitrary", "arbitrary"), 48),
        name="rglru",
    )(*args)
    return res if not chained else res[0]


def _rope_tables(seq, hd):
    rows = seq // GRID_W
    row = jnp.repeat(jnp.arange(rows), GRID_W).astype(F32)
    col = jnp.tile(jnp.arange(GRID_W), rows).astype(F32)
    ax = hd // 2
    inv = ROPE_THETA ** (-jnp.arange(0, ax, 2, dtype=F32) / ax)
    ang_r = row[:, None] * inv[None, :]
    ang_c = col[:, None] * inv[None, :]
    cos = jnp.concatenate([jnp.cos(ang_r)] * 2 + [jnp.cos(ang_c)] * 2, axis=-1)
    sin = jnp.concatenate([-jnp.sin(ang_r), jnp.sin(ang_r),
                           -jnp.sin(ang_c), jnp.sin(ang_c)], axis=-1)
    return cos, sin


def _rotate(x, cos, sin):
    hd = x.shape[1]
    q = hd // 4
    lane = lax.broadcasted_iota(jnp.int32, x.shape, 1)
    ahead = pltpu.roll(x, hd - q, axis=1)
    behind = pltpu.roll(x, q, axis=1)
    partner = jnp.where((lane & q) == 0, ahead, behind)
    return x * cos + partner * sin


def _attn_kernel(*refs, seq, cache_len, tq, tk, grp, hd, rope, emit_kv):
    refs = list(refs)
    zq_ref, zk_ref, zv_ref, qn_ref, kn_ref = refs[:5]
    pos = 5
    if rope:
        cq_ref, sq_ref, ck_ref, sk_ref = refs[pos:pos + 4]
        pos += 4
    if cache_len:
        kc_ref, vc_ref = refs[pos:pos + 2]
        pos += 2
    o_ref = refs[pos]
    pos += 1
    if emit_kv:
        ko_ref, vo_ref = refs[pos:pos + 2]
        pos += 2
    ks_ref, vt_ref = refs[pos:pos + 2]

    @pl.when(pl.program_id(2) == 0)
    def _():
        k = _rms(zk_ref[...], kn_ref[...])
        v = zv_ref[...]
        if emit_kv:
            ko_ref[...] = k
            vo_ref[...] = v
        if rope:
            k = _rotate(k, ck_ref[...], sk_ref[...])
        ks_ref[0:seq, :] = k.astype(BF16)
        vt_ref[:, 0:seq] = v.T.astype(BF16)
        if cache_len:
            ks_ref[seq:seq + cache_len, :] = kc_ref[...].astype(BF16)
            vt_ref[:, seq:seq + cache_len] = vc_ref[...].T.astype(BF16)

    scale = hd ** -0.5 * math.log2(math.e)
    qs = []
    for hh in range(grp):
        q = _rms(zq_ref[:, hh * hd:(hh + 1) * hd], qn_ref[...])
        if rope:
            q = _rotate(q, cq_ref[...], sq_ref[...])
        qs.append((q * scale).astype(BF16))
    q = jnp.concatenate(qs, axis=0)
    n = grp * tq
    m = jnp.full((1, n), -jnp.inf, F32)
    l = jnp.zeros((1, n), F32)
    acc = jnp.zeros((hd, n), F32)
    for ci in range((seq + cache_len) // tk):
        kc = ks_ref[ci * tk:(ci + 1) * tk, :]
        s = lax.dot_general(kc, q, (((1,), (1,)), ((), ())),
                            preferred_element_type=F32)
        m_new = jnp.maximum(m, jnp.max(s, axis=0, keepdims=True))
        alpha = jnp.exp2(m - m_new)
        p = jnp.exp2(s - m_new)
        l = alpha * l + jnp.sum(p, axis=0, keepdims=True)
        acc = alpha * acc + jnp.dot(vt_ref[:, ci * tk:(ci + 1) * tk], p.astype(BF16),
                                    preferred_element_type=F32)
        m = m_new
    out = (acc / l).T
    for hh in range(grp):
        o_ref[:, hh * hd:(hh + 1) * hd] = out[hh * tq:(hh + 1) * tq].astype(o_ref.dtype)


def _attention(z, layer, q_norm, k_norm, *, row0, nb, seq, n_kv, grp, hd, col_q, col_k, col_v,
               tq, tk, rope_tabs=None, cache=None, emit_kv=False, into=None):
    qw = grp * hd
    cache_len = 0 if cache is None else cache[0].shape[2]
    nq = seq // tq
    rbq0, rbk0 = row0 // tq, row0 // seq
    cq0, ck0, cv0 = col_q // qw, col_k // hd, col_v // hd
    in_specs = [
        pl.BlockSpec((tq, qw), lambda b, h, i: (rbq0 + b * nq + i, cq0 + h)),
        pl.BlockSpec((seq, hd), lambda b, h, i: (rbk0 + b, ck0 + h)),
        pl.BlockSpec((seq, hd), lambda b, h, i: (rbk0 + b, cv0 + h)),
        pl.BlockSpec((None, 1, hd), lambda b, h, i: (layer, 0, 0)),
        pl.BlockSpec((None, 1, hd), lambda b, h, i: (layer, 0, 0)),
    ]
    args = [z, z, z, q_norm.reshape(-1, 1, hd), k_norm.reshape(-1, 1, hd)]
    if rope_tabs is not None:
        cos, sin = rope_tabs
        in_specs += [pl.BlockSpec((tq, hd), lambda b, h, i: (i, 0)),
                     pl.BlockSpec((tq, hd), lambda b, h, i: (i, 0)),
                     pl.BlockSpec((seq, hd), lambda b, h, i: (0, 0)),
                     pl.BlockSpec((seq, hd), lambda b, h, i: (0, 0))]
        args += [cos, sin, cos, sin]
    if cache is not None:
        in_specs += [pl.BlockSpec((None, None, cache_len, hd), lambda b, h, i: (b, h, 0, 0)),
                     pl.BlockSpec((None, None, cache_len, hd), lambda b, h, i: (b, h, 0, 0))]
        args += list(cache)
    out_shape = [jax.ShapeDtypeStruct((z.shape[0], n_kv * qw), BF16)]
    out_specs = [pl.BlockSpec((tq, qw), lambda b, h, i: (rbq0 + b * nq + i, h))]
    if emit_kv:
        out_shape += [jax.ShapeDtypeStruct((nb * seq, n_kv * hd), F32)] * 2
        out_specs += [pl.BlockSpec((seq, hd), lambda b, h, i: (b, h))] * 2
    kern = functools.partial(_attn_kernel, seq=seq, cache_len=cache_len, tq=tq, tk=tk, grp=grp,
                             hd=hd, rope=rope_tabs is not None, emit_kv=emit_kv)
    body, aliases = _fill(kern, in_specs, args, into)
    res = pl.pallas_call(
        body,
        out_shape=out_shape,
        grid=(nb, n_kv, nq),
        in_specs=in_specs,
        out_specs=out_specs,
        scratch_shapes=[pltpu.VMEM((seq + cache_len, hd), BF16),
                        pltpu.VMEM((hd, seq + cache_len), BF16)],
        input_output_aliases=aliases,
        compiler_params=_params(("arbitrary", "arbitrary", "arbitrary"), 48),
        name="attention",
    )(*args)
    return res if emit_kv else res[0]


def _pool_kernel(u_ref, w_ref, sc_ref, o_ref, pad_ref, *, rows, chunk, seq_p, seq_s, groups_p):
    gw = u_ref.shape[1]
    ext = chunk + 2 * HALO
    seq = jnp.where(pl.program_id(0) < groups_p, seq_p, seq_s)
    pad_ref[0:HALO, :] = jnp.zeros((HALO, gw), F32)
    pad_ref[HALO + rows:, :] = jnp.zeros((HALO, gw), F32)
    pad_ref[HALO:HALO + rows, :] = u_ref[...]

    for gi, win in enumerate(POOL_WINDOWS):
        hw = win // 2

        @pl.when(pl.program_id(1) == gi)
        def _(hw=hw):
            def body(ci, carry):
                r0 = pl.multiple_of(ci * chunk, chunk)
                xe = pad_ref[pl.ds(r0, ext), :]
                t = (r0 + lax.broadcasted_iota(jnp.int32, (chunk, gw), 0)) & (seq - 1)
                x0 = xe[HALO:HALO + chunk]
                acc = x0
                for d in range(-hw, hw):
                    if d == 0:
                        continue
                    xs = pltpu.roll(xe, (-d) % ext, axis=0)[HALO:HALO + chunk]
                    acc = acc + jnp.where((t >= -d) if d < 0 else (t < seq - d), xs, 0.0)
                cnt = (jnp.minimum(t + hw, seq) - jnp.maximum(t - hw, 0)).astype(F32)
                pooled = acc / cnt - x0
                y = jnp.dot(pooled.astype(BF16), w_ref[...], preferred_element_type=F32)
                o_ref[pl.ds(r0, chunk), :] = (y * sc_ref[...]).astype(o_ref.dtype)
                return carry

            lax.fori_loop(0, rows // chunk, body, 0)


def _pool(z, layer, pool_w, pool_scale, *, rows, groups_p, seq_p, seq_s, col0, bw):
    r = z.shape[0]
    g = pool_w.shape[1]
    gw = bw // g
    cb0 = col0 // gw
    chunk = min(256, rows)
    return pl.pallas_call(
        functools.partial(_pool_kernel, rows=rows, chunk=chunk, seq_p=seq_p, seq_s=seq_s,
                          groups_p=groups_p),
        out_shape=jax.ShapeDtypeStruct((r, bw), BF16),
        grid=(r // rows, g),
        in_specs=[pl.BlockSpec((rows, gw), lambda i, j: (i, cb0 + j)),
                  pl.BlockSpec((None, None, gw, gw), lambda i, j: (layer, j, 0, 0)),
                  pl.BlockSpec((None, 1, gw), lambda i, j: (layer, 0, j))],
        out_specs=pl.BlockSpec((rows, gw), lambda i, j: (i, j)),
        scratch_shapes=[pltpu.VMEM((rows + 2 * HALO, gw), F32)],
        compiler_params=_params(("arbitrary", "arbitrary"), 40),
        name="pool",
    )(z, pool_w, pool_scale.reshape(pool_scale.shape[0], 1, bw))


def _merge_kernel(*refs, n_cast):
    h_ref, g0, g1, g2, g3, o0, o1, o2, o3, wb_ref = refs[:10]
    out_ref = refs[10 + n_cast]
    _run_side_casts(refs[10:10 + n_cast], refs[11 + n_cast:])
    h = h_ref[...]
    acc = None
    for n, (g_ref, o_ref) in enumerate(((g0, o0), (g1, o1), (g2, o2), (g3, o3))):
        gate = _sigmoid(jnp.dot(h, g_ref[...], preferred_element_type=F32))
        y = jnp.dot(o_ref[...], wb_ref[n], preferred_element_type=F32)
        acc = gate * y if acc is None else acc + gate * y
    out_ref[...] = acc.astype(out_ref.dtype)


def _gate_merge(h, w_in, col_g, branches, w_branch, side, tm=512, tn=256):
    m, d = h.shape
    bw = w_branch.shape[1]
    dm = w_branch.shape[2]
    tm, tn = min(tm, m), min(tn, dm)
    nm = m // tm
    gspecs = [pl.BlockSpec((d, tn),
                           functools.partial(lambda j, i, n: (0, (col_g + n * dm) // tn + j), n=n))
              for n in range(N_BRANCH)]
    ospecs = [pl.BlockSpec((tm, bw), lambda j, i: (i, 0)) for _ in range(N_BRANCH)]
    c_in, c_args, c_out, c_shapes = _side_casts(side, (dm // tn) * nm, lambda j, i: j * nm + i)
    res = pl.pallas_call(
        functools.partial(_merge_kernel, n_cast=len(side)),
        out_shape=[jax.ShapeDtypeStruct((m, dm), BF16)] + c_shapes,
        grid=(dm // tn, nm),
        in_specs=[pl.BlockSpec((tm, d), lambda j, i: (i, 0))] + gspecs + ospecs
                 + [pl.BlockSpec((N_BRANCH, bw, tn), lambda j, i: (0, 0, j))] + c_in,
        out_specs=[pl.BlockSpec((tm, tn), lambda j, i: (i, j))] + c_out,
        compiler_params=_params(("arbitrary", "arbitrary"), 58),
        name="gate_merge",
    )(h, w_in, w_in, w_in, w_in, *branches, w_branch, *c_args)
    return res[0], res[1:]


def _out_residual_kernel(m_ref, w_ref, x_ref, g_ref, gate_ref, o_ref):
    y = jnp.dot(m_ref[...], w_ref[...], preferred_element_type=F32)
    o_ref[...] = x_ref[...] + gate_ref[...] * _rms(y, g_ref[...])


def _out_residual(rng, merged, w_out, gain, mod_l, k_gate, p_rows, s_rows, tm, into=None):
    src, src_row0, uni_row0, nrows = rng
    r, d = merged.shape
    st0, ut0 = src_row0 // tm, uni_row0 // tm
    in_specs = [pl.BlockSpec((tm, d), lambda i: (ut0 + i, 0)),
                pl.BlockSpec((d, d), lambda i: (0, 0), pipeline_mode=pl.Buffered(1)),
                pl.BlockSpec((tm, d), lambda i: (st0 + i, 0)),
                pl.BlockSpec((1, d), lambda i: (0, 0)),
                _mod_spec(k_gate, d, tm, p_rows, s_rows, ut0)]
    args = [merged, w_out, src, gain.reshape(1, d), mod_l]
    body, aliases = _fill(_out_residual_kernel, in_specs, args, into)
    return pl.pallas_call(
        body,
        out_shape=jax.ShapeDtypeStruct((r, d), F32),
        grid=(nrows // tm,),
        in_specs=in_specs,
        out_specs=pl.BlockSpec((tm, d), lambda i: (ut0 + i, 0)),
        input_output_aliases=aliases,
        compiler_params=_params(("arbitrary",), 60),
        name="out_residual",
    )(*args)


def _ffn_kernel(x_ref, gpre_ref, sh_ref, sc_ref, gate_ref, gpost_ref, wg_ref, wu_ref, wd_ref,
                o_ref, h_ref, *, slab, overlap):
    k = pl.program_id(1)
    nslab = x_ref.shape[0] // slab

    def rows(i):
        return pl.ds(pl.multiple_of(i * slab, slab), slab)

    @pl.when(k == 0)
    def _():
        def body(i, carry):
            h = _rms(x_ref[rows(i), :], gpre_ref[...])
            h_ref[rows(i), :] = (h * (1.0 + sc_ref[...]) + sh_ref[...]).astype(BF16)
            o_ref[rows(i), :] = jnp.zeros((slab, o_ref.shape[1]), F32)
            return carry
        lax.fori_loop(0, nslab, body, 0)

    h = h_ref[...]
    a = jnp.dot(h, wg_ref[...], preferred_element_type=F32)
    b = jnp.dot(h, wu_ref[...], preferred_element_type=F32)
    act = a * _sigmoid(a) * b
    if overlap:
        col = lax.broadcasted_iota(jnp.int32, act.shape, 1)
        fresh = jnp.logical_or(k < pl.num_programs(1) - 1, col >= overlap)
        act = jnp.where(fresh, act, 0.0)
    act = act.astype(BF16)
    o_ref[...] += jnp.dot(act, wd_ref[...], preferred_element_type=F32)

    @pl.when(k == pl.num_programs(1) - 1)
    def _():
        def body(i, carry):
            o_ref[rows(i), :] = (x_ref[rows(i), :]
                                 + gate_ref[...] * _rms(o_ref[rows(i), :], gpost_ref[...]))
            return carry
        lax.fori_loop(0, nslab, body, 0)


def _ffn(x, row0, nrows, g_pre, g_post, mod_l, wg, wu, wd, p_rows, s_rows, tm=512):
    d = x.shape[1]
    dff = wg.shape[1]
    tm, tc = min(tm, nrows), min(FFN_CHUNK, dff)
    nk = pl.cdiv(dff, tc)
    overlap = nk * tc - dff

    def chunk_start(k):
        return pl.multiple_of(jnp.minimum(k * tc, dff - tc), math.gcd(tc, dff))

    t0 = row0 // tm
    return pl.pallas_call(
        functools.partial(_ffn_kernel, slab=math.gcd(32, tm), overlap=overlap),
        out_shape=jax.ShapeDtypeStruct((nrows, d), F32),
        grid=(nrows // tm, nk),
        in_specs=[pl.BlockSpec((tm, d), lambda i, k: (t0 + i, 0), pipeline_mode=pl.Buffered(1)),
                  pl.BlockSpec((1, d), lambda i, k: (0, 0)),
                  _mod_spec(3, d, tm, p_rows, s_rows, t0),
                  _mod_spec(4, d, tm, p_rows, s_rows, t0),
                  _mod_spec(5, d, tm, p_rows, s_rows, t0),
                  pl.BlockSpec((1, d), lambda i, k: (0, 0)),
                  pl.BlockSpec((pl.Element(d), pl.Element(tc)),
                               lambda i, k: (0, chunk_start(k))),
                  pl.BlockSpec((pl.Element(d), pl.Element(tc)),
                               lambda i, k: (0, chunk_start(k))),
                  pl.BlockSpec((pl.Element(tc), pl.Element(d)),
                               lambda i, k: (chunk_start(k), 0))],
        out_specs=pl.BlockSpec((tm, d), lambda i, k: (i, 0)),
        scratch_shapes=[pltpu.VMEM((tm, d), BF16)],
        compiler_params=_params(("arbitrary", "arbitrary"), 60),
        name="ffn",
    )(x, g_pre.reshape(1, d), mod_l, mod_l, mod_l, g_post.reshape(1, d), wg, wu, wd)


def kernel(x_prompt, x_sample, c, cache_k, cache_v, state_lru, c_ctx, w_mod, b_mod, g_pre_mix, g_post_mix, g_pre_ffn, g_post_ffn, w_in, conv_w, conv_b, lru_wa, lru_ba, lru_wx, lru_bx, lru_lambda, q_norm, k_norm, pool_w, pool_scale, w_branch, w_out, w_ffn_gate, w_ffn_up, w_ffn_down):
    nbp, seq_p, d = x_prompt.shape
    nbs, seq_s, _ = x_sample.shape
    depth = w_in.shape[0]
    bw = w_branch.shape[2]
    cache_len, n_kv, hd = cache_k.shape[2:]
    grp = bw // hd // n_kv
    kvw = n_kv * hd
    p_rows, s_rows = nbp * seq_p, nbs * seq_s
    rows = p_rows + s_rows
    assert seq_p & (seq_p - 1) == 0 and seq_s & (seq_s - 1) == 0
    assert p_rows % seq_s == 0, "row groups of seq_s rows must tile the prompt rows"
    col_a, col_bx, col_by, col_q = 0, bw, 2 * bw, 3 * bw
    col_k = col_q + bw
    col_v = col_k + kvw
    col_d = col_v + kvw
    col_g = col_d + bw
    zw = col_g

    def rows_tile(pref):
        return math.gcd(pref, math.gcd(p_rows, seq_s))

    w_in_l = w_in[0].astype(BF16)
    w_branch_l = w_branch[0].astype(BF16)
    w_out_l = w_out[0].astype(BF16)
    w_branch_2d = w_branch.reshape(depth, N_BRANCH * bw, d)
    wa_b = lru_wa.astype(BF16)
    wx_b = lru_wx.astype(BF16)
    pool_w_b = pool_w.astype(BF16)

    gw = bw // FOURIER_GROUPS
    cc, sc = _dft_tables(gw)
    tab_c = jnp.concatenate([cc, sc], axis=1).astype(BF16)
    cp, sp = _dft_tables(seq_p)
    dft_p = jnp.concatenate([cp, -sp], axis=1).astype(BF16)
    cs, ss = _dft_tables(seq_s)
    dft_s = jnp.concatenate([cs, -ss], axis=1).astype(BF16)
    rope_tabs = _rope_tables(seq_s, hd)

    cvec = jnp.concatenate([c_ctx[None, :], c, jnp.zeros((8 - 1 - nbs, d), c.dtype)], axis=0)
    mod = _mod_all(cvec.astype(F32), w_mod, b_mod)

    cache_kt = jnp.transpose(cache_k, (0, 1, 3, 2, 4)).astype(F32)
    cache_vt = jnp.transpose(cache_v, (0, 1, 3, 2, 4)).astype(F32)

    x_ranges = [(x_prompt.reshape(p_rows, d).astype(F32), 0, 0, p_rows),
                (x_sample.reshape(s_rows, d).astype(F32), 0, p_rows, s_rows)]
    tm_norm = rows_tile(256)
    k_list, v_list, s_list = [], [], []
    y_p = y_s = None
    for l in range(depth):
        mod_l = mod[l].reshape(8, 1, 6 * d)
        h = None
        for rng in x_ranges:
            h = _prenorm(rng, rows, g_pre_mix[l], mod_l, 0, p_rows, seq_s, tm_norm, into=h)
        z = _matmul_cols(h, w_in_l, zw, rows_tile(1024), 512, F32, 48, "in_proj")

        pq_p = _fourier_stage1(z, tab_c, 0, nbp, seq_p, bw, col_a)
        o_a = _matmul_batched_rhs(dft_p, pq_p, 512, 1024, BF16, 48, "fourier2p", rows, 0)
        pq_s = _fourier_stage1(z, tab_c, p_rows, nbs, seq_s, bw, col_a)
        o_a = _matmul_batched_rhs(dft_s, pq_s, 512, 512, BF16, 48, "fourier2s", rows, p_rows,
                                  into=o_a)

        lru_args = (conv_w, conv_b, wa_b, wx_b, lru_ba, lru_bx, lru_lambda)
        o_b, st = _lru(z, l, *lru_args, None, row0=0, rows=p_rows, ngroups=1, seq=seq_p,
                       nseg=nbp, seg_len=seq_p, col_x=col_bx, col_y=col_by, bw=bw)
        seg = min(128, seq_s)
        o_b = _lru(z, l, *lru_args, state_lru[:, l].astype(F32), row0=p_rows, rows=seq_s,
                   ngroups=nbs, seq=seq_s, nseg=seq_s // seg, seg_len=seg,
                   col_x=col_bx, col_y=col_by, bw=bw, into=o_b)
        s_list.append(jnp.transpose(st, (1, 0, 2)))

        att = dict(n_kv=n_kv, grp=grp, hd=hd, col_q=col_q, col_k=col_k, col_v=col_v)
        o_c, k_p, v_p = _attention(z, l, q_norm, k_norm, row0=0, nb=nbp, seq=seq_p, tq=seq_p,
                                   tk=seq_p, emit_kv=True, **att)
        lk = seq_s + cache_len
        tk = lk
        o_c = _attention(z, l, q_norm, k_norm, row0=p_rows, nb=nbs, seq=seq_s,
                         tq=min(256, seq_s), tk=tk, rope_tabs=rope_tabs,
                         cache=(cache_kt[:, l], cache_vt[:, l]), into=o_c, **att)
        k_list.append(k_p.reshape(nbp, seq_p, n_kv, hd))
        v_list.append(v_p.reshape(nbp, seq_p, n_kv, hd))

        o_d = _pool(z, l, pool_w_b, pool_scale, rows=seq_s, groups_p=p_rows // seq_s,
                    seq_p=seq_p, seq_s=seq_s, col0=col_d, bw=bw)

        side = [(w_ffn_gate, l), (w_ffn_up, l), (w_ffn_down, l)]
        if l + 1 < depth:
            side += [(w_in, l + 1), (w_branch_2d, l + 1), (w_out, l + 1)]
        merged, cast = _gate_merge(h, w_in_l, col_g, (o_a, o_b, o_c, o_d), w_branch_l,
                                   side=side, tm=rows_tile(512))
        x = None
        for rng in x_ranges:
            x = _out_residual(rng, merged, w_out_l, g_post_mix[l], mod_l, 2, p_rows, seq_s,
                              tm_norm, into=x)
        ffn = functools.partial(_ffn, x, g_pre=g_pre_ffn[l], g_post=g_post_ffn[l],
                                mod_l=mod_l, wg=cast[0], wu=cast[1], wd=cast[2], p_rows=p_rows,
                                s_rows=seq_s, tm=rows_tile(512))
        if l + 1 < depth:
            w_in_l, w_branch_l, w_out_l = cast[3], cast[4].reshape(N_BRANCH, bw, d), cast[5]
            x_ranges = [(ffn(0, rows), 0, 0, rows)]
        else:
            y_p, y_s = ffn(0, p_rows), ffn(p_rows, s_rows)

    y_p = y_p.reshape(nbp, seq_p, d).astype(x_prompt.dtype)
    y_s = y_s.reshape(nbs, seq_s, d).astype(x_sample.dtype)
    new_k = jnp.stack(k_list, axis=1).astype(cache_k.dtype)
    new_v = jnp.stack(v_list, axis=1).astype(cache_v.dtype)
    new_s = jnp.stack(s_list, axis=1).astype(state_lru.dtype)
    return y_p, y_s, new_k, new_v, new_s
```

```python
import functools
import math

import jax
import jax.numpy as jnp
from jax import lax
from jax.experimental import pallas as pl
from jax.experimental.pallas import tpu as pltpu

F32 = jnp.float32
BF16 = jnp.bfloat16
MIB = 1024 * 1024

EPS = 1e-6
GRID_W = 64
ROPE_THETA = 10000.0
LRU_C = 8.0
FOURIER_GROUPS = 4
POOL_WINDOWS = (2, 4, 8, 16)
N_BRANCH = 4
HALO = 8
CONV_LEFT = 2
FFN_CHUNK = 512
SEG_PAD = 8


def _params(semantics, vmem_mib):
    return pltpu.CompilerParams(dimension_semantics=semantics,
                                vmem_limit_bytes=vmem_mib * MIB)


def _rms(x, gain):
    ms = jnp.mean(x * x, axis=-1, keepdims=True)
    return x * lax.rsqrt(ms + EPS) * gain


def _sigmoid(x):
    return 0.5 * jnp.tanh(0.5 * x) + 0.5


def _row_rinv(src_ref, stat_ref):
    n, d = src_ref.shape

    def body(i, carry):
        sl = pl.ds(pl.multiple_of(i * 8, 8), 8)
        x = src_ref[sl, :]
        ms = jnp.sum(x * x, axis=-1, keepdims=True) * (1.0 / d)
        stat_ref[sl, :] = jnp.broadcast_to(lax.rsqrt(ms + EPS), (8, stat_ref.shape[1]))
        return carry
    lax.fori_loop(0, n // 8, body, 0, unroll=8)


def _rinv_rows(stat_ref, sl, d):
    return jnp.tile(stat_ref[sl, :], (1, d // stat_ref.shape[1]))


def _softplus(x):
    return jnp.maximum(x, 0.0) + jnp.log1p(jnp.exp(-jnp.abs(x)))


def _group_of_tile(i, tm, p_rows, s_rows):
    r = i * tm
    return jnp.where(r < p_rows, 0, 1 + (r - p_rows) // s_rows)


def _mod_spec(k, d, tm, p_rows, s_rows, tile0=0):
    return pl.BlockSpec((None, 1, d),
                        lambda i, *_: (_group_of_tile(tile0 + i, tm, p_rows, s_rows), 0, k))


def _fill(kernel, in_specs, args, into):
    if into is None:
        return kernel, {}
    n = len(args)
    in_specs.append(pl.BlockSpec(memory_space=pl.ANY))
    args.append(into)

    def body(*refs):
        return kernel(*refs[:n], *refs[n + 1:])
    return body, {n: 0}


def _side_casts(weights, nsteps, step_of):
    in_specs, args, out_specs, out_shapes = [], [], [], []
    for w, layer in weights:
        _, rows, cols = w.shape
        br = next(b for b in range(16, rows + 1, 16) if rows % b == 0 and rows // b <= nsteps)
        last = rows // br - 1
        in_specs.append(pl.BlockSpec(
            (None, br, cols),
            functools.partial(lambda *g, layer, last: (layer, jnp.minimum(step_of(*g), last), 0),
                              layer=layer, last=last)))
        out_specs.append(pl.BlockSpec(
            (br, cols),
            functools.partial(lambda *g, last: (jnp.minimum(step_of(*g), last), 0), last=last)))
        args.append(w)
        out_shapes.append(jax.ShapeDtypeStruct((rows, cols), BF16))
    return in_specs, args, out_specs, out_shapes


def _run_side_casts(src_refs, dst_refs):
    for src, dst in zip(src_refs, dst_refs):
        dst[...] = src[...].astype(BF16)


def _mod_kernel(c_ref, w_ref, b_ref, o_ref):
    c = c_ref[...]
    s = (c * jax.nn.sigmoid(c)).astype(BF16)
    o_ref[...] = jnp.dot(s, w_ref[...].astype(BF16),
                         preferred_element_type=F32) + b_ref[...]


def _mod_all(cvec, w_mod, b_mod, tn=512):
    depth, d, n = w_mod.shape
    return pl.pallas_call(
        _mod_kernel,
        out_shape=jax.ShapeDtypeStruct((depth, 8, n), F32),
        grid=(depth, n // tn),
        in_specs=[pl.BlockSpec((8, d), lambda l, j: (0, 0)),
                  pl.BlockSpec((None, d, tn), lambda l, j: (l, 0, j)),
                  pl.BlockSpec((None, 1, tn), lambda l, j: (l, 0, j))],
        out_specs=pl.BlockSpec((None, 8, tn), lambda l, j: (l, 0, j)),
        compiler_params=_params(("arbitrary", "arbitrary"), 40),
        name="mod",
    )(cvec, w_mod, b_mod.reshape(depth, 1, n))


def _prenorm_kernel(x_ref, g_ref, sh_ref, sc_ref, o_ref):
    h = _rms(x_ref[...], g_ref[...])
    o_ref[...] = (h * (1.0 + sc_ref[...]) + sh_ref[...]).astype(o_ref.dtype)


def _prenorm(rng, total_rows, gain, mod_l, k_shift, p_rows, s_rows, tm, into=None):
    src, src_row0, uni_row0, nrows = rng
    d = src.shape[1]
    st0, ut0 = src_row0 // tm, uni_row0 // tm
    in_specs = [pl.BlockSpec((tm, d), lambda i: (st0 + i, 0)),
                pl.BlockSpec((1, d), lambda i: (0, 0)),
                _mod_spec(k_shift, d, tm, p_rows, s_rows, ut0),
                _mod_spec(k_shift + 1, d, tm, p_rows, s_rows, ut0)]
    args = [src, gain.reshape(1, d), mod_l, mod_l]
    body, aliases = _fill(_prenorm_kernel, in_specs, args, into)
    return pl.pallas_call(
        body,
        out_shape=jax.ShapeDtypeStruct((total_rows, d), BF16),
        grid=(nrows // tm,),
        in_specs=in_specs,
        out_specs=pl.BlockSpec((tm, d), lambda i: (ut0 + i, 0)),
        input_output_aliases=aliases,
        compiler_params=_params(("arbitrary",), 40),
        name="prenorm",
    )(*args)


def _mm_kernel(a_ref, b_ref, o_ref):
    o_ref[...] = jnp.dot(a_ref[...], b_ref[...],
                         preferred_element_type=F32).astype(o_ref.dtype)


def _mm_side_kernel(*refs, n_cast):
    a_ref, b_ref = refs[:2]
    _run_side_casts(refs[2:2 + n_cast], refs[3 + n_cast:])
    _mm_kernel(a_ref, b_ref, refs[2 + n_cast])


def _matmul_cols(a, w, n_cols, tm, tn, out_dtype, vmem_mib, name, side=()):
    m, k = a.shape
    tm, tn = min(tm, m), min(tn, n_cols)
    nn = n_cols // tn
    c_in, c_args, c_out, c_shapes = _side_casts(side, (m // tm) * nn, lambda i, j: i * nn + j)
    res = pl.pallas_call(
        functools.partial(_mm_side_kernel, n_cast=len(side)),
        out_shape=[jax.ShapeDtypeStruct((m, n_cols), out_dtype)] + c_shapes,
        grid=(m // tm, nn),
        in_specs=[pl.BlockSpec((tm, k), lambda i, j: (i, 0)),
                  pl.BlockSpec((k, tn), lambda i, j: (0, j))] + c_in,
        out_specs=[pl.BlockSpec((tm, tn), lambda i, j: (i, j))] + c_out,
        compiler_params=_params(("arbitrary", "arbitrary"), vmem_mib),
        name=name,
    )(a, w, *c_args)
    return res[0], res[1:]


def _matmul_batched_rhs(a, b, tm, tn, out_dtype, vmem_mib, name, total_rows, row0, into=None):
    m, k = a.shape
    nb, _, n = b.shape
    tm, tn = min(tm, m), min(tn, n)
    t0, nt = row0 // tm, m // tm
    in_specs = [pl.BlockSpec((tm, k), lambda bi, i, j: (i, 0)),
                pl.BlockSpec((None, k, tn), lambda bi, i, j: (bi, 0, j))]
    args = [a, b]
    body, aliases = _fill(_mm_kernel, in_specs, args, into)
    return pl.pallas_call(
        body,
        out_shape=jax.ShapeDtypeStruct((total_rows, n), out_dtype),
        grid=(nb, nt, n // tn),
        in_specs=in_specs,
        out_specs=pl.BlockSpec((tm, tn), lambda bi, i, j: (t0 + bi * nt + i, j)),
        input_output_aliases=aliases,
        compiler_params=_params(("arbitrary", "arbitrary", "arbitrary"), vmem_mib),
        name=name,
    )(*args)


def _dft_tables(n, split=64):
    s = 1.0 / math.sqrt(n)
    k = jnp.arange(n, dtype=jnp.int32)

    def thin(rows, period):
        r = jnp.arange(rows, dtype=jnp.int32)
        ang = ((r[:, None] * k[None, :]) % period).astype(F32) * (2.0 * math.pi / period)
        return jnp.cos(ang), jnp.sin(ang)

    if n <= split or n % split:
        c, sn = thin(n, n)
        return c * s, sn * s
    ca, sa = thin(n // split, n // split)
    cb, sb = thin(split, n)
    ca, sa = ca[:, None, :] * s, sa[:, None, :] * s
    cos = (ca * cb[None] - sa * sb[None]).reshape(n, n)
    sin = (sa * cb[None] + ca * sb[None]).reshape(n, n)
    return cos, sin


def _four1_kernel(u_ref, t_ref, o_ref, *, seq, gw):
    u = u_ref[...].astype(BF16)
    pq = jnp.dot(u, t_ref[...], preferred_element_type=F32)
    o_ref[0:seq, :] = pq[:, :gw].astype(o_ref.dtype)
    o_ref[seq:2 * seq, :] = pq[:, gw:].astype(o_ref.dtype)


def _fourier_stage1(z, tab, row0, nb, seq, bw, col0):
    gw = bw // FOURIER_GROUPS
    rb0, cb0 = row0 // seq, col0 // gw
    return pl.pallas_call(
        functools.partial(_four1_kernel, seq=seq, gw=gw),
        out_shape=jax.ShapeDtypeStruct((nb, 2 * seq, bw), BF16),
        grid=(nb, FOURIER_GROUPS),
        in_specs=[pl.BlockSpec((seq, gw), lambda b, g: (rb0 + b, cb0 + g)),
                  pl.BlockSpec((gw, 2 * gw), lambda b, g: (0, 0))],
        out_specs=pl.BlockSpec((None, 2 * seq, gw), lambda b, g: (b, 0, g)),
        compiler_params=_params(("arbitrary", "arbitrary"), 40),
        name="fourier1",
    )(z, tab)


def _lru_kernel(*refs, seq, rows, nseg, seg_len, chained, chunk):
    if chained:
        (zx_ref, zy_ref, cw_ref, cb_ref, wa_ref, wx_ref, ba_ref, bx_ref, lam_ref,
         h0_ref, o_ref, pad_ref, a0_ref, a1_ref, b0_ref, b1_ref) = refs
        st_ref = None
    else:
        (zx_ref, zy_ref, cw_ref, cb_ref, wa_ref, wx_ref, ba_ref, bx_ref, lam_ref,
         o_ref, st_ref, pad_ref, a0_ref, a1_ref, b0_ref, b1_ref) = refs
        h0_ref = None
    a_refs, b_refs = (a0_ref, a1_ref), (b0_ref, b1_ref)
    cw_ch = zx_ref.shape[1]
    ext = chunk + 2 * HALO
    n_taps = cw_ref.shape[0]
    pitch = seg_len + SEG_PAD
    segs = chunk // seg_len

    pad_ref[0:HALO, :] = jnp.zeros((HALO, cw_ch), F32)
    pad_ref[HALO + rows:, :] = jnp.zeros((HALO, cw_ch), F32)
    pad_ref[HALO:HALO + rows, :] = zx_ref[...]

    neg_sp = [-LRU_C * _softplus(-lam_ref[d:d + 1, :]) for d in range(2)]

    def seg_rows(ci, j):
        return pl.ds(pl.multiple_of(ci * (segs * pitch), 8) + j * pitch, seg_len)

    def gates(ci, carry):
        r0 = pl.multiple_of(ci * chunk, chunk)
        xe = pad_ref[pl.ds(r0, ext), :]
        t = (r0 + lax.broadcasted_iota(jnp.int32, (chunk, cw_ch), 0)) & (seq - 1)
        xc = jnp.zeros((chunk, cw_ch), F32) + cb_ref[...]
        for j in range(n_taps):
            d = j - CONV_LEFT
            if d == 0:
                xs = xe[HALO:HALO + chunk]
            else:
                xs = pltpu.roll(xe, (-d) % ext, axis=0)[HALO:HALO + chunk]
                xs = jnp.where((t >= -d) if d < 0 else (t < seq - d), xs, 0.0)
            xc = xc + xs * cw_ref[j:j + 1, :]
        xcb = xc.astype(BF16)
        for d in range(2):
            r = _sigmoid(jnp.dot(xcb, wa_ref[d], preferred_element_type=F32)
                         + ba_ref[d:d + 1, :])
            i = _sigmoid(jnp.dot(xcb, wx_ref[d], preferred_element_type=F32)
                         + bx_ref[d:d + 1, :])
            a = jnp.exp(r * neg_sp[d])
            bterm = jnp.sqrt(1.0 - a * a) * (i * xc)
            for j in range(segs):
                a_refs[d][seg_rows(ci, j), :] = a[j * seg_len:(j + 1) * seg_len]
                b_refs[d][seg_rows(ci, j), :] = bterm[j * seg_len:(j + 1) * seg_len]
        return carry

    lax.fori_loop(0, rows // chunk, gates, 0)

    def scan(direction):
        def body(s, carry):
            h, p = carry
            pos = s if direction == 0 else seg_len - 1 - s
            idx = pl.ds(pos, nseg, stride=pitch)
            a = a_refs[direction][idx, :]
            h = a * h + b_refs[direction][idx, :]
            b_refs[direction][idx, :] = h
            if chained:
                p = a * p
                a_refs[direction][idx, :] = p
            return h, p
        init = (jnp.zeros((nseg, cw_ch), F32), jnp.ones((nseg, cw_ch), F32))
        lax.fori_loop(0, seg_len, body, init, unroll=8)

    scan(0)
    scan(1)

    if chained:
        carry = h0_ref[0:1, :]
        for q in range(nseg):
            sl = slice(q * pitch, q * pitch + seg_len)
            hq = b_refs[0][sl, :] + a_refs[0][sl, :] * carry
            b_refs[0][sl, :] = hq
            carry = hq[seg_len - 1:seg_len, :]
        carry = h0_ref[1:2, :]
        for q in reversed(range(nseg)):
            sl = slice(q * pitch, q * pitch + seg_len)
            hq = b_refs[1][sl, :] + a_refs[1][sl, :] * carry
            b_refs[1][sl, :] = hq
            carry = hq[0:1, :]
    else:
        st_ref[0] = b_refs[0][pl.ds(seq - 1, nseg, stride=pitch), :]
        st_ref[1] = b_refs[1][pl.ds(0, nseg, stride=pitch), :]

    def emit(ci, carry):
        r0 = pl.multiple_of(ci * chunk, chunk)
        for j in range(segs):
            sl = pl.ds(r0 + j * seg_len, seg_len)
            hsum = b_refs[0][seg_rows(ci, j), :] + b_refs[1][seg_rows(ci, j), :]
            o_ref[sl, :] = (hsum * jax.nn.gelu(zy_ref[sl, :])).astype(o_ref.dtype)
        return carry

    lax.fori_loop(0, rows // chunk, emit, 0)


def _lru(z, layer, conv_w, conv_b, wa, wx, ba, bx, lam, h0, *, row0, rows, ngroups,
         seq, nseg, seg_len, col_x, col_y, bw, into=None):
    cb = wa.shape[-1]
    ncb = bw // cb
    chained = h0 is not None
    chunk = max(min(256, rows), seg_len)
    assert chunk % seg_len == 0 and rows % chunk == 0 and seg_len % 8 == 0
    assert chained or seg_len == seq
    rb0 = row0 // rows
    cx0, cy0 = col_x // cb, col_y // cb
    in_specs = [
        pl.BlockSpec((rows, cb), lambda g, c: (rb0 + g, cx0 + c)),
        pl.BlockSpec((rows, cb), lambda g, c: (rb0 + g, cy0 + c)),
        pl.BlockSpec((None, conv_w.shape[1], cb), lambda g, c: (layer, 0, c)),
        pl.BlockSpec((None, 1, cb), lambda g, c: (layer, 0, c)),
        pl.BlockSpec((None, 2, None, cb, cb), lambda g, c: (layer, 0, c, 0, 0)),
        pl.BlockSpec((None, 2, None, cb, cb), lambda g, c: (layer, 0, c, 0, 0)),
        pl.BlockSpec((None, 2, cb), lambda g, c: (layer, 0, c)),
        pl.BlockSpec((None, 2, cb), lambda g, c: (layer, 0, c)),
        pl.BlockSpec((None, 2, cb), lambda g, c: (layer, 0, c)),
    ]
    args = [z, z, conv_w, conv_b.reshape(conv_b.shape[0], 1, bw), wa, wx, ba, bx, lam]
    out_shape = [jax.ShapeDtypeStruct((z.shape[0], bw), BF16)]
    out_specs = [pl.BlockSpec((rows, cb), lambda g, c: (rb0 + g, c))]
    if chained:
        in_specs.append(pl.BlockSpec((None, 2, cb), lambda g, c: (g, 0, c)))
        args.append(h0)
    else:
        assert ngroups == 1
        out_shape.append(jax.ShapeDtypeStruct((2, rows // seq, bw), F32))
        out_specs.append(pl.BlockSpec((2, rows // seq, cb), lambda g, c: (0, 0, c)))
    kern = functools.partial(_lru_kernel, seq=seq, rows=rows, nseg=nseg, seg_len=seg_len,
                             chained=chained, chunk=chunk)
    body, aliases = _fill(kern, in_specs, args, into)
    res = pl.pallas_call(
        body,
        out_shape=out_shape,
        grid=(ngroups, ncb),
        in_specs=in_specs,
        out_specs=out_specs,
        scratch_shapes=[pltpu.VMEM((rows + 2 * HALO, cb), F32)]
                       + [pltpu.VMEM((nseg * (seg_len + SEG_PAD), cb), F32)] * 4,
        input_output_aliases=aliases,
        compiler_params=_params(("arbitrary", "arbitrary"), 48),
        name="rglru",
    )(*args)
    return res if not chained else res[0]


def _rope_tables(seq, hd):
    rows = seq // GRID_W
    row = jnp.repeat(jnp.arange(rows), GRID_W).astype(F32)
    col = jnp.tile(jnp.arange(GRID_W), rows).astype(F32)
    ax = hd // 2
    inv = ROPE_THETA ** (-jnp.arange(0, ax, 2, dtype=F32) / ax)
    ang_r = row[:, None] * inv[None, :]
    ang_c = col[:, None] * inv[None, :]
    cos = jnp.concatenate([jnp.cos(ang_r)] * 2 + [jnp.cos(ang_c)] * 2, axis=-1)
    sin = jnp.concatenate([-jnp.sin(ang_r), jnp.sin(ang_r),
                           -jnp.sin(ang_c), jnp.sin(ang_c)], axis=-1)
    return cos, sin


def _rotate(x, cos, sin):
    hd = x.shape[1]
    q = hd // 4
    lane = lax.broadcasted_iota(jnp.int32, x.shape, 1)
    ahead = pltpu.roll(x, hd - q, axis=1)
    behind = pltpu.roll(x, q, axis=1)
    partner = jnp.where((lane & q) == 0, ahead, behind)
    return x * cos + partner * sin


def _attn_kernel(*refs, seq, cache_len, tq, grp, hd, rope, emit_kv):
    refs = list(refs)
    zq_ref, zk_ref, zv_ref, qn_ref, kn_ref = refs[:5]
    pos = 5
    if rope:
        cq_ref, sq_ref, ck_ref, sk_ref = refs[pos:pos + 4]
        pos += 4
    if cache_len:
        kc_ref, vc_ref = refs[pos:pos + 2]
        pos += 2
    o_ref = refs[pos]
    pos += 1
    if emit_kv:
        ko_ref, vo_ref = refs[pos:pos + 2]
        pos += 2
    ks_ref, vt_ref = refs[pos:pos + 2]

    @pl.when(pl.program_id(2) == 0)
    def _():
        k = _rms(zk_ref[...], kn_ref[...])
        v = zv_ref[...]
        if emit_kv:
            ko_ref[...] = k
            vo_ref[...] = v
        if rope:
            k = _rotate(k, ck_ref[...], sk_ref[...])
        ks_ref[0:seq, :] = k.astype(BF16)
        vt_ref[:, 0:seq] = v.T.astype(BF16)
        if cache_len:
            ks_ref[seq:seq + cache_len, :] = kc_ref[...].astype(BF16)
            vt_ref[:, seq:seq + cache_len] = vc_ref[...].T.astype(BF16)

    scale = hd ** -0.5 * math.log2(math.e)
    qs = []
    for hh in range(grp):
        q = _rms(zq_ref[:, hh * hd:(hh + 1) * hd], qn_ref[...])
        if rope:
            q = _rotate(q, cq_ref[...], sq_ref[...])
        qs.append((q * scale).astype(BF16))
    q = jnp.concatenate(qs, axis=0)
    s = lax.dot_general(ks_ref[...], q, (((1,), (1,)), ((), ())),
                        preferred_element_type=F32)
    p = jnp.exp2(s - jnp.max(s, axis=0, keepdims=True))
    acc = jnp.dot(vt_ref[...], p.astype(BF16), preferred_element_type=F32)
    out = (acc / jnp.sum(p, axis=0, keepdims=True)).T
    for hh in range(grp):
        o_ref[:, hh * hd:(hh + 1) * hd] = out[hh * tq:(hh + 1) * tq].astype(o_ref.dtype)


def _attention(z, layer, q_norm, k_norm, *, row0, nb, seq, n_kv, grp, hd, col_q, col_k, col_v,
               tq, rope_tabs=None, cache=None, emit_kv=False, into=None):
    qw = grp * hd
    cache_len = 0 if cache is None else cache[0].shape[2]
    nq = seq // tq
    rbq0, rbk0 = row0 // tq, row0 // seq
    cq0, ck0, cv0 = col_q // qw, col_k // hd, col_v // hd
    in_specs = [
        pl.BlockSpec((tq, qw), lambda b, h, i: (rbq0 + b * nq + i, cq0 + h)),
        pl.BlockSpec((seq, hd), lambda b, h, i: (rbk0 + b, ck0 + h)),
        pl.BlockSpec((seq, hd), lambda b, h, i: (rbk0 + b, cv0 + h)),
        pl.BlockSpec((None, 1, hd), lambda b, h, i: (layer, 0, 0)),
        pl.BlockSpec((None, 1, hd), lambda b, h, i: (layer, 0, 0)),
    ]
    args = [z, z, z, q_norm.reshape(-1, 1, hd), k_norm.reshape(-1, 1, hd)]
    if rope_tabs is not None:
        cos, sin = rope_tabs
        in_specs += [pl.BlockSpec((tq, hd), lambda b, h, i: (i, 0)),
                     pl.BlockSpec((tq, hd), lambda b, h, i: (i, 0)),
                     pl.BlockSpec((seq, hd), lambda b, h, i: (0, 0)),
                     pl.BlockSpec((seq, hd), lambda b, h, i: (0, 0))]
        args += [cos, sin, cos, sin]
    if cache is not None:
        in_specs += [pl.BlockSpec((None, None, cache_len, hd), lambda b, h, i: (b, h, 0, 0)),
                     pl.BlockSpec((None, None, cache_len, hd), lambda b, h, i: (b, h, 0, 0))]
        args += list(cache)
    out_shape = [jax.ShapeDtypeStruct((z.shape[0], n_kv * qw), BF16)]
    out_specs = [pl.BlockSpec((tq, qw), lambda b, h, i: (rbq0 + b * nq + i, h))]
    if emit_kv:
        out_shape += [jax.ShapeDtypeStruct((nb * seq, n_kv * hd), F32)] * 2
        out_specs += [pl.BlockSpec((seq, hd), lambda b, h, i: (b, h))] * 2
    kern = functools.partial(_attn_kernel, seq=seq, cache_len=cache_len, tq=tq, grp=grp,
                             hd=hd, rope=rope_tabs is not None, emit_kv=emit_kv)
    body, aliases = _fill(kern, in_specs, args, into)
    res = pl.pallas_call(
        body,
        out_shape=out_shape,
        grid=(nb, n_kv, nq),
        in_specs=in_specs,
        out_specs=out_specs,
        scratch_shapes=[pltpu.VMEM((seq + cache_len, hd), BF16),
                        pltpu.VMEM((hd, seq + cache_len), BF16)],
        input_output_aliases=aliases,
        compiler_params=_params(("arbitrary", "arbitrary", "arbitrary"), 48),
        name="attention",
    )(*args)
    return res if emit_kv else res[0]


def _pool_kernel(u_ref, w_ref, sc_ref, o_ref, pad_ref, *, rows, chunk, seq_p, seq_s, groups_p):
    gw = u_ref.shape[1]
    ext = chunk + 2 * HALO
    seq = jnp.where(pl.program_id(0) < groups_p, seq_p, seq_s)
    pad_ref[0:HALO, :] = jnp.zeros((HALO, gw), F32)
    pad_ref[HALO + rows:, :] = jnp.zeros((HALO, gw), F32)
    pad_ref[HALO:HALO + rows, :] = u_ref[...]

    for gi, win in enumerate(POOL_WINDOWS):
        hw = win // 2

        @pl.when(pl.program_id(1) == gi)
        def _(hw=hw):
            def body(ci, carry):
                r0 = pl.multiple_of(ci * chunk, chunk)
                xe = pad_ref[pl.ds(r0, ext), :]
                t = (r0 + lax.broadcasted_iota(jnp.int32, (chunk, gw), 0)) & (seq - 1)
                x0 = xe[HALO:HALO + chunk]
                acc = x0
                for d in range(-hw, hw):
                    if d == 0:
                        continue
                    xs = pltpu.roll(xe, (-d) % ext, axis=0)[HALO:HALO + chunk]
                    acc = acc + jnp.where((t >= -d) if d < 0 else (t < seq - d), xs, 0.0)
                cnt = (jnp.minimum(t + hw, seq) - jnp.maximum(t - hw, 0)).astype(F32)
                pooled = acc / cnt - x0
                y = jnp.dot(pooled.astype(BF16), w_ref[...], preferred_element_type=F32)
                o_ref[pl.ds(r0, chunk), :] = (y * sc_ref[...]).astype(o_ref.dtype)
                return carry

            lax.fori_loop(0, rows // chunk, body, 0)


def _pool(z, layer, pool_w, pool_scale, *, rows, groups_p, seq_p, seq_s, col0, bw):
    r = z.shape[0]
    g = pool_w.shape[1]
    gw = bw // g
    cb0 = col0 // gw
    chunk = min(256, rows)
    return pl.pallas_call(
        functools.partial(_pool_kernel, rows=rows, chunk=chunk, seq_p=seq_p, seq_s=seq_s,
                          groups_p=groups_p),
        out_shape=jax.ShapeDtypeStruct((r, bw), BF16),
        grid=(r // rows, g),
        in_specs=[pl.BlockSpec((rows, gw), lambda i, j: (i, cb0 + j)),
                  pl.BlockSpec((None, None, gw, gw), lambda i, j: (layer, j, 0, 0)),
                  pl.BlockSpec((None, 1, gw), lambda i, j: (layer, 0, j))],
        out_specs=pl.BlockSpec((rows, gw), lambda i, j: (i, j)),
        scratch_shapes=[pltpu.VMEM((rows + 2 * HALO, gw), F32)],
        compiler_params=_params(("arbitrary", "arbitrary"), 40),
        name="pool",
    )(z, pool_w, pool_scale.reshape(pool_scale.shape[0], 1, bw))


def _merge_kernel(*refs, n_cast):
    h_ref, g0, g1, g2, g3, o0, o1, o2, o3, wb_ref = refs[:10]
    out_ref = refs[10 + n_cast]
    _run_side_casts(refs[10:10 + n_cast], refs[11 + n_cast:])
    h = h_ref[...]
    acc = None
    for n, (g_ref, o_ref) in enumerate(((g0, o0), (g1, o1), (g2, o2), (g3, o3))):
        gate = _sigmoid(jnp.dot(h, g_ref[...], preferred_element_type=F32))
        y = jnp.dot(o_ref[...], wb_ref[n], preferred_element_type=F32)
        acc = gate * y if acc is None else acc + gate * y
    out_ref[...] = acc.astype(out_ref.dtype)


def _gate_merge(h, w_in, col_g, branches, w_branch, side, tm=512, tn=256):
    m, d = h.shape
    bw = w_branch.shape[1]
    dm = w_branch.shape[2]
    tm, tn = min(tm, m), min(tn, dm)
    nm = m // tm
    gspecs = [pl.BlockSpec((d, tn),
                           functools.partial(lambda j, i, n: (0, (col_g + n * dm) // tn + j), n=n))
              for n in range(N_BRANCH)]
    ospecs = [pl.BlockSpec((tm, bw), lambda j, i: (i, 0)) for _ in range(N_BRANCH)]
    c_in, c_args, c_out, c_shapes = _side_casts(side, (dm // tn) * nm, lambda j, i: j * nm + i)
    res = pl.pallas_call(
        functools.partial(_merge_kernel, n_cast=len(side)),
        out_shape=[jax.ShapeDtypeStruct((m, dm), BF16)] + c_shapes,
        grid=(dm // tn, nm),
        in_specs=[pl.BlockSpec((tm, d), lambda j, i: (i, 0))] + gspecs + ospecs
                 + [pl.BlockSpec((N_BRANCH, bw, tn), lambda j, i: (0, 0, j))] + c_in,
        out_specs=[pl.BlockSpec((tm, tn), lambda j, i: (i, j))] + c_out,
        compiler_params=_params(("arbitrary", "arbitrary"), 58),
        name="gate_merge",
    )(h, w_in, w_in, w_in, w_in, *branches, w_branch, *c_args)
    return res[0], res[1:]


def _out_residual_kernel(m_ref, w_ref, x_ref, g_ref, gate_ref, o_ref):
    y = jnp.dot(m_ref[...], w_ref[...], preferred_element_type=F32)
    o_ref[...] = x_ref[...] + gate_ref[...] * _rms(y, g_ref[...])


def _out_residual(rng, merged, w_out, gain, mod_l, k_gate, p_rows, s_rows, tm, into=None):
    src, src_row0, uni_row0, nrows = rng
    r, d = merged.shape
    st0, ut0 = src_row0 // tm, uni_row0 // tm
    in_specs = [pl.BlockSpec((tm, d), lambda i: (ut0 + i, 0)),
                pl.BlockSpec((d, d), lambda i: (0, 0), pipeline_mode=pl.Buffered(1)),
                pl.BlockSpec((tm, d), lambda i: (st0 + i, 0)),
                pl.BlockSpec((1, d), lambda i: (0, 0)),
                _mod_spec(k_gate, d, tm, p_rows, s_rows, ut0)]
    args = [merged, w_out, src, gain.reshape(1, d), mod_l]
    body, aliases = _fill(_out_residual_kernel, in_specs, args, into)
    return pl.pallas_call(
        body,
        out_shape=jax.ShapeDtypeStruct((r, d), F32),
        grid=(nrows // tm,),
        in_specs=in_specs,
        out_specs=pl.BlockSpec((tm, d), lambda i: (ut0 + i, 0)),
        input_output_aliases=aliases,
        compiler_params=_params(("arbitrary",), 60),
        name="out_residual",
    )(*args)


def _ffn_kernel(x_ref, gpre_ref, sh_ref, sc_ref, gate_ref, gpost_ref, wg_ref, wu_ref, wd_ref,
                o_ref, h_ref, stat_ref, *, slab, overlap):
    k = pl.program_id(1)
    tm, d = x_ref.shape
    nslab = tm // slab

    def rows(i):
        return pl.ds(pl.multiple_of(i * slab, slab), slab)

    @pl.when(k == 0)
    def _():
        _row_rinv(x_ref, stat_ref)

        def body(i, carry):
            h = x_ref[rows(i), :] * _rinv_rows(stat_ref, rows(i), d) * gpre_ref[...]
            h_ref[rows(i), :] = (h * (1.0 + sc_ref[...]) + sh_ref[...]).astype(BF16)
            o_ref[rows(i), :] = jnp.zeros((slab, d), F32)
            return carry
        lax.fori_loop(0, nslab, body, 0, unroll=2)

    h = h_ref[...]
    a = jnp.dot(h, wg_ref[...], preferred_element_type=F32)
    b = jnp.dot(h, wu_ref[...], preferred_element_type=F32)
    act = a * _sigmoid(a) * b
    if overlap:
        col = lax.broadcasted_iota(jnp.int32, act.shape, 1)
        fresh = jnp.logical_or(k < pl.num_programs(1) - 1, col >= overlap)
        act = jnp.where(fresh, act, 0.0)
    act = act.astype(BF16)
    o_ref[...] += jnp.dot(act, wd_ref[...], preferred_element_type=F32)

    @pl.when(k == pl.num_programs(1) - 1)
    def _():
        _row_rinv(o_ref, stat_ref)

        def body(i, carry):
            y = o_ref[rows(i), :] * _rinv_rows(stat_ref, rows(i), d) * gpost_ref[...]
            o_ref[rows(i), :] = x_ref[rows(i), :] + gate_ref[...] * y
            return carry
        lax.fori_loop(0, nslab, body, 0, unroll=2)


def _ffn(x, row0, nrows, g_pre, g_post, mod_l, wg, wu, wd, p_rows, s_rows, tm=512):
    d = x.shape[1]
    dff = wg.shape[1]
    tm, tc = min(tm, nrows), min(FFN_CHUNK, dff)
    nk = pl.cdiv(dff, tc)
    overlap = nk * tc - dff

    def chunk_start(k):
        return pl.multiple_of(jnp.minimum(k * tc, dff - tc), math.gcd(tc, dff))

    t0 = row0 // tm
    return pl.pallas_call(
        functools.partial(_ffn_kernel, slab=math.gcd(16, tm), overlap=overlap),
        out_shape=jax.ShapeDtypeStruct((nrows, d), F32),
        grid=(nrows // tm, nk),
        in_specs=[pl.BlockSpec((tm, d), lambda i, k: (t0 + i, 0), pipeline_mode=pl.Buffered(1)),
                  pl.BlockSpec((1, d), lambda i, k: (0, 0)),
                  _mod_spec(3, d, tm, p_rows, s_rows, t0),
                  _mod_spec(4, d, tm, p_rows, s_rows, t0),
                  _mod_spec(5, d, tm, p_rows, s_rows, t0),
                  pl.BlockSpec((1, d), lambda i, k: (0, 0)),
                  pl.BlockSpec((pl.Element(d), pl.Element(tc)),
                               lambda i, k: (0, chunk_start(k))),
                  pl.BlockSpec((pl.Element(d), pl.Element(tc)),
                               lambda i, k: (0, chunk_start(k))),
                  pl.BlockSpec((pl.Element(tc), pl.Element(d)),
                               lambda i, k: (chunk_start(k), 0))],
        out_specs=pl.BlockSpec((tm, d), lambda i, k: (i, 0)),
        scratch_shapes=[pltpu.VMEM((tm, d), BF16), pltpu.VMEM((tm, 128), F32)],
        compiler_params=_params(("arbitrary", "arbitrary"), 60),
        name="ffn",
    )(x, g_pre.reshape(1, d), mod_l, mod_l, mod_l, g_post.reshape(1, d), wg, wu, wd)


def kernel(x_prompt, x_sample, c, cache_k, cache_v, state_lru, c_ctx, w_mod, b_mod, g_pre_mix, g_post_mix, g_pre_ffn, g_post_ffn, w_in, conv_w, conv_b, lru_wa, lru_ba, lru_wx, lru_bx, lru_lambda, q_norm, k_norm, pool_w, pool_scale, w_branch, w_out, w_ffn_gate, w_ffn_up, w_ffn_down):
    nbp, seq_p, d = x_prompt.shape
    nbs, seq_s, _ = x_sample.shape
    depth = w_in.shape[0]
    bw = w_branch.shape[2]
    cache_len, n_kv, hd = cache_k.shape[2:]
    grp = bw // hd // n_kv
    kvw = n_kv * hd
    p_rows, s_rows = nbp * seq_p, nbs * seq_s
    rows = p_rows + s_rows
    assert seq_p & (seq_p - 1) == 0 and seq_s & (seq_s - 1) == 0
    assert p_rows % seq_s == 0, "row groups of seq_s rows must tile the prompt rows"
    col_a, col_bx, col_by, col_q = 0, bw, 2 * bw, 3 * bw
    col_k = col_q + bw
    col_v = col_k + kvw
    col_d = col_v + kvw
    col_g = col_d + bw
    zw = col_g

    def rows_tile(pref):
        return math.gcd(pref, math.gcd(p_rows, seq_s))

    w_mix_l = w_in[0, :, :zw].astype(BF16)
    w_in_l = w_branch_l = w_out_l = None
    w_branch_2d = w_branch.reshape(depth, N_BRANCH * bw, d)
    wa_b = lru_wa.astype(BF16)
    wx_b = lru_wx.astype(BF16)
    pool_w_b = pool_w.astype(BF16)

    gw = bw // FOURIER_GROUPS
    cc, sc = _dft_tables(gw)
    tab_c = jnp.concatenate([cc, sc], axis=1).astype(BF16)
    cp, sp = _dft_tables(seq_p)
    dft_p = jnp.concatenate([cp, -sp], axis=1).astype(BF16)
    cs, ss = _dft_tables(seq_s)
    dft_s = jnp.concatenate([cs, -ss], axis=1).astype(BF16)
    rope_tabs = _rope_tables(seq_s, hd)

    cvec = jnp.concatenate([c_ctx[None, :], c, jnp.zeros((8 - 1 - nbs, d), c.dtype)], axis=0)
    mod = _mod_all(cvec.astype(F32), w_mod, b_mod)

    cache_kt = jnp.transpose(cache_k, (0, 1, 3, 2, 4)).astype(F32)
    cache_vt = jnp.transpose(cache_v, (0, 1, 3, 2, 4)).astype(F32)

    x_ranges = [(x_prompt.reshape(p_rows, d).astype(F32), 0, 0, p_rows),
                (x_sample.reshape(s_rows, d).astype(F32), 0, p_rows, s_rows)]
    tm_norm = rows_tile(256)
    k_list, v_list, s_list = [], [], []
    y_p = y_s = None
    for l in range(depth):
        mod_l = mod[l].reshape(8, 1, 6 * d)
        h = None
        for rng in x_ranges:
            h = _prenorm(rng, rows, g_pre_mix[l], mod_l, 0, p_rows, seq_s, tm_norm, into=h)
        if l == 0:
            z, (w_in_l, w_branch_l, w_out_l) = _matmul_cols(
                h, w_mix_l, zw, rows_tile(1024), 512, F32, 56, "in_proj",
                side=[(w_in, 0), (w_branch_2d, 0), (w_out, 0)])
            w_branch_l = w_branch_l.reshape(N_BRANCH, bw, d)
        else:
            z, _ = _matmul_cols(h, w_in_l, zw, rows_tile(1024), 512, F32, 48, "in_proj")

        pq_p = _fourier_stage1(z, tab_c, 0, nbp, seq_p, bw, col_a)
        o_a = _matmul_batched_rhs(dft_p, pq_p, 512, 1024, BF16, 48, "fourier2p", rows, 0)
        pq_s = _fourier_stage1(z, tab_c, p_rows, nbs, seq_s, bw, col_a)
        o_a = _matmul_batched_rhs(dft_s, pq_s, 512, 512, BF16, 48, "fourier2s", rows, p_rows,
                                  into=o_a)

        lru_args = (conv_w, conv_b, wa_b, wx_b, lru_ba, lru_bx, lru_lambda)
        o_b, st = _lru(z, l, *lru_args, None, row0=0, rows=p_rows, ngroups=1, seq=seq_p,
                       nseg=nbp, seg_len=seq_p, col_x=col_bx, col_y=col_by, bw=bw)
        seg = min(128, seq_s)
        o_b = _lru(z, l, *lru_args, state_lru[:, l].astype(F32), row0=p_rows, rows=seq_s,
                   ngroups=nbs, seq=seq_s, nseg=seq_s // seg, seg_len=seg,
                   col_x=col_bx, col_y=col_by, bw=bw, into=o_b)
        s_list.append(jnp.transpose(st, (1, 0, 2)))

        att = dict(n_kv=n_kv, grp=grp, hd=hd, col_q=col_q, col_k=col_k, col_v=col_v)
        o_c, k_p, v_p = _attention(z, l, q_norm, k_norm, row0=0, nb=nbp, seq=seq_p, tq=seq_p,
                                   emit_kv=True, **att)
        o_c = _attention(z, l, q_norm, k_norm, row0=p_rows, nb=nbs, seq=seq_s,
                         tq=min(256, seq_s), rope_tabs=rope_tabs,
                         cache=(cache_kt[:, l], cache_vt[:, l]), into=o_c, **att)
        k_list.append(k_p.reshape(nbp, seq_p, n_kv, hd))
        v_list.append(v_p.reshape(nbp, seq_p, n_kv, hd))

        o_d = _pool(z, l, pool_w_b, pool_scale, rows=seq_s, groups_p=p_rows // seq_s,
                    seq_p=seq_p, seq_s=seq_s, col0=col_d, bw=bw)

        side = [(w_ffn_gate, l), (w_ffn_up, l), (w_ffn_down, l)]
        if l + 1 < depth:
            side += [(w_in, l + 1), (w_branch_2d, l + 1), (w_out, l + 1)]
        merged, cast = _gate_merge(h, w_in_l, col_g, (o_a, o_b, o_c, o_d), w_branch_l,
                                   side=side, tm=rows_tile(512))
        x = None
        for rng in x_ranges:
            x = _out_residual(rng, merged, w_out_l, g_post_mix[l], mod_l, 2, p_rows, seq_s,
                              tm_norm, into=x)
        ffn = functools.partial(_ffn, x, g_pre=g_pre_ffn[l], g_post=g_post_ffn[l],
                                mod_l=mod_l, wg=cast[0], wu=cast[1], wd=cast[2], p_rows=p_rows,
                                s_rows=seq_s, tm=rows_tile(512))
        if l + 1 < depth:
            w_in_l, w_branch_l, w_out_l = cast[3], cast[4].reshape(N_BRANCH, bw, d), cast[5]
            x_ranges = [(ffn(0, rows), 0, 0, rows)]
        else:
            y_p, y_s = ffn(0, p_rows), ffn(p_rows, s_rows)

    y_p = y_p.reshape(nbp, seq_p, d).astype(x_prompt.dtype)
    y_s = y_s.reshape(nbs, seq_s, d).astype(x_sample.dtype)
    new_k = jnp.stack(k_list, axis=1).astype(cache_k.dtype)
    new_v = jnp.stack(v_list, axis=1).astype(cache_v.dtype)
    new_s = jnp.stack(s_list, axis=1).astype(state_lru.dtype)
    return y_p, y_s, new_k, new_v, new_s
```

```python
import functools
import math

import jax
import jax.numpy as jnp
from jax import lax
from jax.experimental import pallas as pl
from jax.experimental.pallas import tpu as pltpu

F32 = jnp.float32
BF16 = jnp.bfloat16
MIB = 1024 * 1024

EPS = 1e-6
GRID_W = 64
ROPE_THETA = 10000.0
LRU_C = 8.0
FOURIER_GROUPS = 4
POOL_WINDOWS = (2, 4, 8, 16)
N_BRANCH = 4
HALO = 8
CONV_LEFT = 2
FFN_CHUNK = 512
SEG_PAD = 8


def _params(semantics, vmem_mib):
    return pltpu.CompilerParams(dimension_semantics=semantics,
                                vmem_limit_bytes=vmem_mib * MIB)


def _rms(x, gain):
    ms = jnp.mean(x * x, axis=-1, keepdims=True)
    return x * lax.rsqrt(ms + EPS) * gain


def _sigmoid(x):
    return 0.5 * jnp.tanh(0.5 * x) + 0.5


def _row_rinv(src_ref, stat_ref):
    n, d = src_ref.shape

    def body(i, carry):
        sl = pl.ds(pl.multiple_of(i * 8, 8), 8)
        x = src_ref[sl, :]
        ms = jnp.sum(x * x, axis=-1, keepdims=True) * (1.0 / d)
        stat_ref[sl, :] = jnp.broadcast_to(lax.rsqrt(ms + EPS), (8, stat_ref.shape[1]))
        return carry
    lax.fori_loop(0, n // 8, body, 0, unroll=8)


def _rinv_rows(stat_ref, sl, d):
    return jnp.tile(stat_ref[sl, :], (1, d // stat_ref.shape[1]))


def _softplus(x):
    return jnp.maximum(x, 0.0) + jnp.log1p(jnp.exp(-jnp.abs(x)))


def _group_of_tile(i, tm, p_rows, s_rows):
    r = i * tm
    return jnp.where(r < p_rows, 0, 1 + (r - p_rows) // s_rows)


def _mod_spec(k, d, tm, p_rows, s_rows, tile0=0):
    return pl.BlockSpec((None, 1, d),
                        lambda i, *_: (_group_of_tile(tile0 + i, tm, p_rows, s_rows), 0, k))


def _fill(kernel, in_specs, args, into):
    if into is None:
        return kernel, {}
    n = len(args)
    in_specs.append(pl.BlockSpec(memory_space=pl.ANY))
    args.append(into)

    def body(*refs):
        return kernel(*refs[:n], *refs[n + 1:])
    return body, {n: 0}


def _side_casts(weights, nsteps, step_of):
    in_specs, args, out_specs, out_shapes = [], [], [], []
    for w, layer in weights:
        _, rows, cols = w.shape
        br = next(b for b in range(16, rows + 1, 16) if rows % b == 0 and rows // b <= nsteps)
        last = rows // br - 1
        in_specs.append(pl.BlockSpec(
            (None, br, cols),
            functools.partial(lambda *g, layer, last: (layer, jnp.minimum(step_of(*g), last), 0),
                              layer=layer, last=last)))
        out_specs.append(pl.BlockSpec(
            (br, cols),
            functools.partial(lambda *g, last: (jnp.minimum(step_of(*g), last), 0), last=last)))
        args.append(w)
        out_shapes.append(jax.ShapeDtypeStruct((rows, cols), BF16))
    return in_specs, args, out_specs, out_shapes


def _run_side_casts(src_refs, dst_refs):
    for src, dst in zip(src_refs, dst_refs):
        dst[...] = src[...].astype(BF16)


def _mod_kernel(c_ref, w_ref, b_ref, o_ref):
    c = c_ref[...]
    s = (c * jax.nn.sigmoid(c)).astype(BF16)
    o_ref[...] = jnp.dot(s, w_ref[...].astype(BF16),
                         preferred_element_type=F32) + b_ref[...]


def _mod_all(cvec, w_mod, b_mod, tn=512):
    depth, d, n = w_mod.shape
    return pl.pallas_call(
        _mod_kernel,
        out_shape=jax.ShapeDtypeStruct((depth, 8, n), F32),
        grid=(depth, n // tn),
        in_specs=[pl.BlockSpec((8, d), lambda l, j: (0, 0)),
                  pl.BlockSpec((None, d, tn), lambda l, j: (l, 0, j)),
                  pl.BlockSpec((None, 1, tn), lambda l, j: (l, 0, j))],
        out_specs=pl.BlockSpec((None, 8, tn), lambda l, j: (l, 0, j)),
        compiler_params=_params(("arbitrary", "arbitrary"), 40),
        name="mod",
    )(cvec, w_mod, b_mod.reshape(depth, 1, n))


def _prenorm_kernel(x_ref, g_ref, sh_ref, sc_ref, o_ref, stat_ref):
    tm, d = x_ref.shape
    slab = math.gcd(16, tm)
    _row_rinv(x_ref, stat_ref)

    def body(i, carry):
        sl = pl.ds(pl.multiple_of(i * slab, slab), slab)
        h = x_ref[sl, :] * _rinv_rows(stat_ref, sl, d) * g_ref[...]
        o_ref[sl, :] = (h * (1.0 + sc_ref[...]) + sh_ref[...]).astype(o_ref.dtype)
        return carry
    lax.fori_loop(0, tm // slab, body, 0, unroll=2)


def _prenorm(rng, total_rows, gain, mod_l, k_shift, p_rows, s_rows, tm, into=None):
    src, src_row0, uni_row0, nrows = rng
    d = src.shape[1]
    st0, ut0 = src_row0 // tm, uni_row0 // tm
    in_specs = [pl.BlockSpec((tm, d), lambda i: (st0 + i, 0)),
                pl.BlockSpec((1, d), lambda i: (0, 0)),
                _mod_spec(k_shift, d, tm, p_rows, s_rows, ut0),
                _mod_spec(k_shift + 1, d, tm, p_rows, s_rows, ut0)]
    args = [src, gain.reshape(1, d), mod_l, mod_l]
    body, aliases = _fill(_prenorm_kernel, in_specs, args, into)
    return pl.pallas_call(
        body,
        out_shape=jax.ShapeDtypeStruct((total_rows, d), BF16),
        grid=(nrows // tm,),
        in_specs=in_specs,
        out_specs=pl.BlockSpec((tm, d), lambda i: (ut0 + i, 0)),
        scratch_shapes=[pltpu.VMEM((tm, 128), F32)],
        input_output_aliases=aliases,
        compiler_params=_params(("arbitrary",), 40),
        name="prenorm",
    )(*args)


def _mm_kernel(a_ref, b_ref, o_ref):
    o_ref[...] = jnp.dot(a_ref[...], b_ref[...],
                         preferred_element_type=F32).astype(o_ref.dtype)


def _mm_side_kernel(*refs, n_cast):
    a_ref, b_ref = refs[:2]
    _run_side_casts(refs[2:2 + n_cast], refs[3 + n_cast:])
    _mm_kernel(a_ref, b_ref, refs[2 + n_cast])


def _matmul_cols(a, w, n_cols, tm, tn, out_dtype, vmem_mib, name, side=()):
    m, k = a.shape
    tm, tn = min(tm, m), min(tn, n_cols)
    nn = n_cols // tn
    c_in, c_args, c_out, c_shapes = _side_casts(side, (m // tm) * nn, lambda i, j: i * nn + j)
    res = pl.pallas_call(
        functools.partial(_mm_side_kernel, n_cast=len(side)),
        out_shape=[jax.ShapeDtypeStruct((m, n_cols), out_dtype)] + c_shapes,
        grid=(m // tm, nn),
        in_specs=[pl.BlockSpec((tm, k), lambda i, j: (i, 0)),
                  pl.BlockSpec((k, tn), lambda i, j: (0, j))] + c_in,
        out_specs=[pl.BlockSpec((tm, tn), lambda i, j: (i, j))] + c_out,
        compiler_params=_params(("arbitrary", "arbitrary"), vmem_mib),
        name=name,
    )(a, w, *c_args)
    return res[0], res[1:]


def _matmul_batched_rhs(a, b, tm, tn, out_dtype, vmem_mib, name, total_rows, row0, into=None):
    m, k = a.shape
    nb, _, n = b.shape
    tm, tn = min(tm, m), min(tn, n)
    t0, nt = row0 // tm, m // tm
    in_specs = [pl.BlockSpec((tm, k), lambda bi, i, j: (i, 0)),
                pl.BlockSpec((None, k, tn), lambda bi, i, j: (bi, 0, j))]
    args = [a, b]
    body, aliases = _fill(_mm_kernel, in_specs, args, into)
    return pl.pallas_call(
        body,
        out_shape=jax.ShapeDtypeStruct((total_rows, n), out_dtype),
        grid=(nb, nt, n // tn),
        in_specs=in_specs,
        out_specs=pl.BlockSpec((tm, tn), lambda bi, i, j: (t0 + bi * nt + i, j)),
        input_output_aliases=aliases,
        compiler_params=_params(("arbitrary", "arbitrary", "arbitrary"), vmem_mib),
        name=name,
    )(*args)


def _dft_tables(n, split=64):
    s = 1.0 / math.sqrt(n)
    k = jnp.arange(n, dtype=jnp.int32)

    def thin(rows, period):
        r = jnp.arange(rows, dtype=jnp.int32)
        ang = ((r[:, None] * k[None, :]) % period).astype(F32) * (2.0 * math.pi / period)
        return jnp.cos(ang), jnp.sin(ang)

    if n <= split or n % split:
        c, sn = thin(n, n)
        return c * s, sn * s
    ca, sa = thin(n // split, n // split)
    cb, sb = thin(split, n)
    ca, sa = ca[:, None, :] * s, sa[:, None, :] * s
    cos = (ca * cb[None] - sa * sb[None]).reshape(n, n)
    sin = (sa * cb[None] + ca * sb[None]).reshape(n, n)
    return cos, sin


def _four1_kernel(u_ref, t_ref, o_ref, *, seq, gw):
    u = u_ref[...].astype(BF16)
    pq = jnp.dot(u, t_ref[...], preferred_element_type=F32)
    o_ref[0:seq, :] = pq[:, :gw].astype(o_ref.dtype)
    o_ref[seq:2 * seq, :] = pq[:, gw:].astype(o_ref.dtype)


def _fourier_stage1(z, tab, row0, nb, seq, bw, col0):
    gw = bw // FOURIER_GROUPS
    rb0, cb0 = row0 // seq, col0 // gw
    return pl.pallas_call(
        functools.partial(_four1_kernel, seq=seq, gw=gw),
        out_shape=jax.ShapeDtypeStruct((nb, 2 * seq, bw), BF16),
        grid=(nb, FOURIER_GROUPS),
        in_specs=[pl.BlockSpec((seq, gw), lambda b, g: (rb0 + b, cb0 + g)),
                  pl.BlockSpec((gw, 2 * gw), lambda b, g: (0, 0))],
        out_specs=pl.BlockSpec((None, 2 * seq, gw), lambda b, g: (b, 0, g)),
        compiler_params=_params(("arbitrary", "arbitrary"), 40),
        name="fourier1",
    )(z, tab)


def _lru_kernel(*refs, seq, rows, nseg, seg_len, chained, chunk):
    if chained:
        (zx_ref, zy_ref, cw_ref, cb_ref, wa_ref, wx_ref, ba_ref, bx_ref, lam_ref,
         h0_ref, o_ref, pad_ref, a0_ref, a1_ref, b0_ref, b1_ref) = refs
        st_ref = None
    else:
        (zx_ref, zy_ref, cw_ref, cb_ref, wa_ref, wx_ref, ba_ref, bx_ref, lam_ref,
         o_ref, st_ref, pad_ref, a0_ref, a1_ref, b0_ref, b1_ref) = refs
        h0_ref = None
    a_refs, b_refs = (a0_ref, a1_ref), (b0_ref, b1_ref)
    cw_ch = zx_ref.shape[1]
    ext = chunk + 2 * HALO
    n_taps = cw_ref.shape[0]
    pitch = seg_len + SEG_PAD
    segs = chunk // seg_len

    pad_ref[0:HALO, :] = jnp.zeros((HALO, cw_ch), F32)
    pad_ref[HALO + rows:, :] = jnp.zeros((HALO, cw_ch), F32)
    pad_ref[HALO:HALO + rows, :] = zx_ref[...]

    neg_sp = [-LRU_C * _softplus(-lam_ref[d:d + 1, :]) for d in range(2)]

    def seg_rows(ci, j):
        return pl.ds(pl.multiple_of(ci * (segs * pitch), 8) + j * pitch, seg_len)

    def gates(ci, carry):
        r0 = pl.multiple_of(ci * chunk, chunk)
        xe = pad_ref[pl.ds(r0, ext), :]
        t = (r0 + lax.broadcasted_iota(jnp.int32, (chunk, cw_ch), 0)) & (seq - 1)
        xc = jnp.zeros((chunk, cw_ch), F32) + cb_ref[...]
        for j in range(n_taps):
            d = j - CONV_LEFT
            if d == 0:
                xs = xe[HALO:HALO + chunk]
            else:
                xs = pltpu.roll(xe, (-d) % ext, axis=0)[HALO:HALO + chunk]
                xs = jnp.where((t >= -d) if d < 0 else (t < seq - d), xs, 0.0)
            xc = xc + xs * cw_ref[j:j + 1, :]
        xcb = xc.astype(BF16)
        for d in range(2):
            r = _sigmoid(jnp.dot(xcb, wa_ref[d], preferred_element_type=F32)
                         + ba_ref[d:d + 1, :])
            i = _sigmoid(jnp.dot(xcb, wx_ref[d], preferred_element_type=F32)
                         + bx_ref[d:d + 1, :])
            a = jnp.exp(r * neg_sp[d])
            bterm = jnp.sqrt(1.0 - a * a) * (i * xc)
            for j in range(segs):
                a_refs[d][seg_rows(ci, j), :] = a[j * seg_len:(j + 1) * seg_len]
                b_refs[d][seg_rows(ci, j), :] = bterm[j * seg_len:(j + 1) * seg_len]
        return carry

    lax.fori_loop(0, rows // chunk, gates, 0)

    def scan(direction):
        def body(s, carry):
            h, p = carry
            pos = s if direction == 0 else seg_len - 1 - s
            idx = pl.ds(pos, nseg, stride=pitch)
            a = a_refs[direction][idx, :]
            h = a * h + b_refs[direction][idx, :]
            b_refs[direction][idx, :] = h
            if chained:
                p = a * p
                a_refs[direction][idx, :] = p
            return h, p
        init = (jnp.zeros((nseg, cw_ch), F32), jnp.ones((nseg, cw_ch), F32))
        lax.fori_loop(0, seg_len, body, init, unroll=8)

    scan(0)
    scan(1)

    if chained:
        carry = h0_ref[0:1, :]
        for q in range(nseg):
            sl = slice(q * pitch, q * pitch + seg_len)
            hq = b_refs[0][sl, :] + a_refs[0][sl, :] * carry
            b_refs[0][sl, :] = hq
            carry = hq[seg_len - 1:seg_len, :]
        carry = h0_ref[1:2, :]
        for q in reversed(range(nseg)):
            sl = slice(q * pitch, q * pitch + seg_len)
            hq = b_refs[1][sl, :] + a_refs[1][sl, :] * carry
            b_refs[1][sl, :] = hq
            carry = hq[0:1, :]
    else:
        st_ref[0] = b_refs[0][pl.ds(seq - 1, nseg, stride=pitch), :]
        st_ref[1] = b_refs[1][pl.ds(0, nseg, stride=pitch), :]

    def emit(ci, carry):
        r0 = pl.multiple_of(ci * chunk, chunk)
        for j in range(segs):
            sl = pl.ds(r0 + j * seg_len, seg_len)
            hsum = b_refs[0][seg_rows(ci, j), :] + b_refs[1][seg_rows(ci, j), :]
            o_ref[sl, :] = (hsum * jax.nn.gelu(zy_ref[sl, :])).astype(o_ref.dtype)
        return carry

    lax.fori_loop(0, rows // chunk, emit, 0)


def _lru(z, layer, conv_w, conv_b, wa, wx, ba, bx, lam, h0, *, row0, rows, ngroups,
         seq, nseg, seg_len, col_x, col_y, bw, into=None):
    cb = wa.shape[-1]
    ncb = bw // cb
    chained = h0 is not None
    chunk = max(min(256, rows), seg_len)
    assert chunk % seg_len == 0 and rows % chunk == 0 and seg_len % 8 == 0
    assert chained or seg_len == seq
    rb0 = row0 // rows
    cx0, cy0 = col_x // cb, col_y // cb
    in_specs = [
        pl.BlockSpec((rows, cb), lambda g, c: (rb0 + g, cx0 + c)),
        pl.BlockSpec((rows, cb), lambda g, c: (rb0 + g, cy0 + c)),
        pl.BlockSpec((None, conv_w.shape[1], cb), lambda g, c: (layer, 0, c)),
        pl.BlockSpec((None, 1, cb), lambda g, c: (layer, 0, c)),
        pl.BlockSpec((None, 2, None, cb, cb), lambda g, c: (layer, 0, c, 0, 0)),
        pl.BlockSpec((None, 2, None, cb, cb), lambda g, c: (layer, 0, c, 0, 0)),
        pl.BlockSpec((None, 2, cb), lambda g, c: (layer, 0, c)),
        pl.BlockSpec((None, 2, cb), lambda g, c: (layer, 0, c)),
        pl.BlockSpec((None, 2, cb), lambda g, c: (layer, 0, c)),
    ]
    args = [z, z, conv_w, conv_b.reshape(conv_b.shape[0], 1, bw), wa, wx, ba, bx, lam]
    out_shape = [jax.ShapeDtypeStruct((z.shape[0], bw), BF16)]
    out_specs = [pl.BlockSpec((rows, cb), lambda g, c: (rb0 + g, c))]
    if chained:
        in_specs.append(pl.BlockSpec((None, 2, cb), lambda g, c: (g, 0, c)))
        args.append(h0)
    else:
        assert ngroups == 1
        out_shape.append(jax.ShapeDtypeStruct((2, rows // seq, bw), F32))
        out_specs.append(pl.BlockSpec((2, rows // seq, cb), lambda g, c: (0, 0, c)))
    kern = functools.partial(_lru_kernel, seq=seq, rows=rows, nseg=nseg, seg_len=seg_len,
                             chained=chained, chunk=chunk)
    body, aliases = _fill(kern, in_specs, args, into)
    res = pl.pallas_call(
        body,
        out_shape=out_shape,
        grid=(ngroups, ncb),
        in_specs=in_specs,
        out_specs=out_specs,
        scratch_shapes=[pltpu.VMEM((rows + 2 * HALO, cb), F32)]
                       + [pltpu.VMEM((nseg * (seg_len + SEG_PAD), cb), F32)] * 4,
        input_output_aliases=aliases,
        compiler_params=_params(("arbitrary", "arbitrary"), 48),
        name="rglru",
    )(*args)
    return res if not chained else res[0]


def _rope_tables(seq, hd):
    rows = seq // GRID_W
    row = jnp.repeat(jnp.arange(rows), GRID_W).astype(F32)
    col = jnp.tile(jnp.arange(GRID_W), rows).astype(F32)
    ax = hd // 2
    inv = ROPE_THETA ** (-jnp.arange(0, ax, 2, dtype=F32) / ax)
    ang_r = row[:, None] * inv[None, :]
    ang_c = col[:, None] * inv[None, :]
    cos = jnp.concatenate([jnp.cos(ang_r)] * 2 + [jnp.cos(ang_c)] * 2, axis=-1)
    sin = jnp.concatenate([-jnp.sin(ang_r), jnp.sin(ang_r),
                           -jnp.sin(ang_c), jnp.sin(ang_c)], axis=-1)
    return cos, sin


def _rotate(x, cos, sin):
    hd = x.shape[1]
    q = hd // 4
    lane = lax.broadcasted_iota(jnp.int32, x.shape, 1)
    ahead = pltpu.roll(x, hd - q, axis=1)
    behind = pltpu.roll(x, q, axis=1)
    partner = jnp.where((lane & q) == 0, ahead, behind)
    return x * cos + partner * sin


def _attn_kernel(*refs, seq, cache_len, tq, grp, hd, rope, emit_kv):
    refs = list(refs)
    zq_ref, zk_ref, zv_ref, qn_ref, kn_ref = refs[:5]
    pos = 5
    if rope:
        cq_ref, sq_ref, ck_ref, sk_ref = refs[pos:pos + 4]
        pos += 4
    if cache_len:
        kc_ref, vc_ref = refs[pos:pos + 2]
        pos += 2
    o_ref = refs[pos]
    pos += 1
    if emit_kv:
        ko_ref, vo_ref = refs[pos:pos + 2]
        pos += 2
    ks_ref, vt_ref = refs[pos:pos + 2]

    @pl.when(pl.program_id(2) == 0)
    def _():
        k = _rms(zk_ref[...], kn_ref[...])
        v = zv_ref[...]
        if emit_kv:
            ko_ref[...] = k
            vo_ref[...] = v
        if rope:
            k = _rotate(k, ck_ref[...], sk_ref[...])
        ks_ref[0:seq, :] = k.astype(BF16)
        vt_ref[:, 0:seq] = v.T.astype(BF16)
        if cache_len:
            ks_ref[seq:seq + cache_len, :] = kc_ref[...].astype(BF16)
            vt_ref[:, seq:seq + cache_len] = vc_ref[...].T.astype(BF16)

    scale = hd ** -0.5 * math.log2(math.e)
    qs = []
    for hh in range(grp):
        q = _rms(zq_ref[:, hh * hd:(hh + 1) * hd], qn_ref[...])
        if rope:
            q = _rotate(q, cq_ref[...], sq_ref[...])
        qs.append((q * scale).astype(BF16))
    q = jnp.concatenate(qs, axis=0)
    s = lax.dot_general(ks_ref[...], q, (((1,), (1,)), ((), ())),
                        preferred_element_type=F32)
    p = jnp.exp2(s - jnp.max(s, axis=0, keepdims=True))
    acc = jnp.dot(vt_ref[...], p.astype(BF16), preferred_element_type=F32)
    out = (acc / jnp.sum(p, axis=0, keepdims=True)).T
    for hh in range(grp):
        o_ref[:, hh * hd:(hh + 1) * hd] = out[hh * tq:(hh + 1) * tq].astype(o_ref.dtype)


def _attention(z, layer, q_norm, k_norm, *, row0, nb, seq, n_kv, grp, hd, col_q, col_k, col_v,
               tq, rope_tabs=None, cache=None, emit_kv=False, into=None):
    qw = grp * hd
    cache_len = 0 if cache is None else cache[0].shape[2]
    nq = seq // tq
    rbq0, rbk0 = row0 // tq, row0 // seq
    cq0, ck0, cv0 = col_q // qw, col_k // hd, col_v // hd
    in_specs = [
        pl.BlockSpec((tq, qw), lambda b, h, i: (rbq0 + b * nq + i, cq0 + h)),
        pl.BlockSpec((seq, hd), lambda b, h, i: (rbk0 + b, ck0 + h)),
        pl.BlockSpec((seq, hd), lambda b, h, i: (rbk0 + b, cv0 + h)),
        pl.BlockSpec((None, 1, hd), lambda b, h, i: (layer, 0, 0)),
        pl.BlockSpec((None, 1, hd), lambda b, h, i: (layer, 0, 0)),
    ]
    args = [z, z, z, q_norm.reshape(-1, 1, hd), k_norm.reshape(-1, 1, hd)]
    if rope_tabs is not None:
        cos, sin = rope_tabs
        in_specs += [pl.BlockSpec((tq, hd), lambda b, h, i: (i, 0)),
                     pl.BlockSpec((tq, hd), lambda b, h, i: (i, 0)),
                     pl.BlockSpec((seq, hd), lambda b, h, i: (0, 0)),
                     pl.BlockSpec((seq, hd), lambda b, h, i: (0, 0))]
        args += [cos, sin, cos, sin]
    if cache is not None:
        in_specs += [pl.BlockSpec((None, None, cache_len, hd), lambda b, h, i: (b, h, 0, 0)),
                     pl.BlockSpec((None, None, cache_len, hd), lambda b, h, i: (b, h, 0, 0))]
        args += list(cache)
    out_shape = [jax.ShapeDtypeStruct((z.shape[0], n_kv * qw), BF16)]
    out_specs = [pl.BlockSpec((tq, qw), lambda b, h, i: (rbq0 + b * nq + i, h))]
    if emit_kv:
        out_shape += [jax.ShapeDtypeStruct((nb * seq, n_kv * hd), F32)] * 2
        out_specs += [pl.BlockSpec((seq, hd), lambda b, h, i: (b, h))] * 2
    kern = functools.partial(_attn_kernel, seq=seq, cache_len=cache_len, tq=tq, grp=grp,
                             hd=hd, rope=rope_tabs is not None, emit_kv=emit_kv)
    body, aliases = _fill(kern, in_specs, args, into)
    res = pl.pallas_call(
        body,
        out_shape=out_shape,
        grid=(nb, n_kv, nq),
        in_specs=in_specs,
        out_specs=out_specs,
        scratch_shapes=[pltpu.VMEM((seq + cache_len, hd), BF16),
                        pltpu.VMEM((hd, seq + cache_len), BF16)],
        input_output_aliases=aliases,
        compiler_params=_params(("arbitrary", "arbitrary", "arbitrary"), 48),
        name="attention",
    )(*args)
    return res if emit_kv else res[0]


def _pool_kernel(u_ref, w_ref, sc_ref, o_ref, pad_ref, *, rows, chunk, seq_p, seq_s, groups_p):
    gw = u_ref.shape[1]
    ext = chunk + 2 * HALO
    seq = jnp.where(pl.program_id(0) < groups_p, seq_p, seq_s)
    pad_ref[0:HALO, :] = jnp.zeros((HALO, gw), F32)
    pad_ref[HALO + rows:, :] = jnp.zeros((HALO, gw), F32)
    pad_ref[HALO:HALO + rows, :] = u_ref[...]

    for gi, win in enumerate(POOL_WINDOWS):
        hw = win // 2

        @pl.when(pl.program_id(1) == gi)
        def _(hw=hw):
            def body(ci, carry):
                r0 = pl.multiple_of(ci * chunk, chunk)
                xe = pad_ref[pl.ds(r0, ext), :]
                t = (r0 + lax.broadcasted_iota(jnp.int32, (chunk, gw), 0)) & (seq - 1)
                x0 = xe[HALO:HALO + chunk]
                acc = x0
                for d in range(-hw, hw):
                    if d == 0:
                        continue
                    xs = pltpu.roll(xe, (-d) % ext, axis=0)[HALO:HALO + chunk]
                    acc = acc + jnp.where((t >= -d) if d < 0 else (t < seq - d), xs, 0.0)
                cnt = (jnp.minimum(t + hw, seq) - jnp.maximum(t - hw, 0)).astype(F32)
                pooled = acc / cnt - x0
                y = jnp.dot(pooled.astype(BF16), w_ref[...], preferred_element_type=F32)
                o_ref[pl.ds(r0, chunk), :] = (y * sc_ref[...]).astype(o_ref.dtype)
                return carry

            lax.fori_loop(0, rows // chunk, body, 0)


def _pool(z, layer, pool_w, pool_scale, *, rows, groups_p, seq_p, seq_s, col0, bw):
    r = z.shape[0]
    g = pool_w.shape[1]
    gw = bw // g
    cb0 = col0 // gw
    chunk = min(256, rows)
    return pl.pallas_call(
        functools.partial(_pool_kernel, rows=rows, chunk=chunk, seq_p=seq_p, seq_s=seq_s,
                          groups_p=groups_p),
        out_shape=jax.ShapeDtypeStruct((r, bw), BF16),
        grid=(r // rows, g),
        in_specs=[pl.BlockSpec((rows, gw), lambda i, j: (i, cb0 + j)),
                  pl.BlockSpec((None, None, gw, gw), lambda i, j: (layer, j, 0, 0)),
                  pl.BlockSpec((None, 1, gw), lambda i, j: (layer, 0, j))],
        out_specs=pl.BlockSpec((rows, gw), lambda i, j: (i, j)),
        scratch_shapes=[pltpu.VMEM((rows + 2 * HALO, gw), F32)],
        compiler_params=_params(("arbitrary", "arbitrary"), 40),
        name="pool",
    )(z, pool_w, pool_scale.reshape(pool_scale.shape[0], 1, bw))


def _merge_kernel(*refs, n_cast):
    h_ref, g0, g1, g2, g3, o0, o1, o2, o3, wb_ref = refs[:10]
    out_ref = refs[10 + n_cast]
    _run_side_casts(refs[10:10 + n_cast], refs[11 + n_cast:])
    h = h_ref[...]
    acc = None
    for n, (g_ref, o_ref) in enumerate(((g0, o0), (g1, o1), (g2, o2), (g3, o3))):
        gate = _sigmoid(jnp.dot(h, g_ref[...], preferred_element_type=F32))
        y = jnp.dot(o_ref[...], wb_ref[n], preferred_element_type=F32)
        acc = gate * y if acc is None else acc + gate * y
    out_ref[...] = acc.astype(out_ref.dtype)


def _gate_merge(h, w_in, col_g, branches, w_branch, side, tm=512, tn=256):
    m, d = h.shape
    bw = w_branch.shape[1]
    dm = w_branch.shape[2]
    tm, tn = min(tm, m), min(tn, dm)
    nm = m // tm
    gspecs = [pl.BlockSpec((d, tn),
                           functools.partial(lambda j, i, n: (0, (col_g + n * dm) // tn + j), n=n))
              for n in range(N_BRANCH)]
    ospecs = [pl.BlockSpec((tm, bw), lambda j, i: (i, 0)) for _ in range(N_BRANCH)]
    c_in, c_args, c_out, c_shapes = _side_casts(side, (dm // tn) * nm, lambda j, i: j * nm + i)
    res = pl.pallas_call(
        functools.partial(_merge_kernel, n_cast=len(side)),
        out_shape=[jax.ShapeDtypeStruct((m, dm), BF16)] + c_shapes,
        grid=(dm // tn, nm),
        in_specs=[pl.BlockSpec((tm, d), lambda j, i: (i, 0))] + gspecs + ospecs
                 + [pl.BlockSpec((N_BRANCH, bw, tn), lambda j, i: (0, 0, j))] + c_in,
        out_specs=[pl.BlockSpec((tm, tn), lambda j, i: (i, j))] + c_out,
        compiler_params=_params(("arbitrary", "arbitrary"), 58),
        name="gate_merge",
    )(h, w_in, w_in, w_in, w_in, *branches, w_branch, *c_args)
    return res[0], res[1:]


def _out_residual_kernel(m_ref, w_ref, x_ref, g_ref, gate_ref, o_ref):
    y = jnp.dot(m_ref[...], w_ref[...], preferred_element_type=F32)
    o_ref[...] = x_ref[...] + gate_ref[...] * _rms(y, g_ref[...])


def _out_residual(rng, merged, w_out, gain, mod_l, k_gate, p_rows, s_rows, tm, into=None):
    src, src_row0, uni_row0, nrows = rng
    r, d = merged.shape
    st0, ut0 = src_row0 // tm, uni_row0 // tm
    in_specs = [pl.BlockSpec((tm, d), lambda i: (ut0 + i, 0)),
                pl.BlockSpec((d, d), lambda i: (0, 0), pipeline_mode=pl.Buffered(1)),
                pl.BlockSpec((tm, d), lambda i: (st0 + i, 0)),
                pl.BlockSpec((1, d), lambda i: (0, 0)),
                _mod_spec(k_gate, d, tm, p_rows, s_rows, ut0)]
    args = [merged, w_out, src, gain.reshape(1, d), mod_l]
    body, aliases = _fill(_out_residual_kernel, in_specs, args, into)
    return pl.pallas_call(
        body,
        out_shape=jax.ShapeDtypeStruct((r, d), F32),
        grid=(nrows // tm,),
        in_specs=in_specs,
        out_specs=pl.BlockSpec((tm, d), lambda i: (ut0 + i, 0)),
        input_output_aliases=aliases,
        compiler_params=_params(("arbitrary",), 60),
        name="out_residual",
    )(*args)


def _ffn_kernel(x_ref, gpre_ref, sh_ref, sc_ref, gate_ref, gpost_ref, wg_ref, wu_ref, wd_ref,
                o_ref, h_ref, stat_ref, *, slab, overlap):
    k = pl.program_id(1)
    tm, d = x_ref.shape
    nslab = tm // slab

    def rows(i):
        return pl.ds(pl.multiple_of(i * slab, slab), slab)

    @pl.when(k == 0)
    def _():
        _row_rinv(x_ref, stat_ref)

        def body(i, carry):
            h = x_ref[rows(i), :] * _rinv_rows(stat_ref, rows(i), d) * gpre_ref[...]
            h_ref[rows(i), :] = (h * (1.0 + sc_ref[...]) + sh_ref[...]).astype(BF16)
            o_ref[rows(i), :] = jnp.zeros((slab, d), F32)
            return carry
        lax.fori_loop(0, nslab, body, 0, unroll=2)

    last = pl.num_programs(1) - 1

    def chunk(lo):
        h = h_ref[...]
        a = jnp.dot(h, wg_ref[:, lo:], preferred_element_type=F32)
        b = jnp.dot(h, wu_ref[:, lo:], preferred_element_type=F32)
        act = (a * _sigmoid(a) * b).astype(BF16)
        o_ref[...] += jnp.dot(act, wd_ref[lo:, :], preferred_element_type=F32)

    if overlap:
        pl.when(k < last)(lambda: chunk(0))
        pl.when(k == last)(lambda: chunk(overlap))
    else:
        chunk(0)

    @pl.when(k == last)
    def _():
        _row_rinv(o_ref, stat_ref)

        def body(i, carry):
            y = o_ref[rows(i), :] * _rinv_rows(stat_ref, rows(i), d) * gpost_ref[...]
            o_ref[rows(i), :] = x_ref[rows(i), :] + gate_ref[...] * y
            return carry
        lax.fori_loop(0, nslab, body, 0, unroll=2)


def _ffn(x, row0, nrows, g_pre, g_post, mod_l, wg, wu, wd, p_rows, s_rows, tm=512):
    d = x.shape[1]
    dff = wg.shape[1]
    tm, tc = min(tm, nrows), min(FFN_CHUNK, dff)
    nk = pl.cdiv(dff, tc)
    overlap = nk * tc - dff

    def chunk_start(k):
        return pl.multiple_of(jnp.minimum(k * tc, dff - tc), math.gcd(tc, dff))

    t0 = row0 // tm
    return pl.pallas_call(
        functools.partial(_ffn_kernel, slab=math.gcd(16, tm), overlap=overlap),
        out_shape=jax.ShapeDtypeStruct((nrows, d), F32),
        grid=(nrows // tm, nk),
        in_specs=[pl.BlockSpec((tm, d), lambda i, k: (t0 + i, 0), pipeline_mode=pl.Buffered(1)),
                  pl.BlockSpec((1, d), lambda i, k: (0, 0)),
                  _mod_spec(3, d, tm, p_rows, s_rows, t0),
                  _mod_spec(4, d, tm, p_rows, s_rows, t0),
                  _mod_spec(5, d, tm, p_rows, s_rows, t0),
                  pl.BlockSpec((1, d), lambda i, k: (0, 0)),
                  pl.BlockSpec((pl.Element(d), pl.Element(tc)),
                               lambda i, k: (0, chunk_start(k))),
                  pl.BlockSpec((pl.Element(d), pl.Element(tc)),
                               lambda i, k: (0, chunk_start(k))),
                  pl.BlockSpec((pl.Element(tc), pl.Element(d)),
                               lambda i, k: (chunk_start(k), 0))],
        out_specs=pl.BlockSpec((tm, d), lambda i, k: (i, 0)),
        scratch_shapes=[pltpu.VMEM((tm, d), BF16), pltpu.VMEM((tm, 128), F32)],
        compiler_params=_params(("arbitrary", "arbitrary"), 60),
        name="ffn",
    )(x, g_pre.reshape(1, d), mod_l, mod_l, mod_l, g_post.reshape(1, d), wg, wu, wd)


def kernel(x_prompt, x_sample, c, cache_k, cache_v, state_lru, c_ctx, w_mod, b_mod, g_pre_mix, g_post_mix, g_pre_ffn, g_post_ffn, w_in, conv_w, conv_b, lru_wa, lru_ba, lru_wx, lru_bx, lru_lambda, q_norm, k_norm, pool_w, pool_scale, w_branch, w_out, w_ffn_gate, w_ffn_up, w_ffn_down):
    nbp, seq_p, d = x_prompt.shape
    nbs, seq_s, _ = x_sample.shape
    depth = w_in.shape[0]
    bw = w_branch.shape[2]
    cache_len, n_kv, hd = cache_k.shape[2:]
    grp = bw // hd // n_kv
    kvw = n_kv * hd
    p_rows, s_rows = nbp * seq_p, nbs * seq_s
    rows = p_rows + s_rows
    assert seq_p & (seq_p - 1) == 0 and seq_s & (seq_s - 1) == 0
    assert p_rows % seq_s == 0, "row groups of seq_s rows must tile the prompt rows"
    col_a, col_bx, col_by, col_q = 0, bw, 2 * bw, 3 * bw
    col_k = col_q + bw
    col_v = col_k + kvw
    col_d = col_v + kvw
    col_g = col_d + bw
    zw = col_g

    def rows_tile(pref):
        return math.gcd(pref, math.gcd(p_rows, seq_s))

    w_mix_l = w_in[0, :, :zw].astype(BF16)
    w_in_l = w_branch_l = w_out_l = None
    w_branch_2d = w_branch.reshape(depth, N_BRANCH * bw, d)
    wa_b = lru_wa.astype(BF16)
    wx_b = lru_wx.astype(BF16)
    pool_w_b = pool_w.astype(BF16)

    gw = bw // FOURIER_GROUPS
    cc, sc = _dft_tables(gw)
    tab_c = jnp.concatenate([cc, sc], axis=1).astype(BF16)
    cp, sp = _dft_tables(seq_p)
    dft_p = jnp.concatenate([cp, -sp], axis=1).astype(BF16)
    cs, ss = _dft_tables(seq_s)
    dft_s = jnp.concatenate([cs, -ss], axis=1).astype(BF16)
    rope_tabs = _rope_tables(seq_s, hd)

    cvec = jnp.concatenate([c_ctx[None, :], c, jnp.zeros((8 - 1 - nbs, d), c.dtype)], axis=0)
    mod = _mod_all(cvec.astype(F32), w_mod, b_mod)

    cache_kt = jnp.transpose(cache_k, (0, 1, 3, 2, 4)).astype(F32)
    cache_vt = jnp.transpose(cache_v, (0, 1, 3, 2, 4)).astype(F32)

    x_ranges = [(x_prompt.reshape(p_rows, d).astype(F32), 0, 0, p_rows),
                (x_sample.reshape(s_rows, d).astype(F32), 0, p_rows, s_rows)]
    tm_norm = rows_tile(256)
    k_list, v_list, s_list = [], [], []
    y_p = y_s = None
    for l in range(depth):
        mod_l = mod[l].reshape(8, 1, 6 * d)
        h = None
        for rng in x_ranges:
            h = _prenorm(rng, rows, g_pre_mix[l], mod_l, 0, p_rows, seq_s, tm_norm, into=h)
        if l == 0:
            z, (w_in_l, w_branch_l, w_out_l) = _matmul_cols(
                h, w_mix_l, zw, rows_tile(1024), 512, F32, 56, "in_proj",
                side=[(w_in, 0), (w_branch_2d, 0), (w_out, 0)])
            w_branch_l = w_branch_l.reshape(N_BRANCH, bw, d)
        else:
            z, _ = _matmul_cols(h, w_in_l, zw, rows_tile(1024), 512, F32, 48, "in_proj")

        pq_p = _fourier_stage1(z, tab_c, 0, nbp, seq_p, bw, col_a)
        o_a = _matmul_batched_rhs(dft_p, pq_p, 512, 1024, BF16, 48, "fourier2p", rows, 0)
        pq_s = _fourier_stage1(z, tab_c, p_rows, nbs, seq_s, bw, col_a)
        o_a = _matmul_batched_rhs(dft_s, pq_s, 512, 512, BF16, 48, "fourier2s", rows, p_rows,
                                  into=o_a)

        lru_args = (conv_w, conv_b, wa_b, wx_b, lru_ba, lru_bx, lru_lambda)
        o_b, st = _lru(z, l, *lru_args, None, row0=0, rows=p_rows, ngroups=1, seq=seq_p,
                       nseg=nbp, seg_len=seq_p, col_x=col_bx, col_y=col_by, bw=bw)
        seg = min(128, seq_s)
        o_b = _lru(z, l, *lru_args, state_lru[:, l].astype(F32), row0=p_rows, rows=seq_s,
                   ngroups=nbs, seq=seq_s, nseg=seq_s // seg, seg_len=seg,
                   col_x=col_bx, col_y=col_by, bw=bw, into=o_b)
        s_list.append(jnp.transpose(st, (1, 0, 2)))

        att = dict(n_kv=n_kv, grp=grp, hd=hd, col_q=col_q, col_k=col_k, col_v=col_v)
        o_c, k_p, v_p = _attention(z, l, q_norm, k_norm, row0=0, nb=nbp, seq=seq_p, tq=seq_p,
                                   emit_kv=True, **att)
        o_c = _attention(z, l, q_norm, k_norm, row0=p_rows, nb=nbs, seq=seq_s,
                         tq=min(256, seq_s), rope_tabs=rope_tabs,
                         cache=(cache_kt[:, l], cache_vt[:, l]), into=o_c, **att)
        k_list.append(k_p.reshape(nbp, seq_p, n_kv, hd))
        v_list.append(v_p.reshape(nbp, seq_p, n_kv, hd))

        o_d = _pool(z, l, pool_w_b, pool_scale, rows=seq_s, groups_p=p_rows // seq_s,
                    seq_p=seq_p, seq_s=seq_s, col0=col_d, bw=bw)

        side = [(w_ffn_gate, l), (w_ffn_up, l), (w_ffn_down, l)]
        if l + 1 < depth:
            side += [(w_in, l + 1), (w_branch_2d, l + 1), (w_out, l + 1)]
        merged, cast = _gate_merge(h, w_in_l, col_g, (o_a, o_b, o_c, o_d), w_branch_l,
                                   side=side, tm=rows_tile(512))
        x = None
        for rng in x_ranges:
            x = _out_residual(rng, merged, w_out_l, g_post_mix[l], mod_l, 2, p_rows, seq_s,
                              tm_norm, into=x)
        ffn = functools.partial(_ffn, x, g_pre=g_pre_ffn[l], g_post=g_post_ffn[l],
                                mod_l=mod_l, wg=cast[0], wu=cast[1], wd=cast[2], p_rows=p_rows,
                                s_rows=seq_s, tm=rows_tile(512))
        if l + 1 < depth:
            w_in_l, w_branch_l, w_out_l = cast[3], cast[4].reshape(N_BRANCH, bw, d), cast[5]
            x_ranges = [(ffn(0, rows), 0, 0, rows)]
        else:
            y_p, y_s = ffn(0, p_rows), ffn(p_rows, s_rows)

    y_p = y_p.reshape(nbp, seq_p, d).astype(x_prompt.dtype)
    y_s = y_s.reshape(nbs, seq_s, d).astype(x_sample.dtype)
    new_k = jnp.stack(k_list, axis=1).astype(cache_k.dtype)
    new_v = jnp.stack(v_list, axis=1).astype(cache_v.dtype)
    new_s = jnp.stack(s_list, axis=1).astype(state_lru.dtype)
    return y_p, y_s, new_k, new_v, new_s
```

```python
import functools
import math

import jax
import jax.numpy as jnp
from jax import lax
from jax.experimental import pallas as pl
from jax.experimental.pallas import tpu as pltpu

F32 = jnp.float32
BF16 = jnp.bfloat16
MIB = 1024 * 1024

EPS = 1e-6
GRID_W = 64
ROPE_THETA = 10000.0
LRU_C = 8.0
FOURIER_GROUPS = 4
POOL_WINDOWS = (2, 4, 8, 16)
N_BRANCH = 4
HALO = 8
CONV_LEFT = 2
FFN_CHUNK = 512
SEG_PAD = 8


def _params(semantics, vmem_mib):
    return pltpu.CompilerParams(dimension_semantics=semantics,
                                vmem_limit_bytes=vmem_mib * MIB)


def _rms(x, gain):
    ms = jnp.mean(x * x, axis=-1, keepdims=True)
    return x * lax.rsqrt(ms + EPS) * gain


def _sigmoid(x):
    return 0.5 * jnp.tanh(0.5 * x) + 0.5


def _row_rinv(src_ref, stat_ref):
    n, d = src_ref.shape

    def body(i, carry):
        sl = pl.ds(pl.multiple_of(i * 8, 8), 8)
        x = src_ref[sl, :]
        ms = jnp.sum(x * x, axis=-1, keepdims=True) * (1.0 / d)
        stat_ref[sl, :] = jnp.broadcast_to(lax.rsqrt(ms + EPS), (8, stat_ref.shape[1]))
        return carry
    lax.fori_loop(0, n // 8, body, 0, unroll=8)


def _rinv_rows(stat_ref, sl, d):
    return jnp.tile(stat_ref[sl, :], (1, d // stat_ref.shape[1]))


def _softplus(x):
    return jnp.maximum(x, 0.0) + jnp.log1p(jnp.exp(-jnp.abs(x)))


def _group_of_tile(i, tm, p_rows, s_rows):
    r = i * tm
    return jnp.where(r < p_rows, 0, 1 + (r - p_rows) // s_rows)


def _mod_spec(k, d, tm, p_rows, s_rows, tile0=0):
    return pl.BlockSpec((None, 1, d),
                        lambda i, *_: (_group_of_tile(tile0 + i, tm, p_rows, s_rows), 0, k))


def _fill(kernel, in_specs, args, into):
    if into is None:
        return kernel, {}
    n = len(args)
    in_specs.append(pl.BlockSpec(memory_space=pl.ANY))
    args.append(into)

    def body(*refs):
        return kernel(*refs[:n], *refs[n + 1:])
    return body, {n: 0}


def _side_casts(weights, nsteps, step_of):
    in_specs, args, out_specs, out_shapes = [], [], [], []
    for w, layer in weights:
        _, rows, cols = w.shape
        br = next(b for b in range(16, rows + 1, 16) if rows % b == 0 and rows // b <= nsteps)
        last = rows // br - 1
        in_specs.append(pl.BlockSpec(
            (None, br, cols),
            functools.partial(lambda *g, layer, last: (layer, jnp.minimum(step_of(*g), last), 0),
                              layer=layer, last=last)))
        out_specs.append(pl.BlockSpec(
            (br, cols),
            functools.partial(lambda *g, last: (jnp.minimum(step_of(*g), last), 0), last=last)))
        args.append(w)
        out_shapes.append(jax.ShapeDtypeStruct((rows, cols), BF16))
    return in_specs, args, out_specs, out_shapes


def _run_side_casts(src_refs, dst_refs):
    for src, dst in zip(src_refs, dst_refs):
        dst[...] = src[...].astype(BF16)


def _mod_kernel(c_ref, w_ref, b_ref, o_ref):
    c = c_ref[...]
    s = (c * jax.nn.sigmoid(c)).astype(BF16)
    o_ref[...] = jnp.dot(s, w_ref[...].astype(BF16),
                         preferred_element_type=F32) + b_ref[...]


def _mod_all(cvec, w_mod, b_mod, tn=512):
    depth, d, n = w_mod.shape
    return pl.pallas_call(
        _mod_kernel,
        out_shape=jax.ShapeDtypeStruct((depth, 8, n), F32),
        grid=(depth, n // tn),
        in_specs=[pl.BlockSpec((8, d), lambda l, j: (0, 0)),
                  pl.BlockSpec((None, d, tn), lambda l, j: (l, 0, j)),
                  pl.BlockSpec((None, 1, tn), lambda l, j: (l, 0, j))],
        out_specs=pl.BlockSpec((None, 8, tn), lambda l, j: (l, 0, j)),
        compiler_params=_params(("arbitrary", "arbitrary"), 40),
        name="mod",
    )(cvec, w_mod, b_mod.reshape(depth, 1, n))


def _prenorm_kernel(x_ref, g_ref, sh_ref, sc_ref, o_ref, stat_ref):
    tm, d = x_ref.shape
    slab = math.gcd(16, tm)
    _row_rinv(x_ref, stat_ref)

    def body(i, carry):
        sl = pl.ds(pl.multiple_of(i * slab, slab), slab)
        h = x_ref[sl, :] * _rinv_rows(stat_ref, sl, d) * g_ref[...]
        o_ref[sl, :] = (h * (1.0 + sc_ref[...]) + sh_ref[...]).astype(o_ref.dtype)
        return carry
    lax.fori_loop(0, tm // slab, body, 0, unroll=2)


def _prenorm(rng, total_rows, gain, mod_l, k_shift, p_rows, s_rows, tm, into=None):
    src, src_row0, uni_row0, nrows = rng
    d = src.shape[1]
    st0, ut0 = src_row0 // tm, uni_row0 // tm
    in_specs = [pl.BlockSpec((tm, d), lambda i: (st0 + i, 0)),
                pl.BlockSpec((1, d), lambda i: (0, 0)),
                _mod_spec(k_shift, d, tm, p_rows, s_rows, ut0),
                _mod_spec(k_shift + 1, d, tm, p_rows, s_rows, ut0)]
    args = [src, gain.reshape(1, d), mod_l, mod_l]
    body, aliases = _fill(_prenorm_kernel, in_specs, args, into)
    return pl.pallas_call(
        body,
        out_shape=jax.ShapeDtypeStruct((total_rows, d), BF16),
        grid=(nrows // tm,),
        in_specs=in_specs,
        out_specs=pl.BlockSpec((tm, d), lambda i: (ut0 + i, 0)),
        scratch_shapes=[pltpu.VMEM((tm, 128), F32)],
        input_output_aliases=aliases,
        compiler_params=_params(("arbitrary",), 40),
        name="prenorm",
    )(*args)


def _mm_kernel(a_ref, b_ref, o_ref):
    o_ref[...] = jnp.dot(a_ref[...], b_ref[...],
                         preferred_element_type=F32).astype(o_ref.dtype)


def _mm_side_kernel(*refs, n_cast):
    a_ref, b_ref = refs[:2]
    _run_side_casts(refs[2:2 + n_cast], refs[3 + n_cast:])
    _mm_kernel(a_ref, b_ref, refs[2 + n_cast])


def _matmul_cols(a, w, n_cols, tm, tn, out_dtype, vmem_mib, name, side=()):
    m, k = a.shape
    tm, tn = min(tm, m), min(tn, n_cols)
    nn = n_cols // tn
    c_in, c_args, c_out, c_shapes = _side_casts(side, (m // tm) * nn, lambda i, j: i * nn + j)
    res = pl.pallas_call(
        functools.partial(_mm_side_kernel, n_cast=len(side)),
        out_shape=[jax.ShapeDtypeStruct((m, n_cols), out_dtype)] + c_shapes,
        grid=(m // tm, nn),
        in_specs=[pl.BlockSpec((tm, k), lambda i, j: (i, 0)),
                  pl.BlockSpec((k, tn), lambda i, j: (0, j))] + c_in,
        out_specs=[pl.BlockSpec((tm, tn), lambda i, j: (i, j))] + c_out,
        compiler_params=_params(("arbitrary", "arbitrary"), vmem_mib),
        name=name,
    )(a, w, *c_args)
    return res[0], res[1:]


def _matmul_batched_rhs(a, b, tm, tn, out_dtype, vmem_mib, name, total_rows, row0, into=None):
    m, k = a.shape
    nb, _, n = b.shape
    tm, tn = min(tm, m), min(tn, n)
    t0, nt = row0 // tm, m // tm
    in_specs = [pl.BlockSpec((tm, k), lambda bi, i, j: (i, 0)),
                pl.BlockSpec((None, k, tn), lambda bi, i, j: (bi, 0, j))]
    args = [a, b]
    body, aliases = _fill(_mm_kernel, in_specs, args, into)
    return pl.pallas_call(
        body,
        out_shape=jax.ShapeDtypeStruct((total_rows, n), out_dtype),
        grid=(nb, nt, n // tn),
        in_specs=in_specs,
        out_specs=pl.BlockSpec((tm, tn), lambda bi, i, j: (t0 + bi * nt + i, j)),
        input_output_aliases=aliases,
        compiler_params=_params(("arbitrary", "arbitrary", "arbitrary"), vmem_mib),
        name=name,
    )(*args)


FOLD_MIN_SEQ = 1024
FLIP_BLOCK = 128


def _fold_cols(seq):
    return -(-(seq // 2 + 1) // 128) * 128


def _dft_matrix(n, kcols, sign, fold, split=64):
    col = jnp.arange(2 * kcols, dtype=jnp.int32)
    t = col % kcols
    phase = (col // kcols).astype(F32) * (sign * 0.5 * math.pi)
    w = jnp.full((2 * kcols,), 1.0 / math.sqrt(n), F32)
    if fold:
        w = w * jnp.where((t == 0) | (t == n // 2), 0.5, jnp.where(t < n // 2, 1.0, 0.0))

    def angle(rows, period):
        r = jnp.arange(rows, dtype=jnp.int32)
        return ((r[:, None] * t[None, :]) % period).astype(F32) * (2.0 * math.pi / period)

    if n <= split or n % split:
        return jnp.cos(angle(n, n) + phase) * w
    ang_a = angle(n // split, n // split)
    ang_b = angle(split, n) + phase
    tab = (jnp.cos(ang_a)[:, None, :] * jnp.cos(ang_b)[None]
           - jnp.sin(ang_a)[:, None, :] * jnp.sin(ang_b)[None]) * w
    return tab.reshape(n, 2 * kcols)


def _four1_kernel(u_ref, t_ref, o_ref, *, seq, gw, kfold):
    u = u_ref[...].astype(BF16)
    pq = jnp.dot(u, t_ref[...], preferred_element_type=F32)
    if not kfold:
        o_ref[0:seq, :] = pq[:, :gw].astype(o_ref.dtype)
        o_ref[seq:2 * seq, :] = pq[:, gw:].astype(o_ref.dtype)
        return
    pqb = pq.astype(BF16)
    nblk = seq // FLIP_BLOCK
    r = lax.broadcasted_iota(jnp.int32, (FLIP_BLOCK, FLIP_BLOCK), 0)
    c = lax.broadcasted_iota(jnp.int32, (FLIP_BLOCK, FLIP_BLOCK), 1)
    anti = jnp.where(r + c == FLIP_BLOCK - 1, 1.0, 0.0).astype(BF16)
    flipped = jnp.concatenate(
        [jnp.dot(anti, pqb[(nblk - 1 - i) * FLIP_BLOCK:(nblk - i) * FLIP_BLOCK],
                 preferred_element_type=F32) for i in range(kfold // FLIP_BLOCK)], axis=0)
    row = lax.broadcasted_iota(jnp.int32, flipped.shape, 0)
    head = pqb[0:kfold].astype(F32)
    rev = jnp.where(row == 0, head[0:1], pltpu.roll(flipped, 1, axis=0))
    o_ref[0:kfold, :] = (head[:, :gw] + rev[:, :gw]).astype(o_ref.dtype)
    o_ref[kfold:2 * kfold, :] = (head[:, gw:] - rev[:, gw:]).astype(o_ref.dtype)


def _fourier_stage1(z, tab, row0, nb, seq, bw, col0):
    gw = bw // FOURIER_GROUPS
    rb0, cb0 = row0 // seq, col0 // gw
    kfold = _fold_cols(seq) if seq >= FOLD_MIN_SEQ else 0
    krows = 2 * (kfold or seq)
    return pl.pallas_call(
        functools.partial(_four1_kernel, seq=seq, gw=gw, kfold=kfold),
        out_shape=jax.ShapeDtypeStruct((nb, krows, bw), BF16),
        grid=(nb, FOURIER_GROUPS),
        in_specs=[pl.BlockSpec((seq, gw), lambda b, g: (rb0 + b, cb0 + g)),
                  pl.BlockSpec((gw, 2 * gw), lambda b, g: (0, 0))],
        out_specs=pl.BlockSpec((None, krows, gw), lambda b, g: (b, 0, g)),
        compiler_params=_params(("arbitrary", "arbitrary"), 48),
        name="fourier1",
    )(z, tab)


def _fourier_matrix(seq):
    if seq >= FOLD_MIN_SEQ:
        return _dft_matrix(seq, _fold_cols(seq), 1.0, True).astype(BF16)
    return _dft_matrix(seq, seq, 1.0, False).astype(BF16)


def _lru_kernel(*refs, seq, rows, nseg, seg_len, chained, chunk):
    if chained:
        (zx_ref, zy_ref, cw_ref, cb_ref, wa_ref, wx_ref, ba_ref, bx_ref, lam_ref,
         h0_ref, o_ref, pad_ref, a0_ref, a1_ref, b0_ref, b1_ref) = refs
        st_ref = None
    else:
        (zx_ref, zy_ref, cw_ref, cb_ref, wa_ref, wx_ref, ba_ref, bx_ref, lam_ref,
         o_ref, st_ref, pad_ref, a0_ref, a1_ref, b0_ref, b1_ref) = refs
        h0_ref = None
    a_refs, b_refs = (a0_ref, a1_ref), (b0_ref, b1_ref)
    cw_ch = zx_ref.shape[1]
    ext = chunk + 2 * HALO
    n_taps = cw_ref.shape[0]
    pitch = seg_len + SEG_PAD
    segs = chunk // seg_len

    pad_ref[0:HALO, :] = jnp.zeros((HALO, cw_ch), F32)
    pad_ref[HALO + rows:, :] = jnp.zeros((HALO, cw_ch), F32)
    pad_ref[HALO:HALO + rows, :] = zx_ref[...]

    neg_sp = [-LRU_C * _softplus(-lam_ref[d:d + 1, :]) for d in range(2)]

    def seg_rows(ci, j):
        return pl.ds(pl.multiple_of(ci * (segs * pitch), 8) + j * pitch, seg_len)

    def gates(ci, carry):
        r0 = pl.multiple_of(ci * chunk, chunk)
        xe = pad_ref[pl.ds(r0, ext), :]
        t = (r0 + lax.broadcasted_iota(jnp.int32, (chunk, cw_ch), 0)) & (seq - 1)
        xc = jnp.zeros((chunk, cw_ch), F32) + cb_ref[...]
        for j in range(n_taps):
            d = j - CONV_LEFT
            if d == 0:
                xs = xe[HALO:HALO + chunk]
            else:
                xs = pltpu.roll(xe, (-d) % ext, axis=0)[HALO:HALO + chunk]
                xs = jnp.where((t >= -d) if d < 0 else (t < seq - d), xs, 0.0)
            xc = xc + xs * cw_ref[j:j + 1, :]
        xcb = xc.astype(BF16)
        for d in range(2):
            r = _sigmoid(jnp.dot(xcb, wa_ref[d], preferred_element_type=F32)
                         + ba_ref[d:d + 1, :])
            i = _sigmoid(jnp.dot(xcb, wx_ref[d], preferred_element_type=F32)
                         + bx_ref[d:d + 1, :])
            a = jnp.exp(r * neg_sp[d])
            bterm = jnp.sqrt(1.0 - a * a) * (i * xc)
            for j in range(segs):
                a_refs[d][seg_rows(ci, j), :] = a[j * seg_len:(j + 1) * seg_len]
                b_refs[d][seg_rows(ci, j), :] = bterm[j * seg_len:(j + 1) * seg_len]
        return carry

    lax.fori_loop(0, rows // chunk, gates, 0)

    def scan(direction):
        def body(s, carry):
            h, p = carry
            pos = s if direction == 0 else seg_len - 1 - s
            idx = pl.ds(pos, nseg, stride=pitch)
            a = a_refs[direction][idx, :]
            h = a * h + b_refs[direction][idx, :]
            b_refs[direction][idx, :] = h
            if chained:
                p = a * p
                a_refs[direction][idx, :] = p
            return h, p
        init = (jnp.zeros((nseg, cw_ch), F32), jnp.ones((nseg, cw_ch), F32))
        lax.fori_loop(0, seg_len, body, init, unroll=8)

    scan(0)
    scan(1)

    if chained:
        carry = h0_ref[0:1, :]
        for q in range(nseg):
            sl = slice(q * pitch, q * pitch + seg_len)
            hq = b_refs[0][sl, :] + a_refs[0][sl, :] * carry
            b_refs[0][sl, :] = hq
            carry = hq[seg_len - 1:seg_len, :]
        carry = h0_ref[1:2, :]
        for q in reversed(range(nseg)):
            sl = slice(q * pitch, q * pitch + seg_len)
            hq = b_refs[1][sl, :] + a_refs[1][sl, :] * carry
            b_refs[1][sl, :] = hq
            carry = hq[0:1, :]
    else:
        st_ref[0] = b_refs[0][pl.ds(seq - 1, nseg, stride=pitch), :]
        st_ref[1] = b_refs[1][pl.ds(0, nseg, stride=pitch), :]

    def emit(ci, carry):
        r0 = pl.multiple_of(ci * chunk, chunk)
        for j in range(segs):
            sl = pl.ds(r0 + j * seg_len, seg_len)
            hsum = b_refs[0][seg_rows(ci, j), :] + b_refs[1][seg_rows(ci, j), :]
            o_ref[sl, :] = (hsum * jax.nn.gelu(zy_ref[sl, :])).astype(o_ref.dtype)
        return carry

    lax.fori_loop(0, rows // chunk, emit, 0)


def _lru(z, layer, conv_w, conv_b, wa, wx, ba, bx, lam, h0, *, row0, rows, ngroups,
         seq, nseg, seg_len, col_x, col_y, bw, into=None):
    cb = wa.shape[-1]
    ncb = bw // cb
    chained = h0 is not None
    chunk = max(min(256, rows), seg_len)
    assert chunk % seg_len == 0 and rows % chunk == 0 and seg_len % 8 == 0
    assert chained or seg_len == seq
    rb0 = row0 // rows
    cx0, cy0 = col_x // cb, col_y // cb
    in_specs = [
        pl.BlockSpec((rows, cb), lambda g, c: (rb0 + g, cx0 + c)),
        pl.BlockSpec((rows, cb), lambda g, c: (rb0 + g, cy0 + c)),
        pl.BlockSpec((None, conv_w.shape[1], cb), lambda g, c: (layer, 0, c)),
        pl.BlockSpec((None, 1, cb), lambda g, c: (layer, 0, c)),
        pl.BlockSpec((None, 2, None, cb, cb), lambda g, c: (layer, 0, c, 0, 0)),
        pl.BlockSpec((None, 2, None, cb, cb), lambda g, c: (layer, 0, c, 0, 0)),
        pl.BlockSpec((None, 2, cb), lambda g, c: (layer, 0, c)),
        pl.BlockSpec((None, 2, cb), lambda g, c: (layer, 0, c)),
        pl.BlockSpec((None, 2, cb), lambda g, c: (layer, 0, c)),
    ]
    args = [z, z, conv_w, conv_b.reshape(conv_b.shape[0], 1, bw), wa, wx, ba, bx, lam]
    out_shape = [jax.ShapeDtypeStruct((z.shape[0], bw), BF16)]
    out_specs = [pl.BlockSpec((rows, cb), lambda g, c: (rb0 + g, c))]
    if chained:
        in_specs.append(pl.BlockSpec((None, 2, cb), lambda g, c: (g, 0, c)))
        args.append(h0)
    else:
        assert ngroups == 1
        out_shape.append(jax.ShapeDtypeStruct((2, rows // seq, bw), F32))
        out_specs.append(pl.BlockSpec((2, rows // seq, cb), lambda g, c: (0, 0, c)))
    kern = functools.partial(_lru_kernel, seq=seq, rows=rows, nseg=nseg, seg_len=seg_len,
                             chained=chained, chunk=chunk)
    body, aliases = _fill(kern, in_specs, args, into)
    res = pl.pallas_call(
        body,
        out_shape=out_shape,
        grid=(ngroups, ncb),
        in_specs=in_specs,
        out_specs=out_specs,
        scratch_shapes=[pltpu.VMEM((rows + 2 * HALO, cb), F32)]
                       + [pltpu.VMEM((nseg * (seg_len + SEG_PAD), cb), F32)] * 4,
        input_output_aliases=aliases,
        compiler_params=_params(("arbitrary", "arbitrary"), 48),
        name="rglru",
    )(*args)
    return res if not chained else res[0]


def _rope_tables(seq, hd):
    rows = seq // GRID_W
    row = jnp.repeat(jnp.arange(rows), GRID_W).astype(F32)
    col = jnp.tile(jnp.arange(GRID_W), rows).astype(F32)
    ax = hd // 2
    inv = ROPE_THETA ** (-jnp.arange(0, ax, 2, dtype=F32) / ax)
    ang_r = row[:, None] * inv[None, :]
    ang_c = col[:, None] * inv[None, :]
    cos = jnp.concatenate([jnp.cos(ang_r)] * 2 + [jnp.cos(ang_c)] * 2, axis=-1)
    sin = jnp.concatenate([-jnp.sin(ang_r), jnp.sin(ang_r),
                           -jnp.sin(ang_c), jnp.sin(ang_c)], axis=-1)
    return cos, sin


def _rotate(x, cos, sin):
    hd = x.shape[1]
    q = hd // 4
    lane = lax.broadcasted_iota(jnp.int32, x.shape, 1)
    ahead = pltpu.roll(x, hd - q, axis=1)
    behind = pltpu.roll(x, q, axis=1)
    partner = jnp.where((lane & q) == 0, ahead, behind)
    return x * cos + partner * sin


def _attn_kernel(*refs, seq, cache_len, tq, grp, hd, rope, emit_kv):
    refs = list(refs)
    zq_ref, zk_ref, zv_ref, qn_ref, kn_ref = refs[:5]
    pos = 5
    if rope:
        cq_ref, sq_ref, ck_ref, sk_ref = refs[pos:pos + 4]
        pos += 4
    if cache_len:
        kc_ref, vc_ref = refs[pos:pos + 2]
        pos += 2
    o_ref = refs[pos]
    pos += 1
    if emit_kv:
        ko_ref, vo_ref = refs[pos:pos + 2]
        pos += 2
    ks_ref, vt_ref = refs[pos:pos + 2]

    @pl.when(pl.program_id(2) == 0)
    def _():
        k = _rms(zk_ref[...], kn_ref[...])
        v = zv_ref[...]
        if emit_kv:
            ko_ref[...] = k
            vo_ref[...] = v
        if rope:
            k = _rotate(k, ck_ref[...], sk_ref[...])
        ks_ref[0:seq, :] = k.astype(BF16)
        vt_ref[:, 0:seq] = v.T.astype(BF16)
        if cache_len:
            ks_ref[seq:seq + cache_len, :] = kc_ref[...].astype(BF16)
            vt_ref[:, seq:seq + cache_len] = vc_ref[...].T.astype(BF16)

    scale = hd ** -0.5 * math.log2(math.e)
    qs = []
    for hh in range(grp):
        q = _rms(zq_ref[:, hh * hd:(hh + 1) * hd], qn_ref[...])
        if rope:
            q = _rotate(q, cq_ref[...], sq_ref[...])
        qs.append((q * scale).astype(BF16))
    q = jnp.concatenate(qs, axis=0)
    s = lax.dot_general(ks_ref[...], q, (((1,), (1,)), ((), ())),
                        preferred_element_type=F32)
    p = jnp.exp2(s - jnp.max(s, axis=0, keepdims=True))
    acc = jnp.dot(vt_ref[...], p.astype(BF16), preferred_element_type=F32)
    out = (acc / jnp.sum(p, axis=0, keepdims=True)).T
    for hh in range(grp):
        o_ref[:, hh * hd:(hh + 1) * hd] = out[hh * tq:(hh + 1) * tq].astype(o_ref.dtype)


def _attention(z, layer, q_norm, k_norm, *, row0, nb, seq, n_kv, grp, hd, col_q, col_k, col_v,
               tq, rope_tabs=None, cache=None, emit_kv=False, into=None):
    qw = grp * hd
    cache_len = 0 if cache is None else cache[0].shape[2]
    nq = seq // tq
    rbq0, rbk0 = row0 // tq, row0 // seq
    cq0, ck0, cv0 = col_q // qw, col_k // hd, col_v // hd
    in_specs = [
        pl.BlockSpec((tq, qw), lambda b, h, i: (rbq0 + b * nq + i, cq0 + h)),
        pl.BlockSpec((seq, hd), lambda b, h, i: (rbk0 + b, ck0 + h)),
        pl.BlockSpec((seq, hd), lambda b, h, i: (rbk0 + b, cv0 + h)),
        pl.BlockSpec((None, 1, hd), lambda b, h, i: (layer, 0, 0)),
        pl.BlockSpec((None, 1, hd), lambda b, h, i: (layer, 0, 0)),
    ]
    args = [z, z, z, q_norm.reshape(-1, 1, hd), k_norm.reshape(-1, 1, hd)]
    if rope_tabs is not None:
        cos, sin = rope_tabs
        in_specs += [pl.BlockSpec((tq, hd), lambda b, h, i: (i, 0)),
                     pl.BlockSpec((tq, hd), lambda b, h, i: (i, 0)),
                     pl.BlockSpec((seq, hd), lambda b, h, i: (0, 0)),
                     pl.BlockSpec((seq, hd), lambda b, h, i: (0, 0))]
        args += [cos, sin, cos, sin]
    if cache is not None:
        in_specs += [pl.BlockSpec((None, None, cache_len, hd), lambda b, h, i: (b, h, 0, 0)),
                     pl.BlockSpec((None, None, cache_len, hd), lambda b, h, i: (b, h, 0, 0))]
        args += list(cache)
    out_shape = [jax.ShapeDtypeStruct((z.shape[0], n_kv * qw), BF16)]
    out_specs = [pl.BlockSpec((tq, qw), lambda b, h, i: (rbq0 + b * nq + i, h))]
    if emit_kv:
        out_shape += [jax.ShapeDtypeStruct((nb * seq, n_kv * hd), F32)] * 2
        out_specs += [pl.BlockSpec((seq, hd), lambda b, h, i: (b, h))] * 2
    kern = functools.partial(_attn_kernel, seq=seq, cache_len=cache_len, tq=tq, grp=grp,
                             hd=hd, rope=rope_tabs is not None, emit_kv=emit_kv)
    body, aliases = _fill(kern, in_specs, args, into)
    res = pl.pallas_call(
        body,
        out_shape=out_shape,
        grid=(nb, n_kv, nq),
        in_specs=in_specs,
        out_specs=out_specs,
        scratch_shapes=[pltpu.VMEM((seq + cache_len, hd), BF16),
                        pltpu.VMEM((hd, seq + cache_len), BF16)],
        input_output_aliases=aliases,
        compiler_params=_params(("arbitrary", "arbitrary", "arbitrary"), 48),
        name="attention",
    )(*args)
    return res if emit_kv else res[0]


def _pool_kernel(u_ref, w_ref, sc_ref, o_ref, pad_ref, *, rows, chunk, seq_p, seq_s, groups_p):
    gw = u_ref.shape[1]
    ext = chunk + 2 * HALO
    seq = jnp.where(pl.program_id(0) < groups_p, seq_p, seq_s)
    pad_ref[0:HALO, :] = jnp.zeros((HALO, gw), F32)
    pad_ref[HALO + rows:, :] = jnp.zeros((HALO, gw), F32)
    pad_ref[HALO:HALO + rows, :] = u_ref[...]

    for gi, win in enumerate(POOL_WINDOWS):
        hw = win // 2

        @pl.when(pl.program_id(1) == gi)
        def _(hw=hw):
            def body(ci, carry):
                r0 = pl.multiple_of(ci * chunk, chunk)
                xe = pad_ref[pl.ds(r0, ext), :]
                t = (r0 + lax.broadcasted_iota(jnp.int32, (chunk, gw), 0)) & (seq - 1)
                x0 = xe[HALO:HALO + chunk]
                acc = x0
                for d in range(-hw, hw):
                    if d == 0:
                        continue
                    xs = pltpu.roll(xe, (-d) % ext, axis=0)[HALO:HALO + chunk]
                    acc = acc + jnp.where((t >= -d) if d < 0 else (t < seq - d), xs, 0.0)
                cnt = (jnp.minimum(t + hw, seq) - jnp.maximum(t - hw, 0)).astype(F32)
                pooled = acc / cnt - x0
                y = jnp.dot(pooled.astype(BF16), w_ref[...], preferred_element_type=F32)
                o_ref[pl.ds(r0, chunk), :] = (y * sc_ref[...]).astype(o_ref.dtype)
                return carry

            lax.fori_loop(0, rows // chunk, body, 0)


def _pool(z, layer, pool_w, pool_scale, *, rows, groups_p, seq_p, seq_s, col0, bw):
    r = z.shape[0]
    g = pool_w.shape[1]
    gw = bw // g
    cb0 = col0 // gw
    chunk = min(256, rows)
    return pl.pallas_call(
        functools.partial(_pool_kernel, rows=rows, chunk=chunk, seq_p=seq_p, seq_s=seq_s,
                          groups_p=groups_p),
        out_shape=jax.ShapeDtypeStruct((r, bw), BF16),
        grid=(r // rows, g),
        in_specs=[pl.BlockSpec((rows, gw), lambda i, j: (i, cb0 + j)),
                  pl.BlockSpec((None, None, gw, gw), lambda i, j: (layer, j, 0, 0)),
                  pl.BlockSpec((None, 1, gw), lambda i, j: (layer, 0, j))],
        out_specs=pl.BlockSpec((rows, gw), lambda i, j: (i, j)),
        scratch_shapes=[pltpu.VMEM((rows + 2 * HALO, gw), F32)],
        compiler_params=_params(("arbitrary", "arbitrary"), 40),
        name="pool",
    )(z, pool_w, pool_scale.reshape(pool_scale.shape[0], 1, bw))


def _merge_kernel(*refs, n_cast):
    h_ref, g0, g1, g2, g3, o0, o1, o2, o3, wb_ref = refs[:10]
    out_ref = refs[10 + n_cast]
    _run_side_casts(refs[10:10 + n_cast], refs[11 + n_cast:])
    h = h_ref[...]
    acc = None
    for n, (g_ref, o_ref) in enumerate(((g0, o0), (g1, o1), (g2, o2), (g3, o3))):
        gate = _sigmoid(jnp.dot(h, g_ref[...], preferred_element_type=F32))
        y = jnp.dot(o_ref[...], wb_ref[n], preferred_element_type=F32)
        acc = gate * y if acc is None else acc + gate * y
    out_ref[...] = acc.astype(out_ref.dtype)


def _gate_merge(h, w_in, col_g, branches, w_branch, side, tm=512, tn=256):
    m, d = h.shape
    bw = w_branch.shape[1]
    dm = w_branch.shape[2]
    tm, tn = min(tm, m), min(tn, dm)
    nm = m // tm
    gspecs = [pl.BlockSpec((d, tn),
                           functools.partial(lambda j, i, n: (0, (col_g + n * dm) // tn + j), n=n))
              for n in range(N_BRANCH)]
    ospecs = [pl.BlockSpec((tm, bw), lambda j, i: (i, 0)) for _ in range(N_BRANCH)]
    c_in, c_args, c_out, c_shapes = _side_casts(side, (dm // tn) * nm, lambda j, i: j * nm + i)
    res = pl.pallas_call(
        functools.partial(_merge_kernel, n_cast=len(side)),
        out_shape=[jax.ShapeDtypeStruct((m, dm), BF16)] + c_shapes,
        grid=(dm // tn, nm),
        in_specs=[pl.BlockSpec((tm, d), lambda j, i: (i, 0))] + gspecs + ospecs
                 + [pl.BlockSpec((N_BRANCH, bw, tn), lambda j, i: (0, 0, j))] + c_in,
        out_specs=[pl.BlockSpec((tm, tn), lambda j, i: (i, j))] + c_out,
        compiler_params=_params(("arbitrary", "arbitrary"), 58),
        name="gate_merge",
    )(h, w_in, w_in, w_in, w_in, *branches, w_branch, *c_args)
    return res[0], res[1:]


def _out_residual_kernel(m_ref, w_ref, x_ref, g_ref, gate_ref, o_ref):
    y = jnp.dot(m_ref[...], w_ref[...], preferred_element_type=F32)
    o_ref[...] = x_ref[...] + gate_ref[...] * _rms(y, g_ref[...])


def _out_residual(rng, merged, w_out, gain, mod_l, k_gate, p_rows, s_rows, tm, into=None):
    src, src_row0, uni_row0, nrows = rng
    r, d = merged.shape
    st0, ut0 = src_row0 // tm, uni_row0 // tm
    in_specs = [pl.BlockSpec((tm, d), lambda i: (ut0 + i, 0)),
                pl.BlockSpec((d, d), lambda i: (0, 0), pipeline_mode=pl.Buffered(1)),
                pl.BlockSpec((tm, d), lambda i: (st0 + i, 0)),
                pl.BlockSpec((1, d), lambda i: (0, 0)),
                _mod_spec(k_gate, d, tm, p_rows, s_rows, ut0)]
    args = [merged, w_out, src, gain.reshape(1, d), mod_l]
    body, aliases = _fill(_out_residual_kernel, in_specs, args, into)
    return pl.pallas_call(
        body,
        out_shape=jax.ShapeDtypeStruct((r, d), F32),
        grid=(nrows // tm,),
        in_specs=in_specs,
        out_specs=pl.BlockSpec((tm, d), lambda i: (ut0 + i, 0)),
        input_output_aliases=aliases,
        compiler_params=_params(("arbitrary",), 60),
        name="out_residual",
    )(*args)


def _ffn_kernel(x_ref, gpre_ref, sh_ref, sc_ref, gate_ref, gpost_ref, wg_ref, wu_ref, wd_ref,
                o_ref, h_ref, stat_ref, *, slab, overlap):
    k = pl.program_id(1)
    tm, d = x_ref.shape
    nslab = tm // slab

    def rows(i):
        return pl.ds(pl.multiple_of(i * slab, slab), slab)

    @pl.when(k == 0)
    def _():
        _row_rinv(x_ref, stat_ref)

        def body(i, carry):
            h = x_ref[rows(i), :] * _rinv_rows(stat_ref, rows(i), d) * gpre_ref[...]
            h_ref[rows(i), :] = (h * (1.0 + sc_ref[...]) + sh_ref[...]).astype(BF16)
            o_ref[rows(i), :] = jnp.zeros((slab, d), F32)
            return carry
        lax.fori_loop(0, nslab, body, 0, unroll=2)

    last = pl.num_programs(1) - 1

    def chunk(lo):
        h = h_ref[...]
        a = jnp.dot(h, wg_ref[:, lo:], preferred_element_type=F32)
        b = jnp.dot(h, wu_ref[:, lo:], preferred_element_type=F32)
        act = (a * _sigmoid(a) * b).astype(BF16)
        o_ref[...] += jnp.dot(act, wd_ref[lo:, :], preferred_element_type=F32)

    if overlap:
        pl.when(k < last)(lambda: chunk(0))
        pl.when(k == last)(lambda: chunk(overlap))
    else:
        chunk(0)

    @pl.when(k == last)
    def _():
        _row_rinv(o_ref, stat_ref)

        def body(i, carry):
            y = o_ref[rows(i), :] * _rinv_rows(stat_ref, rows(i), d) * gpost_ref[...]
            o_ref[rows(i), :] = x_ref[rows(i), :] + gate_ref[...] * y
            return carry
        lax.fori_loop(0, nslab, body, 0, unroll=2)


def _ffn(x, row0, nrows, g_pre, g_post, mod_l, wg, wu, wd, p_rows, s_rows, tm=512):
    d = x.shape[1]
    dff = wg.shape[1]
    tm, tc = min(tm, nrows), min(FFN_CHUNK, dff)
    nk = pl.cdiv(dff, tc)
    overlap = nk * tc - dff

    def chunk_start(k):
        return pl.multiple_of(jnp.minimum(k * tc, dff - tc), math.gcd(tc, dff))

    t0 = row0 // tm
    return pl.pallas_call(
        functools.partial(_ffn_kernel, slab=math.gcd(16, tm), overlap=overlap),
        out_shape=jax.ShapeDtypeStruct((nrows, d), F32),
        grid=(nrows // tm, nk),
        in_specs=[pl.BlockSpec((tm, d), lambda i, k: (t0 + i, 0), pipeline_mode=pl.Buffered(1)),
                  pl.BlockSpec((1, d), lambda i, k: (0, 0)),
                  _mod_spec(3, d, tm, p_rows, s_rows, t0),
                  _mod_spec(4, d, tm, p_rows, s_rows, t0),
                  _mod_spec(5, d, tm, p_rows, s_rows, t0),
                  pl.BlockSpec((1, d), lambda i, k: (0, 0)),
                  pl.BlockSpec((pl.Element(d), pl.Element(tc)),
                               lambda i, k: (0, chunk_start(k))),
                  pl.BlockSpec((pl.Element(d), pl.Element(tc)),
                               lambda i, k: (0, chunk_start(k))),
                  pl.BlockSpec((pl.Element(tc), pl.Element(d)),
                               lambda i, k: (chunk_start(k), 0))],
        out_specs=pl.BlockSpec((tm, d), lambda i, k: (i, 0)),
        scratch_shapes=[pltpu.VMEM((tm, d), BF16), pltpu.VMEM((tm, 128), F32)],
        compiler_params=_params(("arbitrary", "arbitrary"), 60),
        name="ffn",
    )(x, g_pre.reshape(1, d), mod_l, mod_l, mod_l, g_post.reshape(1, d), wg, wu, wd)


def kernel(x_prompt, x_sample, c, cache_k, cache_v, state_lru, c_ctx, w_mod, b_mod, g_pre_mix, g_post_mix, g_pre_ffn, g_post_ffn, w_in, conv_w, conv_b, lru_wa, lru_ba, lru_wx, lru_bx, lru_lambda, q_norm, k_norm, pool_w, pool_scale, w_branch, w_out, w_ffn_gate, w_ffn_up, w_ffn_down):
    nbp, seq_p, d = x_prompt.shape
    nbs, seq_s, _ = x_sample.shape
    depth = w_in.shape[0]
    bw = w_branch.shape[2]
    cache_len, n_kv, hd = cache_k.shape[2:]
    grp = bw // hd // n_kv
    kvw = n_kv * hd
    p_rows, s_rows = nbp * seq_p, nbs * seq_s
    rows = p_rows + s_rows
    assert seq_p & (seq_p - 1) == 0 and seq_s & (seq_s - 1) == 0
    assert p_rows % seq_s == 0, "row groups of seq_s rows must tile the prompt rows"
    col_a, col_bx, col_by, col_q = 0, bw, 2 * bw, 3 * bw
    col_k = col_q + bw
    col_v = col_k + kvw
    col_d = col_v + kvw
    col_g = col_d + bw
    zw = col_g

    def rows_tile(pref):
        return math.gcd(pref, math.gcd(p_rows, seq_s))

    w_mix_l = w_in[0, :, :zw].astype(BF16)
    w_in_l = w_branch_l = w_out_l = None
    w_branch_2d = w_branch.reshape(depth, N_BRANCH * bw, d)
    wa_b = lru_wa.astype(BF16)
    wx_b = lru_wx.astype(BF16)
    pool_w_b = pool_w.astype(BF16)

    gw = bw // FOURIER_GROUPS
    tab_c = _dft_matrix(gw, gw, -1.0, False).astype(BF16)
    dft_p = _fourier_matrix(seq_p)
    dft_s = _fourier_matrix(seq_s)
    rope_tabs = _rope_tables(seq_s, hd)

    cvec = jnp.concatenate([c_ctx[None, :], c, jnp.zeros((8 - 1 - nbs, d), c.dtype)], axis=0)
    mod = _mod_all(cvec.astype(F32), w_mod, b_mod)

    cache_kt = jnp.transpose(cache_k, (0, 1, 3, 2, 4)).astype(F32)
    cache_vt = jnp.transpose(cache_v, (0, 1, 3, 2, 4)).astype(F32)

    x_ranges = [(x_prompt.reshape(p_rows, d).astype(F32), 0, 0, p_rows),
                (x_sample.reshape(s_rows, d).astype(F32), 0, p_rows, s_rows)]
    tm_norm = rows_tile(256)
    k_list, v_list, s_list = [], [], []
    y_p = y_s = None
    for l in range(depth):
        mod_l = mod[l].reshape(8, 1, 6 * d)
        h = None
        for rng in x_ranges:
            h = _prenorm(rng, rows, g_pre_mix[l], mod_l, 0, p_rows, seq_s, tm_norm, into=h)
        if l == 0:
            z, (w_in_l, w_branch_l, w_out_l) = _matmul_cols(
                h, w_mix_l, zw, rows_tile(1024), 512, F32, 56, "in_proj",
                side=[(w_in, 0), (w_branch_2d, 0), (w_out, 0)])
            w_branch_l = w_branch_l.reshape(N_BRANCH, bw, d)
        else:
            z, _ = _matmul_cols(h, w_in_l, zw, rows_tile(1024), 512, F32, 48, "in_proj")

        pq_p = _fourier_stage1(z, tab_c, 0, nbp, seq_p, bw, col_a)
        o_a = _matmul_batched_rhs(dft_p, pq_p, 512, 1024, BF16, 48, "fourier2p", rows, 0)
        pq_s = _fourier_stage1(z, tab_c, p_rows, nbs, seq_s, bw, col_a)
        o_a = _matmul_batched_rhs(dft_s, pq_s, 512, 1024, BF16, 48, "fourier2s", rows, p_rows,
                                  into=o_a)

        lru_args = (conv_w, conv_b, wa_b, wx_b, lru_ba, lru_bx, lru_lambda)
        o_b, st = _lru(z, l, *lru_args, None, row0=0, rows=p_rows, ngroups=1, seq=seq_p,
                       nseg=nbp, seg_len=seq_p, col_x=col_bx, col_y=col_by, bw=bw)
        seg = min(128, seq_s)
        o_b = _lru(z, l, *lru_args, state_lru[:, l].astype(F32), row0=p_rows, rows=seq_s,
                   ngroups=nbs, seq=seq_s, nseg=seq_s // seg, seg_len=seg,
                   col_x=col_bx, col_y=col_by, bw=bw, into=o_b)
        s_list.append(jnp.transpose(st, (1, 0, 2)))

        att = dict(n_kv=n_kv, grp=grp, hd=hd, col_q=col_q, col_k=col_k, col_v=col_v)
        o_c, k_p, v_p = _attention(z, l, q_norm, k_norm, row0=0, nb=nbp, seq=seq_p, tq=seq_p,
                                   emit_kv=True, **att)
        o_c = _attention(z, l, q_norm, k_norm, row0=p_rows, nb=nbs, seq=seq_s,
                         tq=min(256, seq_s), rope_tabs=rope_tabs,
                         cache=(cache_kt[:, l], cache_vt[:, l]), into=o_c, **att)
        k_list.append(k_p.reshape(nbp, seq_p, n_kv, hd))
        v_list.append(v_p.reshape(nbp, seq_p, n_kv, hd))

        o_d = _pool(z, l, pool_w_b, pool_scale, rows=seq_s, groups_p=p_rows // seq_s,
                    seq_p=seq_p, seq_s=seq_s, col0=col_d, bw=bw)

        side = [(w_ffn_gate, l), (w_ffn_up, l), (w_ffn_down, l)]
        if l + 1 < depth:
            side += [(w_in, l + 1), (w_branch_2d, l + 1), (w_out, l + 1)]
        merged, cast = _gate_merge(h, w_in_l, col_g, (o_a, o_b, o_c, o_d), w_branch_l,
                                   side=side, tm=rows_tile(512))
        x = None
        for rng in x_ranges:
            x = _out_residual(rng, merged, w_out_l, g_post_mix[l], mod_l, 2, p_rows, seq_s,
                              tm_norm, into=x)
        ffn = functools.partial(_ffn, x, g_pre=g_pre_ffn[l], g_post=g_post_ffn[l],
                                mod_l=mod_l, wg=cast[0], wu=cast[1], wd=cast[2], p_rows=p_rows,
                                s_rows=seq_s, tm=rows_tile(512))
        if l + 1 < depth:
            w_in_l, w_branch_l, w_out_l = cast[3], cast[4].reshape(N_BRANCH, bw, d), cast[5]
            x_ranges = [(ffn(0, rows), 0, 0, rows)]
        else:
            y_p, y_s = ffn(0, p_rows), ffn(p_rows, s_rows)

    y_p = y_p.reshape(nbp, seq_p, d).astype(x_prompt.dtype)
    y_s = y_s.reshape(nbs, seq_s, d).astype(x_sample.dtype)
    new_k = jnp.stack(k_list, axis=1).astype(cache_k.dtype)
    new_v = jnp.stack(v_list, axis=1).astype(cache_v.dtype)
    new_s = jnp.stack(s_list, axis=1).astype(state_lru.dtype)
    return y_p, y_s, new_k, new_v, new_s
```

```python
import functools
import math

import jax
import jax.numpy as jnp
from jax import lax
from jax.experimental import pallas as pl
from jax.experimental.pallas import tpu as pltpu

F32 = jnp.float32
BF16 = jnp.bfloat16
MIB = 1024 * 1024

EPS = 1e-6
GRID_W = 64
ROPE_THETA = 10000.0
LRU_C = 8.0
FOURIER_GROUPS = 4
POOL_WINDOWS = (2, 4, 8, 16)
N_BRANCH = 4
HALO = 8
CONV_LEFT = 2
FFN_CHUNK = 512
SEG_PAD = 8


def _params(semantics, vmem_mib):
    return pltpu.CompilerParams(dimension_semantics=semantics,
                                vmem_limit_bytes=vmem_mib * MIB)


def _rms(x, gain):
    ms = jnp.mean(x * x, axis=-1, keepdims=True)
    return x * lax.rsqrt(ms + EPS) * gain


def _sigmoid(x):
    return 0.5 * jnp.tanh(0.5 * x) + 0.5


def _row_rinv(src_ref, stat_ref):
    n, d = src_ref.shape

    def body(i, carry):
        sl = pl.ds(pl.multiple_of(i * 8, 8), 8)
        x = src_ref[sl, :]
        ms = jnp.sum(x * x, axis=-1, keepdims=True) * (1.0 / d)
        stat_ref[sl, :] = jnp.broadcast_to(lax.rsqrt(ms + EPS), (8, stat_ref.shape[1]))
        return carry
    lax.fori_loop(0, n // 8, body, 0, unroll=8)


def _rinv_rows(stat_ref, sl, d):
    return jnp.tile(stat_ref[sl, :], (1, d // stat_ref.shape[1]))


def _softplus(x):
    return jnp.maximum(x, 0.0) + jnp.log1p(jnp.exp(-jnp.abs(x)))


def _group_of_tile(i, tm, p_rows, s_rows):
    r = i * tm
    return jnp.where(r < p_rows, 0, 1 + (r - p_rows) // s_rows)


def _mod_spec(k, d, tm, p_rows, s_rows, tile0=0):
    return pl.BlockSpec((None, 1, d),
                        lambda i, *_: (_group_of_tile(tile0 + i, tm, p_rows, s_rows), 0, k))


def _fill(kernel, in_specs, args, into):
    if into is None:
        return kernel, {}
    n = len(args)
    in_specs.append(pl.BlockSpec(memory_space=pl.ANY))
    args.append(into)

    def body(*refs):
        return kernel(*refs[:n], *refs[n + 1:])
    return body, {n: 0}


def _side_casts(weights, nsteps, step_of):
    in_specs, args, out_specs, out_shapes = [], [], [], []
    for w, layer in weights:
        _, rows, cols = w.shape
        br = next(b for b in range(16, rows + 1, 16) if rows % b == 0 and rows // b <= nsteps)
        last = rows // br - 1
        in_specs.append(pl.BlockSpec(
            (None, br, cols),
            functools.partial(lambda *g, layer, last: (layer, jnp.minimum(step_of(*g), last), 0),
                              layer=layer, last=last)))
        out_specs.append(pl.BlockSpec(
            (br, cols),
            functools.partial(lambda *g, last: (jnp.minimum(step_of(*g), last), 0), last=last)))
        args.append(w)
        out_shapes.append(jax.ShapeDtypeStruct((rows, cols), BF16))
    return in_specs, args, out_specs, out_shapes


def _run_side_casts(src_refs, dst_refs):
    for src, dst in zip(src_refs, dst_refs):
        dst[...] = src[...].astype(BF16)


def _mod_kernel(c_ref, w_ref, b_ref, o_ref):
    c = c_ref[...]
    s = (c * jax.nn.sigmoid(c)).astype(BF16)
    o_ref[...] = jnp.dot(s, w_ref[...].astype(BF16),
                         preferred_element_type=F32) + b_ref[...]


def _mod_all(cvec, w_mod, b_mod, tn=512):
    depth, d, n = w_mod.shape
    return pl.pallas_call(
        _mod_kernel,
        out_shape=jax.ShapeDtypeStruct((depth, 8, n), F32),
        grid=(depth, n // tn),
        in_specs=[pl.BlockSpec((8, d), lambda l, j: (0, 0)),
                  pl.BlockSpec((None, d, tn), lambda l, j: (l, 0, j)),
                  pl.BlockSpec((None, 1, tn), lambda l, j: (l, 0, j))],
        out_specs=pl.BlockSpec((None, 8, tn), lambda l, j: (l, 0, j)),
        compiler_params=_params(("arbitrary", "arbitrary"), 40),
        name="mod",
    )(cvec, w_mod, b_mod.reshape(depth, 1, n))


def _prenorm_kernel(x_ref, g_ref, sh_ref, sc_ref, o_ref, stat_ref):
    tm, d = x_ref.shape
    slab = math.gcd(16, tm)
    _row_rinv(x_ref, stat_ref)

    def body(i, carry):
        sl = pl.ds(pl.multiple_of(i * slab, slab), slab)
        h = x_ref[sl, :] * _rinv_rows(stat_ref, sl, d) * g_ref[...]
        o_ref[sl, :] = (h * (1.0 + sc_ref[...]) + sh_ref[...]).astype(o_ref.dtype)
        return carry
    lax.fori_loop(0, tm // slab, body, 0, unroll=2)


def _prenorm(rng, total_rows, gain, mod_l, k_shift, p_rows, s_rows, tm, into=None):
    src, src_row0, uni_row0, nrows = rng
    d = src.shape[1]
    st0, ut0 = src_row0 // tm, uni_row0 // tm
    in_specs = [pl.BlockSpec((tm, d), lambda i: (st0 + i, 0)),
                pl.BlockSpec((1, d), lambda i: (0, 0)),
                _mod_spec(k_shift, d, tm, p_rows, s_rows, ut0),
                _mod_spec(k_shift + 1, d, tm, p_rows, s_rows, ut0)]
    args = [src, gain.reshape(1, d), mod_l, mod_l]
    body, aliases = _fill(_prenorm_kernel, in_specs, args, into)
    return pl.pallas_call(
        body,
        out_shape=jax.ShapeDtypeStruct((total_rows, d), BF16),
        grid=(nrows // tm,),
        in_specs=in_specs,
        out_specs=pl.BlockSpec((tm, d), lambda i: (ut0 + i, 0)),
        scratch_shapes=[pltpu.VMEM((tm, 128), F32)],
        input_output_aliases=aliases,
        compiler_params=_params(("arbitrary",), 40),
        name="prenorm",
    )(*args)


def _mm_kernel(a_ref, b_ref, o_ref):
    o_ref[...] = jnp.dot(a_ref[...], b_ref[...],
                         preferred_element_type=F32).astype(o_ref.dtype)


def _mm_side_kernel(*refs, n_cast):
    a_ref, b_ref = refs[:2]
    _run_side_casts(refs[2:2 + n_cast], refs[3 + n_cast:])
    _mm_kernel(a_ref, b_ref, refs[2 + n_cast])


def _matmul_cols(a, w, n_cols, tm, tn, out_dtype, vmem_mib, name, side=()):
    m, k = a.shape
    tm, tn = min(tm, m), min(tn, n_cols)
    nn = n_cols // tn
    c_in, c_args, c_out, c_shapes = _side_casts(side, (m // tm) * nn, lambda i, j: i * nn + j)
    res = pl.pallas_call(
        functools.partial(_mm_side_kernel, n_cast=len(side)),
        out_shape=[jax.ShapeDtypeStruct((m, n_cols), out_dtype)] + c_shapes,
        grid=(m // tm, nn),
        in_specs=[pl.BlockSpec((tm, k), lambda i, j: (i, 0)),
                  pl.BlockSpec((k, tn), lambda i, j: (0, j))] + c_in,
        out_specs=[pl.BlockSpec((tm, tn), lambda i, j: (i, j))] + c_out,
        compiler_params=_params(("arbitrary", "arbitrary"), vmem_mib),
        name=name,
    )(a, w, *c_args)
    return res[0], res[1:]


def _matmul_batched_rhs(a, b, tm, tn, out_dtype, vmem_mib, name, total_rows, row0, into=None):
    m, k = a.shape
    nb, _, n = b.shape
    tm, tn = min(tm, m), min(tn, n)
    t0, nt = row0 // tm, m // tm
    in_specs = [pl.BlockSpec((tm, k), lambda bi, i, j: (i, 0)),
                pl.BlockSpec((None, k, tn), lambda bi, i, j: (bi, 0, j))]
    args = [a, b]
    body, aliases = _fill(_mm_kernel, in_specs, args, into)
    return pl.pallas_call(
        body,
        out_shape=jax.ShapeDtypeStruct((total_rows, n), out_dtype),
        grid=(nb, nt, n // tn),
        in_specs=in_specs,
        out_specs=pl.BlockSpec((tm, tn), lambda bi, i, j: (t0 + bi * nt + i, j)),
        input_output_aliases=aliases,
        compiler_params=_params(("arbitrary", "arbitrary", "arbitrary"), vmem_mib),
        name=name,
    )(*args)


FOLD_MIN_SEQ = 1024
FLIP_BLOCK = 128
FOUR1_ROWS = 4096


def _fold_cols(seq):
    return -(-(seq // 2 + 1) // 128) * 128


def _dft_matrix(n, kcols, sign, fold, split=64):
    col = jnp.arange(2 * kcols, dtype=jnp.int32)
    t = col % kcols
    phase = (col // kcols).astype(F32) * (sign * 0.5 * math.pi)
    w = jnp.full((2 * kcols,), 1.0 / math.sqrt(n), F32)
    if fold:
        w = w * jnp.where((t == 0) | (t == n // 2), 0.5, jnp.where(t < n // 2, 1.0, 0.0))

    def angle(rows, period):
        r = jnp.arange(rows, dtype=jnp.int32)
        return ((r[:, None] * t[None, :]) % period).astype(F32) * (2.0 * math.pi / period)

    if n <= split or n % split:
        return jnp.cos(angle(n, n) + phase) * w
    ang_a = angle(n // split, n // split)
    ang_b = angle(split, n) + phase
    tab = (jnp.cos(ang_a)[:, None, :] * jnp.cos(ang_b)[None]
           - jnp.sin(ang_a)[:, None, :] * jnp.sin(ang_b)[None]) * w
    return tab.reshape(n, 2 * kcols)


def _four1_kernel(u_ref, t_ref, o_ref, *, seq, gw, kfold):
    u = u_ref[...].astype(BF16)
    pq = jnp.dot(u, t_ref[...], preferred_element_type=F32)
    if not kfold:
        for b in range(o_ref.shape[0]):
            o_ref[b, 0:seq, :] = pq[b * seq:(b + 1) * seq, :gw].astype(o_ref.dtype)
            o_ref[b, seq:2 * seq, :] = pq[b * seq:(b + 1) * seq, gw:].astype(o_ref.dtype)
        return
    pqb = pq.astype(BF16)
    nblk = seq // FLIP_BLOCK
    r = lax.broadcasted_iota(jnp.int32, (FLIP_BLOCK, FLIP_BLOCK), 0)
    c = lax.broadcasted_iota(jnp.int32, (FLIP_BLOCK, FLIP_BLOCK), 1)
    anti = jnp.where(r + c == FLIP_BLOCK - 1, 1.0, 0.0).astype(BF16)
    flipped = jnp.concatenate(
        [jnp.dot(anti, pqb[(nblk - 1 - i) * FLIP_BLOCK:(nblk - i) * FLIP_BLOCK],
                 preferred_element_type=F32) for i in range(kfold // FLIP_BLOCK)], axis=0)
    row = lax.broadcasted_iota(jnp.int32, flipped.shape, 0)
    head = pqb[0:kfold].astype(F32)
    rev = jnp.where(row == 0, head[0:1], pltpu.roll(flipped, 1, axis=0))
    o_ref[0, 0:kfold, :] = (head[:, :gw] + rev[:, :gw]).astype(o_ref.dtype)
    o_ref[0, kfold:2 * kfold, :] = (head[:, gw:] - rev[:, gw:]).astype(o_ref.dtype)


def _fourier_stage1(z, tab, row0, nb, seq, bw, col0):
    gw = bw // FOURIER_GROUPS
    kfold = _fold_cols(seq) if seq >= FOLD_MIN_SEQ else 0
    krows = 2 * (kfold or seq)
    per = 1 if kfold else math.gcd(math.gcd(nb, max(1, FOUR1_ROWS // seq)), row0 // seq)
    rb0, cb0 = row0 // (per * seq), col0 // gw
    return pl.pallas_call(
        functools.partial(_four1_kernel, seq=seq, gw=gw, kfold=kfold),
        out_shape=jax.ShapeDtypeStruct((nb, krows, bw), BF16),
        grid=(nb // per, FOURIER_GROUPS),
        in_specs=[pl.BlockSpec((per * seq, gw), lambda b, g: (rb0 + b, cb0 + g)),
                  pl.BlockSpec((gw, 2 * gw), lambda b, g: (0, 0))],
        out_specs=pl.BlockSpec((per, krows, gw), lambda b, g: (b, 0, g)),
        compiler_params=_params(("arbitrary", "arbitrary"), 48),
        name="fourier1",
    )(z, tab)


def _fourier_matrix(seq):
    if seq >= FOLD_MIN_SEQ:
        return _dft_matrix(seq, _fold_cols(seq), 1.0, True).astype(BF16)
    return _dft_matrix(seq, seq, 1.0, False).astype(BF16)


def _lru_kernel(*refs, seq, rows, nseg, seg_len, chained, chunk):
    if chained:
        (zx_ref, zy_ref, cw_ref, cb_ref, wa_ref, wx_ref, ba_ref, bx_ref, lam_ref,
         h0_ref, o_ref, pad_ref, a0_ref, a1_ref, b0_ref, b1_ref) = refs
        st_ref = None
    else:
        (zx_ref, zy_ref, cw_ref, cb_ref, wa_ref, wx_ref, ba_ref, bx_ref, lam_ref,
         o_ref, st_ref, pad_ref, a0_ref, a1_ref, b0_ref, b1_ref) = refs
        h0_ref = None
    a_refs, b_refs = (a0_ref, a1_ref), (b0_ref, b1_ref)
    cw_ch = zx_ref.shape[1]
    ext = chunk + 2 * HALO
    n_taps = cw_ref.shape[0]
    pitch = seg_len + SEG_PAD
    segs = chunk // seg_len

    pad_ref[0:HALO, :] = jnp.zeros((HALO, cw_ch), F32)
    pad_ref[HALO + rows:, :] = jnp.zeros((HALO, cw_ch), F32)
    pad_ref[HALO:HALO + rows, :] = zx_ref[...]

    neg_sp = [-LRU_C * _softplus(-lam_ref[d:d + 1, :]) for d in range(2)]

    def seg_rows(ci, j):
        return pl.ds(pl.multiple_of(ci * (segs * pitch), 8) + j * pitch, seg_len)

    def gates(ci, carry):
        r0 = pl.multiple_of(ci * chunk, chunk)
        xe = pad_ref[pl.ds(r0, ext), :]
        t = (r0 + lax.broadcasted_iota(jnp.int32, (chunk, cw_ch), 0)) & (seq - 1)
        xc = jnp.zeros((chunk, cw_ch), F32) + cb_ref[...]
        for j in range(n_taps):
            d = j - CONV_LEFT
            if d == 0:
                xs = xe[HALO:HALO + chunk]
            else:
                xs = pltpu.roll(xe, (-d) % ext, axis=0)[HALO:HALO + chunk]
                xs = jnp.where((t >= -d) if d < 0 else (t < seq - d), xs, 0.0)
            xc = xc + xs * cw_ref[j:j + 1, :]
        xcb = xc.astype(BF16)
        for d in range(2):
            r = _sigmoid(jnp.dot(xcb, wa_ref[d], preferred_element_type=F32)
                         + ba_ref[d:d + 1, :])
            i = _sigmoid(jnp.dot(xcb, wx_ref[d], preferred_element_type=F32)
                         + bx_ref[d:d + 1, :])
            a = jnp.exp(r * neg_sp[d])
            bterm = jnp.sqrt(1.0 - a * a) * (i * xc)
            for j in range(segs):
                a_refs[d][seg_rows(ci, j), :] = a[j * seg_len:(j + 1) * seg_len]
                b_refs[d][seg_rows(ci, j), :] = bterm[j * seg_len:(j + 1) * seg_len]
        return carry

    lax.fori_loop(0, rows // chunk, gates, 0)

    def scan(direction):
        def body(s, carry):
            h, p = carry
            pos = s if direction == 0 else seg_len - 1 - s
            idx = pl.ds(pos, nseg, stride=pitch)
            a = a_refs[direction][idx, :]
            h = a * h + b_refs[direction][idx, :]
            b_refs[direction][idx, :] = h
            if chained:
                p = a * p
                a_refs[direction][idx, :] = p
            return h, p
        init = (jnp.zeros((nseg, cw_ch), F32), jnp.ones((nseg, cw_ch), F32))
        lax.fori_loop(0, seg_len, body, init, unroll=8)

    scan(0)
    scan(1)

    if chained:
        carry = h0_ref[0:1, :]
        for q in range(nseg):
            sl = slice(q * pitch, q * pitch + seg_len)
            hq = b_refs[0][sl, :] + a_refs[0][sl, :] * carry
            b_refs[0][sl, :] = hq
            carry = hq[seg_len - 1:seg_len, :]
        carry = h0_ref[1:2, :]
        for q in reversed(range(nseg)):
            sl = slice(q * pitch, q * pitch + seg_len)
            hq = b_refs[1][sl, :] + a_refs[1][sl, :] * carry
            b_refs[1][sl, :] = hq
            carry = hq[0:1, :]
    else:
        st_ref[0] = b_refs[0][pl.ds(seq - 1, nseg, stride=pitch), :]
        st_ref[1] = b_refs[1][pl.ds(0, nseg, stride=pitch), :]

    def emit(ci, carry):
        r0 = pl.multiple_of(ci * chunk, chunk)
        for j in range(segs):
            sl = pl.ds(r0 + j * seg_len, seg_len)
            hsum = b_refs[0][seg_rows(ci, j), :] + b_refs[1][seg_rows(ci, j), :]
            o_ref[sl, :] = (hsum * jax.nn.gelu(zy_ref[sl, :])).astype(o_ref.dtype)
        return carry

    lax.fori_loop(0, rows // chunk, emit, 0)


def _lru(z, layer, conv_w, conv_b, wa, wx, ba, bx, lam, h0, *, row0, rows, ngroups,
         seq, nseg, seg_len, col_x, col_y, bw, into=None):
    cb = wa.shape[-1]
    ncb = bw // cb
    chained = h0 is not None
    chunk = max(min(256, rows), seg_len)
    assert chunk % seg_len == 0 and rows % chunk == 0 and seg_len % 8 == 0
    assert chained or seg_len == seq
    rb0 = row0 // rows
    cx0, cy0 = col_x // cb, col_y // cb
    in_specs = [
        pl.BlockSpec((rows, cb), lambda g, c: (rb0 + g, cx0 + c)),
        pl.BlockSpec((rows, cb), lambda g, c: (rb0 + g, cy0 + c)),
        pl.BlockSpec((None, conv_w.shape[1], cb), lambda g, c: (layer, 0, c)),
        pl.BlockSpec((None, 1, cb), lambda g, c: (layer, 0, c)),
        pl.BlockSpec((None, 2, None, cb, cb), lambda g, c: (layer, 0, c, 0, 0)),
        pl.BlockSpec((None, 2, None, cb, cb), lambda g, c: (layer, 0, c, 0, 0)),
        pl.BlockSpec((None, 2, cb), lambda g, c: (layer, 0, c)),
        pl.BlockSpec((None, 2, cb), lambda g, c: (layer, 0, c)),
        pl.BlockSpec((None, 2, cb), lambda g, c: (layer, 0, c)),
    ]
    args = [z, z, conv_w, conv_b.reshape(conv_b.shape[0], 1, bw), wa, wx, ba, bx, lam]
    out_shape = [jax.ShapeDtypeStruct((z.shape[0], bw), BF16)]
    out_specs = [pl.BlockSpec((rows, cb), lambda g, c: (rb0 + g, c))]
    if chained:
        in_specs.append(pl.BlockSpec((None, 2, cb), lambda g, c: (g, 0, c)))
        args.append(h0)
    else:
        assert ngroups == 1
        out_shape.append(jax.ShapeDtypeStruct((2, rows // seq, bw), F32))
        out_specs.append(pl.BlockSpec((2, rows // seq, cb), lambda g, c: (0, 0, c)))
    kern = functools.partial(_lru_kernel, seq=seq, rows=rows, nseg=nseg, seg_len=seg_len,
                             chained=chained, chunk=chunk)
    body, aliases = _fill(kern, in_specs, args, into)
    res = pl.pallas_call(
        body,
        out_shape=out_shape,
        grid=(ngroups, ncb),
        in_specs=in_specs,
        out_specs=out_specs,
        scratch_shapes=[pltpu.VMEM((rows + 2 * HALO, cb), F32)]
                       + [pltpu.VMEM((nseg * (seg_len + SEG_PAD), cb), F32)] * 4,
        input_output_aliases=aliases,
        compiler_params=_params(("arbitrary", "arbitrary"), 48),
        name="rglru",
    )(*args)
    return res if not chained else res[0]


def _rope_tables(seq, hd):
    rows = seq // GRID_W
    row = jnp.repeat(jnp.arange(rows), GRID_W).astype(F32)
    col = jnp.tile(jnp.arange(GRID_W), rows).astype(F32)
    ax = hd // 2
    inv = ROPE_THETA ** (-jnp.arange(0, ax, 2, dtype=F32) / ax)
    ang_r = row[:, None] * inv[None, :]
    ang_c = col[:, None] * inv[None, :]
    cos = jnp.concatenate([jnp.cos(ang_r)] * 2 + [jnp.cos(ang_c)] * 2, axis=-1)
    sin = jnp.concatenate([-jnp.sin(ang_r), jnp.sin(ang_r),
                           -jnp.sin(ang_c), jnp.sin(ang_c)], axis=-1)
    return cos, sin


def _rotate(x, cos, sin):
    hd = x.shape[1]
    q = hd // 4
    lane = lax.broadcasted_iota(jnp.int32, x.shape, 1)
    ahead = pltpu.roll(x, hd - q, axis=1)
    behind = pltpu.roll(x, q, axis=1)
    partner = jnp.where((lane & q) == 0, ahead, behind)
    return x * cos + partner * sin


def _attn_kernel(*refs, seq, cache_len, tq, grp, hd, rope, emit_kv):
    refs = list(refs)
    zq_ref, zk_ref, zv_ref, qn_ref, kn_ref = refs[:5]
    pos = 5
    if rope:
        cq_ref, sq_ref, ck_ref, sk_ref = refs[pos:pos + 4]
        pos += 4
    if cache_len:
        kc_ref, vc_ref = refs[pos:pos + 2]
        pos += 2
    o_ref = refs[pos]
    pos += 1
    if emit_kv:
        ko_ref, vo_ref = refs[pos:pos + 2]
        pos += 2
    ks_ref, vt_ref = refs[pos:pos + 2]

    @pl.when(pl.program_id(2) == 0)
    def _():
        k = _rms(zk_ref[...], kn_ref[...])
        v = zv_ref[...]
        if emit_kv:
            ko_ref[...] = k
            vo_ref[...] = v
        if rope:
            k = _rotate(k, ck_ref[...], sk_ref[...])
        ks_ref[0:seq, :] = k.astype(BF16)
        vt_ref[:, 0:seq] = v.T.astype(BF16)
        if cache_len:
            ks_ref[seq:seq + cache_len, :] = kc_ref[...].astype(BF16)
            vt_ref[:, seq:seq + cache_len] = vc_ref[...].T.astype(BF16)

    scale = hd ** -0.5 * math.log2(math.e)
    qs = []
    for hh in range(grp):
        q = _rms(zq_ref[:, hh * hd:(hh + 1) * hd], qn_ref[...])
        if rope:
            q = _rotate(q, cq_ref[...], sq_ref[...])
        qs.append((q * scale).astype(BF16))
    q = jnp.concatenate(qs, axis=0)
    s = lax.dot_general(ks_ref[...], q, (((1,), (1,)), ((), ())),
                        preferred_element_type=F32)
    p = jnp.exp2(s - jnp.max(s, axis=0, keepdims=True))
    acc = jnp.dot(vt_ref[...], p.astype(BF16), preferred_element_type=F32)
    out = (acc / jnp.sum(p, axis=0, keepdims=True)).T
    for hh in range(grp):
        o_ref[:, hh * hd:(hh + 1) * hd] = out[hh * tq:(hh + 1) * tq].astype(o_ref.dtype)


def _attention(z, layer, q_norm, k_norm, *, row0, nb, seq, n_kv, grp, hd, col_q, col_k, col_v,
               tq, rope_tabs=None, cache=None, emit_kv=False, into=None):
    qw = grp * hd
    cache_len = 0 if cache is None else cache[0].shape[2]
    nq = seq // tq
    rbq0, rbk0 = row0 // tq, row0 // seq
    cq0, ck0, cv0 = col_q // qw, col_k // hd, col_v // hd
    in_specs = [
        pl.BlockSpec((tq, qw), lambda b, h, i: (rbq0 + b * nq + i, cq0 + h)),
        pl.BlockSpec((seq, hd), lambda b, h, i: (rbk0 + b, ck0 + h)),
        pl.BlockSpec((seq, hd), lambda b, h, i: (rbk0 + b, cv0 + h)),
        pl.BlockSpec((None, 1, hd), lambda b, h, i: (layer, 0, 0)),
        pl.BlockSpec((None, 1, hd), lambda b, h, i: (layer, 0, 0)),
    ]
    args = [z, z, z, q_norm.reshape(-1, 1, hd), k_norm.reshape(-1, 1, hd)]
    if rope_tabs is not None:
        cos, sin = rope_tabs
        in_specs += [pl.BlockSpec((tq, hd), lambda b, h, i: (i, 0)),
                     pl.BlockSpec((tq, hd), lambda b, h, i: (i, 0)),
                     pl.BlockSpec((seq, hd), lambda b, h, i: (0, 0)),
                     pl.BlockSpec((seq, hd), lambda b, h, i: (0, 0))]
        args += [cos, sin, cos, sin]
    if cache is not None:
        in_specs += [pl.BlockSpec((None, None, cache_len, hd), lambda b, h, i: (b, h, 0, 0)),
                     pl.BlockSpec((None, None, cache_len, hd), lambda b, h, i: (b, h, 0, 0))]
        args += list(cache)
    out_shape = [jax.ShapeDtypeStruct((z.shape[0], n_kv * qw), BF16)]
    out_specs = [pl.BlockSpec((tq, qw), lambda b, h, i: (rbq0 + b * nq + i, h))]
    if emit_kv:
        out_shape += [jax.ShapeDtypeStruct((nb * seq, n_kv * hd), F32)] * 2
        out_specs += [pl.BlockSpec((seq, hd), lambda b, h, i: (b, h))] * 2
    kern = functools.partial(_attn_kernel, seq=seq, cache_len=cache_len, tq=tq, grp=grp,
                             hd=hd, rope=rope_tabs is not None, emit_kv=emit_kv)
    body, aliases = _fill(kern, in_specs, args, into)
    res = pl.pallas_call(
        body,
        out_shape=out_shape,
        grid=(nb, n_kv, nq),
        in_specs=in_specs,
        out_specs=out_specs,
        scratch_shapes=[pltpu.VMEM((seq + cache_len, hd), BF16),
                        pltpu.VMEM((hd, seq + cache_len), BF16)],
        input_output_aliases=aliases,
        compiler_params=_params(("arbitrary", "arbitrary", "arbitrary"), 48),
        name="attention",
    )(*args)
    return res if emit_kv else res[0]


def _pool_kernel(u_ref, w_ref, sc_ref, o_ref, pad_ref, *, rows, chunk, seq_p, seq_s, groups_p):
    gw = u_ref.shape[1]
    ext = chunk + 2 * HALO
    seq = jnp.where(pl.program_id(0) < groups_p, seq_p, seq_s)
    pad_ref[0:HALO, :] = jnp.zeros((HALO, gw), F32)
    pad_ref[HALO + rows:, :] = jnp.zeros((HALO, gw), F32)
    pad_ref[HALO:HALO + rows, :] = u_ref[...]

    for gi, win in enumerate(POOL_WINDOWS):
        hw = win // 2

        @pl.when(pl.program_id(1) == gi)
        def _(hw=hw):
            def body(ci, carry):
                r0 = pl.multiple_of(ci * chunk, chunk)
                xe = pad_ref[pl.ds(r0, ext), :]
                t = (r0 + lax.broadcasted_iota(jnp.int32, (chunk, gw), 0)) & (seq - 1)
                x0 = xe[HALO:HALO + chunk]
                acc = x0
                for d in range(-hw, hw):
                    if d == 0:
                        continue
                    xs = pltpu.roll(xe, (-d) % ext, axis=0)[HALO:HALO + chunk]
                    acc = acc + jnp.where((t >= -d) if d < 0 else (t < seq - d), xs, 0.0)
                cnt = (jnp.minimum(t + hw, seq) - jnp.maximum(t - hw, 0)).astype(F32)
                pooled = acc / cnt - x0
                y = jnp.dot(pooled.astype(BF16), w_ref[...], preferred_element_type=F32)
                o_ref[pl.ds(r0, chunk), :] = (y * sc_ref[...]).astype(o_ref.dtype)
                return carry

            lax.fori_loop(0, rows // chunk, body, 0)


def _pool(z, layer, pool_w, pool_scale, *, rows, groups_p, seq_p, seq_s, col0, bw):
    r = z.shape[0]
    g = pool_w.shape[1]
    gw = bw // g
    cb0 = col0 // gw
    chunk = min(256, rows)
    return pl.pallas_call(
        functools.partial(_pool_kernel, rows=rows, chunk=chunk, seq_p=seq_p, seq_s=seq_s,
                          groups_p=groups_p),
        out_shape=jax.ShapeDtypeStruct((r, bw), BF16),
        grid=(r // rows, g),
        in_specs=[pl.BlockSpec((rows, gw), lambda i, j: (i, cb0 + j)),
                  pl.BlockSpec((None, None, gw, gw), lambda i, j: (layer, j, 0, 0)),
                  pl.BlockSpec((None, 1, gw), lambda i, j: (layer, 0, j))],
        out_specs=pl.BlockSpec((rows, gw), lambda i, j: (i, j)),
        scratch_shapes=[pltpu.VMEM((rows + 2 * HALO, gw), F32)],
        compiler_params=_params(("arbitrary", "arbitrary"), 40),
        name="pool",
    )(z, pool_w, pool_scale.reshape(pool_scale.shape[0], 1, bw))


def _merge_kernel(*refs, n_cast):
    h_ref, g0, g1, g2, g3, o0, o1, o2, o3, wb_ref = refs[:10]
    out_ref = refs[10 + n_cast]
    _run_side_casts(refs[10:10 + n_cast], refs[11 + n_cast:])
    h = h_ref[...]
    acc = None
    for n, (g_ref, o_ref) in enumerate(((g0, o0), (g1, o1), (g2, o2), (g3, o3))):
        gate = _sigmoid(jnp.dot(h, g_ref[...], preferred_element_type=F32))
        y = jnp.dot(o_ref[...], wb_ref[n], preferred_element_type=F32)
        acc = gate * y if acc is None else acc + gate * y
    out_ref[...] = acc.astype(out_ref.dtype)


def _gate_merge(h, w_in, col_g, branches, w_branch, side, tm=512, tn=256):
    m, d = h.shape
    bw = w_branch.shape[1]
    dm = w_branch.shape[2]
    tm, tn = min(tm, m), min(tn, dm)
    nm = m // tm
    gspecs = [pl.BlockSpec((d, tn),
                           functools.partial(lambda j, i, n: (0, (col_g + n * dm) // tn + j), n=n))
              for n in range(N_BRANCH)]
    ospecs = [pl.BlockSpec((tm, bw), lambda j, i: (i, 0)) for _ in range(N_BRANCH)]
    c_in, c_args, c_out, c_shapes = _side_casts(side, (dm // tn) * nm, lambda j, i: j * nm + i)
    res = pl.pallas_call(
        functools.partial(_merge_kernel, n_cast=len(side)),
        out_shape=[jax.ShapeDtypeStruct((m, dm), BF16)] + c_shapes,
        grid=(dm // tn, nm),
        in_specs=[pl.BlockSpec((tm, d), lambda j, i: (i, 0))] + gspecs + ospecs
                 + [pl.BlockSpec((N_BRANCH, bw, tn), lambda j, i: (0, 0, j))] + c_in,
        out_specs=[pl.BlockSpec((tm, tn), lambda j, i: (i, j))] + c_out,
        compiler_params=_params(("arbitrary", "arbitrary"), 58),
        name="gate_merge",
    )(h, w_in, w_in, w_in, w_in, *branches, w_branch, *c_args)
    return res[0], res[1:]


def _out_residual_kernel(m_ref, w_ref, x_ref, g_ref, gate_ref, o_ref, stat_ref):
    tm, d = x_ref.shape
    slab = math.gcd(16, tm)
    o_ref[...] = jnp.dot(m_ref[...], w_ref[...], preferred_element_type=F32)
    _row_rinv(o_ref, stat_ref)
    scale = g_ref[...] * gate_ref[...]

    def body(i, carry):
        sl = pl.ds(pl.multiple_of(i * slab, slab), slab)
        o_ref[sl, :] = x_ref[sl, :] + o_ref[sl, :] * _rinv_rows(stat_ref, sl, d) * scale
        return carry
    lax.fori_loop(0, tm // slab, body, 0, unroll=2)


def _out_residual(rng, merged, w_out, gain, mod_l, k_gate, p_rows, s_rows, tm, into=None):
    src, src_row0, uni_row0, nrows = rng
    r, d = merged.shape
    st0, ut0 = src_row0 // tm, uni_row0 // tm
    in_specs = [pl.BlockSpec((tm, d), lambda i: (ut0 + i, 0)),
                pl.BlockSpec((d, d), lambda i: (0, 0), pipeline_mode=pl.Buffered(1)),
                pl.BlockSpec((tm, d), lambda i: (st0 + i, 0)),
                pl.BlockSpec((1, d), lambda i: (0, 0)),
                _mod_spec(k_gate, d, tm, p_rows, s_rows, ut0)]
    args = [merged, w_out, src, gain.reshape(1, d), mod_l]
    body, aliases = _fill(_out_residual_kernel, in_specs, args, into)
    return pl.pallas_call(
        body,
        out_shape=jax.ShapeDtypeStruct((r, d), F32),
        grid=(nrows // tm,),
        in_specs=in_specs,
        out_specs=pl.BlockSpec((tm, d), lambda i: (ut0 + i, 0)),
        scratch_shapes=[pltpu.VMEM((tm, 128), F32)],
        input_output_aliases=aliases,
        compiler_params=_params(("arbitrary",), 60),
        name="out_residual",
    )(*args)


def _ffn_kernel(x_ref, gpre_ref, sh_ref, sc_ref, gate_ref, gpost_ref, wg_ref, wu_ref, wd_ref,
                o_ref, h_ref, stat_ref, cst_ref, *, slab, overlap):
    k = pl.program_id(1)
    tm, d = x_ref.shape
    nslab = tm // slab

    def rows(i):
        return pl.ds(pl.multiple_of(i * slab, slab), slab)

    @pl.when(k == 0)
    def _():
        _row_rinv(x_ref, stat_ref)
        cst_ref[0] = jnp.broadcast_to(gpre_ref[...] * (1.0 + sc_ref[...]), (slab, d))
        cst_ref[1] = jnp.broadcast_to(sh_ref[...], (slab, d))

        def body(i, carry):
            h = x_ref[rows(i), :] * _rinv_rows(stat_ref, rows(i), d) * cst_ref[0] + cst_ref[1]
            h_ref[rows(i), :] = h.astype(BF16)
            o_ref[rows(i), :] = jnp.zeros((slab, d), F32)
            return carry
        lax.fori_loop(0, nslab, body, 0, unroll=2)

    last = pl.num_programs(1) - 1

    def chunk(lo):
        h = h_ref[...]
        a = jnp.dot(h, wg_ref[:, lo:], preferred_element_type=F32)
        b = jnp.dot(h, wu_ref[:, lo:], preferred_element_type=F32)
        act = (a * _sigmoid(a) * b).astype(BF16)
        o_ref[...] += jnp.dot(act, wd_ref[lo:, :], preferred_element_type=F32)

    if overlap:
        pl.when(k < last)(lambda: chunk(0))
        pl.when(k == last)(lambda: chunk(overlap))
    else:
        chunk(0)

    @pl.when(k == last)
    def _():
        _row_rinv(o_ref, stat_ref)
        cst_ref[0] = jnp.broadcast_to(gpost_ref[...] * gate_ref[...], (slab, d))

        def body(i, carry):
            y = o_ref[rows(i), :] * _rinv_rows(stat_ref, rows(i), d) * cst_ref[0]
            o_ref[rows(i), :] = x_ref[rows(i), :] + y
            return carry
        lax.fori_loop(0, nslab, body, 0, unroll=2)


def _ffn(x, row0, nrows, g_pre, g_post, mod_l, wg, wu, wd, p_rows, s_rows, tm=512):
    d = x.shape[1]
    dff = wg.shape[1]
    tm, tc = min(tm, nrows), min(FFN_CHUNK, dff)
    nk = pl.cdiv(dff, tc)
    overlap = nk * tc - dff

    def chunk_start(k):
        return pl.multiple_of(jnp.minimum(k * tc, dff - tc), math.gcd(tc, dff))

    t0 = row0 // tm
    return pl.pallas_call(
        functools.partial(_ffn_kernel, slab=math.gcd(16, tm), overlap=overlap),
        out_shape=jax.ShapeDtypeStruct((nrows, d), F32),
        grid=(nrows // tm, nk),
        in_specs=[pl.BlockSpec((tm, d), lambda i, k: (t0 + i, 0), pipeline_mode=pl.Buffered(1)),
                  pl.BlockSpec((1, d), lambda i, k: (0, 0)),
                  _mod_spec(3, d, tm, p_rows, s_rows, t0),
                  _mod_spec(4, d, tm, p_rows, s_rows, t0),
                  _mod_spec(5, d, tm, p_rows, s_rows, t0),
                  pl.BlockSpec((1, d), lambda i, k: (0, 0)),
                  pl.BlockSpec((pl.Element(d), pl.Element(tc)),
                               lambda i, k: (0, chunk_start(k))),
                  pl.BlockSpec((pl.Element(d), pl.Element(tc)),
                               lambda i, k: (0, chunk_start(k))),
                  pl.BlockSpec((pl.Element(tc), pl.Element(d)),
                               lambda i, k: (chunk_start(k), 0))],
        out_specs=pl.BlockSpec((tm, d), lambda i, k: (i, 0)),
        scratch_shapes=[pltpu.VMEM((tm, d), BF16), pltpu.VMEM((tm, 128), F32),
                        pltpu.VMEM((2, math.gcd(16, tm), d), F32)],
        compiler_params=_params(("arbitrary", "arbitrary"), 60),
        name="ffn",
    )(x, g_pre.reshape(1, d), mod_l, mod_l, mod_l, g_post.reshape(1, d), wg, wu, wd)


def kernel(x_prompt, x_sample, c, cache_k, cache_v, state_lru, c_ctx, w_mod, b_mod, g_pre_mix, g_post_mix, g_pre_ffn, g_post_ffn, w_in, conv_w, conv_b, lru_wa, lru_ba, lru_wx, lru_bx, lru_lambda, q_norm, k_norm, pool_w, pool_scale, w_branch, w_out, w_ffn_gate, w_ffn_up, w_ffn_down):
    nbp, seq_p, d = x_prompt.shape
    nbs, seq_s, _ = x_sample.shape
    depth = w_in.shape[0]
    bw = w_branch.shape[2]
    cache_len, n_kv, hd = cache_k.shape[2:]
    grp = bw // hd // n_kv
    kvw = n_kv * hd
    p_rows, s_rows = nbp * seq_p, nbs * seq_s
    rows = p_rows + s_rows
    assert seq_p & (seq_p - 1) == 0 and seq_s & (seq_s - 1) == 0
    assert p_rows % seq_s == 0, "row groups of seq_s rows must tile the prompt rows"
    col_a, col_bx, col_by, col_q = 0, bw, 2 * bw, 3 * bw
    col_k = col_q + bw
    col_v = col_k + kvw
    col_d = col_v + kvw
    col_g = col_d + bw
    zw = col_g

    def rows_tile(pref):
        return math.gcd(pref, math.gcd(p_rows, seq_s))

    w_mix_l = w_in[0, :, :zw].astype(BF16)
    w_in_l = w_branch_l = w_out_l = None
    w_branch_2d = w_branch.reshape(depth, N_BRANCH * bw, d)
    wa_b = lru_wa.astype(BF16)
    wx_b = lru_wx.astype(BF16)
    pool_w_b = pool_w.astype(BF16)

    gw = bw // FOURIER_GROUPS
    tab_c = _dft_matrix(gw, gw, -1.0, False).astype(BF16)
    dft_p = _fourier_matrix(seq_p)
    dft_s = _fourier_matrix(seq_s)
    rope_tabs = _rope_tables(seq_s, hd)

    cvec = jnp.concatenate([c_ctx[None, :], c, jnp.zeros((8 - 1 - nbs, d), c.dtype)], axis=0)
    mod = _mod_all(cvec.astype(F32), w_mod, b_mod)

    cache_kt = jnp.transpose(cache_k, (0, 1, 3, 2, 4)).astype(F32)
    cache_vt = jnp.transpose(cache_v, (0, 1, 3, 2, 4)).astype(F32)

    x_ranges = [(x_prompt.reshape(p_rows, d).astype(F32), 0, 0, p_rows),
                (x_sample.reshape(s_rows, d).astype(F32), 0, p_rows, s_rows)]
    tm_norm = rows_tile(256)
    k_list, v_list, s_list = [], [], []
    y_p = y_s = None
    for l in range(depth):
        mod_l = mod[l].reshape(8, 1, 6 * d)
        h = None
        for rng in x_ranges:
            h = _prenorm(rng, rows, g_pre_mix[l], mod_l, 0, p_rows, seq_s, tm_norm, into=h)
        if l == 0:
            z, (w_in_l, w_branch_l, w_out_l) = _matmul_cols(
                h, w_mix_l, zw, rows_tile(1024), 512, F32, 56, "in_proj",
                side=[(w_in, 0), (w_branch_2d, 0), (w_out, 0)])
            w_branch_l = w_branch_l.reshape(N_BRANCH, bw, d)
        else:
            z, _ = _matmul_cols(h, w_in_l, zw, rows_tile(1024), 512, F32, 48, "in_proj")

        pq_p = _fourier_stage1(z, tab_c, 0, nbp, seq_p, bw, col_a)
        o_a = _matmul_batched_rhs(dft_p, pq_p, 512, 1024, BF16, 48, "fourier2p", rows, 0)
        pq_s = _fourier_stage1(z, tab_c, p_rows, nbs, seq_s, bw, col_a)
        o_a = _matmul_batched_rhs(dft_s, pq_s, 512, 1024, BF16, 48, "fourier2s", rows, p_rows,
                                  into=o_a)

        lru_args = (conv_w, conv_b, wa_b, wx_b, lru_ba, lru_bx, lru_lambda)
        o_b, st = _lru(z, l, *lru_args, None, row0=0, rows=p_rows, ngroups=1, seq=seq_p,
                       nseg=nbp, seg_len=seq_p, col_x=col_bx, col_y=col_by, bw=bw)
        seg = min(128, seq_s)
        o_b = _lru(z, l, *lru_args, state_lru[:, l].astype(F32), row0=p_rows, rows=seq_s,
                   ngroups=nbs, seq=seq_s, nseg=seq_s // seg, seg_len=seg,
                   col_x=col_bx, col_y=col_by, bw=bw, into=o_b)
        s_list.append(jnp.transpose(st, (1, 0, 2)))

        att = dict(n_kv=n_kv, grp=grp, hd=hd, col_q=col_q, col_k=col_k, col_v=col_v)
        o_c, k_p, v_p = _attention(z, l, q_norm, k_norm, row0=0, nb=nbp, seq=seq_p, tq=seq_p,
                                   emit_kv=True, **att)
        o_c = _attention(z, l, q_norm, k_norm, row0=p_rows, nb=nbs, seq=seq_s,
                         tq=min(256, seq_s), rope_tabs=rope_tabs,
                         cache=(cache_kt[:, l], cache_vt[:, l]), into=o_c, **att)
        k_list.append(k_p.reshape(nbp, seq_p, n_kv, hd))
        v_list.append(v_p.reshape(nbp, seq_p, n_kv, hd))

        o_d = _pool(z, l, pool_w_b, pool_scale, rows=seq_s, groups_p=p_rows // seq_s,
                    seq_p=seq_p, seq_s=seq_s, col0=col_d, bw=bw)

        side = [(w_ffn_gate, l), (w_ffn_up, l), (w_ffn_down, l)]
        if l + 1 < depth:
            side += [(w_in, l + 1), (w_branch_2d, l + 1), (w_out, l + 1)]
        merged, cast = _gate_merge(h, w_in_l, col_g, (o_a, o_b, o_c, o_d), w_branch_l,
                                   side=side, tm=rows_tile(512))
        x = None
        for rng in x_ranges:
            x = _out_residual(rng, merged, w_out_l, g_post_mix[l], mod_l, 2, p_rows, seq_s,
                              tm_norm, into=x)
        ffn = functools.partial(_ffn, x, g_pre=g_pre_ffn[l], g_post=g_post_ffn[l],
                                mod_l=mod_l, wg=cast[0], wu=cast[1], wd=cast[2], p_rows=p_rows,
                                s_rows=seq_s, tm=rows_tile(512))
        if l + 1 < depth:
            w_in_l, w_branch_l, w_out_l = cast[3], cast[4].reshape(N_BRANCH, bw, d), cast[5]
            x_ranges = [(ffn(0, rows), 0, 0, rows)]
        else:
            y_p, y_s = ffn(0, p_rows), ffn(p_rows, s_rows)

    y_p = y_p.reshape(nbp, seq_p, d).astype(x_prompt.dtype)
    y_s = y_s.reshape(nbs, seq_s, d).astype(x_sample.dtype)
    new_k = jnp.stack(k_list, axis=1).astype(cache_k.dtype)
    new_v = jnp.stack(v_list, axis=1).astype(cache_v.dtype)
    new_s = jnp.stack(s_list, axis=1).astype(state_lru.dtype)
    return y_p, y_s, new_k, new_v, new_s
```

```python
import functools
import math

import jax
import jax.numpy as jnp
from jax import lax
from jax.experimental import pallas as pl
from jax.experimental.pallas import tpu as pltpu

F32 = jnp.float32
BF16 = jnp.bfloat16
MIB = 1024 * 1024

EPS = 1e-6
GRID_W = 64
ROPE_THETA = 10000.0
LRU_C = 8.0
FOURIER_GROUPS = 4
POOL_WINDOWS = (2, 4, 8, 16)
N_BRANCH = 4
HALO = 8
CONV_LEFT = 2
GATE_TN = 256
FFN_CHUNK = 512
SEG_PAD = 8


def _params(semantics, vmem_mib):
    return pltpu.CompilerParams(dimension_semantics=semantics,
                                vmem_limit_bytes=vmem_mib * MIB)


def _rms(x, gain):
    ms = jnp.mean(x * x, axis=-1, keepdims=True)
    return x * lax.rsqrt(ms + EPS) * gain


def _sigmoid(x):
    return 0.5 * jnp.tanh(0.5 * x) + 0.5


def _row_rinv(src_ref, stat_ref):
    n, d = src_ref.shape

    def body(i, carry):
        sl = pl.ds(pl.multiple_of(i * 8, 8), 8)
        x = src_ref[sl, :]
        ms = jnp.sum(x * x, axis=-1, keepdims=True) * (1.0 / d)
        stat_ref[sl, :] = jnp.broadcast_to(lax.rsqrt(ms + EPS), (8, stat_ref.shape[1]))
        return carry
    lax.fori_loop(0, n // 8, body, 0, unroll=8)


def _rinv_rows(stat_ref, sl, d):
    return jnp.tile(stat_ref[sl, :], (1, d // stat_ref.shape[1]))


def _softplus(x):
    return jnp.maximum(x, 0.0) + jnp.log1p(jnp.exp(-jnp.abs(x)))


def _group_of_tile(i, tm, p_rows, s_rows):
    r = i * tm
    return jnp.where(r < p_rows, 0, 1 + (r - p_rows) // s_rows)


def _mod_spec(k, d, tm, p_rows, s_rows, tile0=0):
    return pl.BlockSpec((None, 1, d),
                        lambda i, *_: (_group_of_tile(tile0 + i, tm, p_rows, s_rows), 0, k))


def _fill(kernel, in_specs, args, into):
    if into is None:
        return kernel, {}
    n = len(args)
    in_specs.append(pl.BlockSpec(memory_space=pl.ANY))
    args.append(into)

    def body(*refs):
        return kernel(*refs[:n], *refs[n + 1:])
    return body, {n: 0}


def _side_casts(weights, nsteps, step_of):
    in_specs, args, out_specs, out_shapes, plan = [], [], [], [], []
    for w, layer, layout in weights:
        _, rows, cols = w.shape
        br = next(b for b in range(16, rows + 1, 16) if rows % b == 0 and rows // b <= nsteps)
        last = rows // br - 1
        in_specs.append(pl.BlockSpec(
            (None, br, cols),
            functools.partial(lambda *g, layer, last: (layer, jnp.minimum(step_of(*g), last), 0),
                              layer=layer, last=last)))
        args.append(w)
        outs = []
        for ncols, moves in (layout or [(cols, None)]):
            outs.append((len(out_specs), moves))
            out_specs.append(pl.BlockSpec(
                (br, ncols),
                functools.partial(lambda *g, last: (jnp.minimum(step_of(*g), last), 0),
                                  last=last)))
            out_shapes.append(jax.ShapeDtypeStruct((rows, ncols), BF16))
        plan.append(outs)
    return in_specs, args, out_specs, out_shapes, plan


def _run_side_casts(src_refs, dst_refs, plan):
    for src, outs in zip(src_refs, plan):
        for dst_idx, moves in outs:
            dst = dst_refs[dst_idx]
            if moves is None:
                dst[...] = src[...].astype(BF16)
            else:
                for dst_col, src_col, width in moves:
                    dst[:, dst_col:dst_col + width] = src[:, src_col:src_col + width].astype(BF16)


def _w_in_layout(zw, dm, tn):
    gates = [((j * N_BRANCH + n) * tn, zw + n * dm + j * tn, tn)
             for j in range(dm // tn) for n in range(N_BRANCH)]
    return [(zw, [(0, 0, zw)]), (N_BRANCH * dm, gates)]


def _mod_kernel(c_ref, w_ref, b_ref, o_ref):
    c = c_ref[...]
    s = (c * jax.nn.sigmoid(c)).astype(BF16)
    o_ref[...] = jnp.dot(s, w_ref[...].astype(BF16),
                         preferred_element_type=F32) + b_ref[...]


def _mod_all(cvec, w_mod, b_mod, tn=512):
    depth, d, n = w_mod.shape
    return pl.pallas_call(
        _mod_kernel,
        out_shape=jax.ShapeDtypeStruct((depth, 8, n), F32),
        grid=(depth, n // tn),
        in_specs=[pl.BlockSpec((8, d), lambda l, j: (0, 0)),
                  pl.BlockSpec((None, d, tn), lambda l, j: (l, 0, j)),
                  pl.BlockSpec((None, 1, tn), lambda l, j: (l, 0, j))],
        out_specs=pl.BlockSpec((None, 8, tn), lambda l, j: (l, 0, j)),
        compiler_params=_params(("arbitrary", "arbitrary"), 40),
        name="mod",
    )(cvec, w_mod, b_mod.reshape(depth, 1, n))


def _prenorm_kernel(x_ref, g_ref, sh_ref, sc_ref, o_ref, stat_ref):
    tm, d = x_ref.shape
    slab = math.gcd(16, tm)
    _row_rinv(x_ref, stat_ref)

    def body(i, carry):
        sl = pl.ds(pl.multiple_of(i * slab, slab), slab)
        h = x_ref[sl, :] * _rinv_rows(stat_ref, sl, d) * g_ref[...]
        o_ref[sl, :] = (h * (1.0 + sc_ref[...]) + sh_ref[...]).astype(o_ref.dtype)
        return carry
    lax.fori_loop(0, tm // slab, body, 0, unroll=2)


def _prenorm(rng, total_rows, gain, mod_l, k_shift, p_rows, s_rows, tm, into=None):
    src, src_row0, uni_row0, nrows = rng
    d = src.shape[1]
    st0, ut0 = src_row0 // tm, uni_row0 // tm
    in_specs = [pl.BlockSpec((tm, d), lambda i: (st0 + i, 0)),
                pl.BlockSpec((1, d), lambda i: (0, 0)),
                _mod_spec(k_shift, d, tm, p_rows, s_rows, ut0),
                _mod_spec(k_shift + 1, d, tm, p_rows, s_rows, ut0)]
    args = [src, gain.reshape(1, d), mod_l, mod_l]
    body, aliases = _fill(_prenorm_kernel, in_specs, args, into)
    return pl.pallas_call(
        body,
        out_shape=jax.ShapeDtypeStruct((total_rows, d), BF16),
        grid=(nrows // tm,),
        in_specs=in_specs,
        out_specs=pl.BlockSpec((tm, d), lambda i: (ut0 + i, 0)),
        scratch_shapes=[pltpu.VMEM((tm, 128), F32)],
        input_output_aliases=aliases,
        compiler_params=_params(("arbitrary",), 40),
        name="prenorm",
    )(*args)


def _mm_kernel(a_ref, b_ref, o_ref):
    o_ref[...] = jnp.dot(a_ref[...], b_ref[...],
                         preferred_element_type=F32).astype(o_ref.dtype)


def _mm_side_kernel(*refs, n_cast, plan):
    a_ref, b_ref = refs[:2]
    _run_side_casts(refs[2:2 + n_cast], refs[3 + n_cast:], plan)
    _mm_kernel(a_ref, b_ref, refs[2 + n_cast])


def _matmul_cols(a, w, n_cols, tm, tn, out_dtype, vmem_mib, name, side=()):
    m, k = a.shape
    tm, tn = min(tm, m), min(tn, n_cols)
    nn = n_cols // tn
    c_in, c_args, c_out, c_shapes, plan = _side_casts(side, (m // tm) * nn,
                                                      lambda i, j: i * nn + j)
    res = pl.pallas_call(
        functools.partial(_mm_side_kernel, n_cast=len(side), plan=plan),
        out_shape=[jax.ShapeDtypeStruct((m, n_cols), out_dtype)] + c_shapes,
        grid=(m // tm, nn),
        in_specs=[pl.BlockSpec((tm, k), lambda i, j: (i, 0)),
                  pl.BlockSpec((k, tn), lambda i, j: (0, j))] + c_in,
        out_specs=[pl.BlockSpec((tm, tn), lambda i, j: (i, j))] + c_out,
        compiler_params=_params(("arbitrary", "arbitrary"), vmem_mib),
        name=name,
    )(a, w, *c_args)
    return res[0], res[1:]


def _matmul_batched_rhs(a, b, tm, tn, out_dtype, vmem_mib, name, total_rows, row0, into=None):
    m, k = a.shape
    nb, _, n = b.shape
    tm, tn = min(tm, m), min(tn, n)
    t0, nt = row0 // tm, m // tm
    in_specs = [pl.BlockSpec((tm, k), lambda bi, i, j: (i, 0)),
                pl.BlockSpec((None, k, tn), lambda bi, i, j: (bi, 0, j))]
    args = [a, b]
    body, aliases = _fill(_mm_kernel, in_specs, args, into)
    return pl.pallas_call(
        body,
        out_shape=jax.ShapeDtypeStruct((total_rows, n), out_dtype),
        grid=(nb, nt, n // tn),
        in_specs=in_specs,
        out_specs=pl.BlockSpec((tm, tn), lambda bi, i, j: (t0 + bi * nt + i, j)),
        input_output_aliases=aliases,
        compiler_params=_params(("arbitrary", "arbitrary", "arbitrary"), vmem_mib),
        name=name,
    )(*args)


FOLD_MIN_SEQ = 1024
FLIP_BLOCK = 128
FOUR1_ROWS = 4096


def _fold_cols(seq):
    return -(-(seq // 2 + 1) // 128) * 128


def _dft_matrix(n, kcols, sign, fold, split=64):
    col = jnp.arange(2 * kcols, dtype=jnp.int32)
    t = col % kcols
    phase = (col // kcols).astype(F32) * (sign * 0.5 * math.pi)
    w = jnp.full((2 * kcols,), 1.0 / math.sqrt(n), F32)
    if fold:
        w = w * jnp.where((t == 0) | (t == n // 2), 0.5, jnp.where(t < n // 2, 1.0, 0.0))

    def angle(rows, period):
        r = jnp.arange(rows, dtype=jnp.int32)
        return ((r[:, None] * t[None, :]) % period).astype(F32) * (2.0 * math.pi / period)

    if n <= split or n % split:
        return jnp.cos(angle(n, n) + phase) * w
    ang_a = angle(n // split, n // split)
    ang_b = angle(split, n) + phase
    tab = (jnp.cos(ang_a)[:, None, :] * jnp.cos(ang_b)[None]
           - jnp.sin(ang_a)[:, None, :] * jnp.sin(ang_b)[None]) * w
    return tab.reshape(n, 2 * kcols)


def _four1_kernel(u_ref, t_ref, o_ref, *, seq, gw, kfold):
    u = u_ref[...].astype(BF16)
    pq = jnp.dot(u, t_ref[...], preferred_element_type=F32)
    if not kfold:
        for b in range(o_ref.shape[0]):
            o_ref[b, 0:seq, :] = pq[b * seq:(b + 1) * seq, :gw].astype(o_ref.dtype)
            o_ref[b, seq:2 * seq, :] = pq[b * seq:(b + 1) * seq, gw:].astype(o_ref.dtype)
        return
    pqb = pq.astype(BF16)
    nblk = seq // FLIP_BLOCK
    r = lax.broadcasted_iota(jnp.int32, (FLIP_BLOCK, FLIP_BLOCK), 0)
    c = lax.broadcasted_iota(jnp.int32, (FLIP_BLOCK, FLIP_BLOCK), 1)
    anti = jnp.where(r + c == FLIP_BLOCK - 1, 1.0, 0.0).astype(BF16)
    flipped = jnp.concatenate(
        [jnp.dot(anti, pqb[(nblk - 1 - i) * FLIP_BLOCK:(nblk - i) * FLIP_BLOCK],
                 preferred_element_type=F32) for i in range(kfold // FLIP_BLOCK)], axis=0)
    row = lax.broadcasted_iota(jnp.int32, flipped.shape, 0)
    head = pqb[0:kfold].astype(F32)
    rev = jnp.where(row == 0, head[0:1], pltpu.roll(flipped, 1, axis=0))
    o_ref[0, 0:kfold, :] = (head[:, :gw] + rev[:, :gw]).astype(o_ref.dtype)
    o_ref[0, kfold:2 * kfold, :] = (head[:, gw:] - rev[:, gw:]).astype(o_ref.dtype)


def _fourier_stage1(z, tab, row0, nb, seq, bw, col0):
    gw = bw // FOURIER_GROUPS
    kfold = _fold_cols(seq) if seq >= FOLD_MIN_SEQ else 0
    krows = 2 * (kfold or seq)
    per = 1 if kfold else math.gcd(math.gcd(nb, max(1, FOUR1_ROWS // seq)), row0 // seq)
    rb0, cb0 = row0 // (per * seq), col0 // gw
    return pl.pallas_call(
        functools.partial(_four1_kernel, seq=seq, gw=gw, kfold=kfold),
        out_shape=jax.ShapeDtypeStruct((nb, krows, bw), BF16),
        grid=(nb // per, FOURIER_GROUPS),
        in_specs=[pl.BlockSpec((per * seq, gw), lambda b, g: (rb0 + b, cb0 + g)),
                  pl.BlockSpec((gw, 2 * gw), lambda b, g: (0, 0))],
        out_specs=pl.BlockSpec((per, krows, gw), lambda b, g: (b, 0, g)),
        compiler_params=_params(("arbitrary", "arbitrary"), 48),
        name="fourier1",
    )(z, tab)


def _fourier_matrix(seq):
    if seq >= FOLD_MIN_SEQ:
        return _dft_matrix(seq, _fold_cols(seq), 1.0, True).astype(BF16)
    return _dft_matrix(seq, seq, 1.0, False).astype(BF16)


def _lru_kernel(*refs, seq, rows, nseg, seg_len, chained, chunk):
    if chained:
        (zx_ref, zy_ref, cw_ref, cb_ref, wa_ref, wx_ref, ba_ref, bx_ref, lam_ref,
         h0_ref, o_ref, pad_ref, a0_ref, a1_ref, b0_ref, b1_ref) = refs
        st_ref = None
    else:
        (zx_ref, zy_ref, cw_ref, cb_ref, wa_ref, wx_ref, ba_ref, bx_ref, lam_ref,
         o_ref, st_ref, pad_ref, a0_ref, a1_ref, b0_ref, b1_ref) = refs
        h0_ref = None
    a_refs, b_refs = (a0_ref, a1_ref), (b0_ref, b1_ref)
    cw_ch = zx_ref.shape[1]
    ext = chunk + 2 * HALO
    n_taps = cw_ref.shape[0]
    pitch = seg_len + SEG_PAD
    segs = chunk // seg_len

    pad_ref[0:HALO, :] = jnp.zeros((HALO, cw_ch), F32)
    pad_ref[HALO + rows:, :] = jnp.zeros((HALO, cw_ch), F32)
    pad_ref[HALO:HALO + rows, :] = zx_ref[...]

    neg_sp = [-LRU_C * _softplus(-lam_ref[d:d + 1, :]) for d in range(2)]

    def seg_rows(ci, j):
        return pl.ds(pl.multiple_of(ci * (segs * pitch), 8) + j * pitch, seg_len)

    def gates(ci, carry):
        r0 = pl.multiple_of(ci * chunk, chunk)
        xe = pad_ref[pl.ds(r0, ext), :]
        t = (r0 + lax.broadcasted_iota(jnp.int32, (chunk, cw_ch), 0)) & (seq - 1)
        xc = jnp.zeros((chunk, cw_ch), F32) + cb_ref[...]
        for j in range(n_taps):
            d = j - CONV_LEFT
            if d == 0:
                xs = xe[HALO:HALO + chunk]
            else:
                xs = pltpu.roll(xe, (-d) % ext, axis=0)[HALO:HALO + chunk]
                xs = jnp.where((t >= -d) if d < 0 else (t < seq - d), xs, 0.0)
            xc = xc + xs * cw_ref[j:j + 1, :]
        xcb = xc.astype(BF16)
        for d in range(2):
            r = _sigmoid(jnp.dot(xcb, wa_ref[d], preferred_element_type=F32)
                         + ba_ref[d:d + 1, :])
            i = _sigmoid(jnp.dot(xcb, wx_ref[d], preferred_element_type=F32)
                         + bx_ref[d:d + 1, :])
            a = jnp.exp(r * neg_sp[d])
            bterm = jnp.sqrt(1.0 - a * a) * (i * xc)
            for j in range(segs):
                a_refs[d][seg_rows(ci, j), :] = a[j * seg_len:(j + 1) * seg_len]
                b_refs[d][seg_rows(ci, j), :] = bterm[j * seg_len:(j + 1) * seg_len]
        return carry

    lax.fori_loop(0, rows // chunk, gates, 0)

    def scan(direction):
        def body(s, carry):
            h, p = carry
            pos = s if direction == 0 else seg_len - 1 - s
            idx = pl.ds(pos, nseg, stride=pitch)
            a = a_refs[direction][idx, :]
            h = a * h + b_refs[direction][idx, :]
            b_refs[direction][idx, :] = h
            if chained:
                p = a * p
                a_refs[direction][idx, :] = p
            return h, p
        init = (jnp.zeros((nseg, cw_ch), F32), jnp.ones((nseg, cw_ch), F32))
        lax.fori_loop(0, seg_len, body, init, unroll=8)

    scan(0)
    scan(1)

    if chained:
        carry = h0_ref[0:1, :]
        for q in range(nseg):
            sl = slice(q * pitch, q * pitch + seg_len)
            hq = b_refs[0][sl, :] + a_refs[0][sl, :] * carry
            b_refs[0][sl, :] = hq
            carry = hq[seg_len - 1:seg_len, :]
        carry = h0_ref[1:2, :]
        for q in reversed(range(nseg)):
            sl = slice(q * pitch, q * pitch + seg_len)
            hq = b_refs[1][sl, :] + a_refs[1][sl, :] * carry
            b_refs[1][sl, :] = hq
            carry = hq[0:1, :]
    else:
        st_ref[0] = b_refs[0][pl.ds(seq - 1, nseg, stride=pitch), :]
        st_ref[1] = b_refs[1][pl.ds(0, nseg, stride=pitch), :]

    def emit(ci, carry):
        r0 = pl.multiple_of(ci * chunk, chunk)
        for j in range(segs):
            sl = pl.ds(r0 + j * seg_len, seg_len)
            hsum = b_refs[0][seg_rows(ci, j), :] + b_refs[1][seg_rows(ci, j), :]
            o_ref[sl, :] = (hsum * jax.nn.gelu(zy_ref[sl, :])).astype(o_ref.dtype)
        return carry

    lax.fori_loop(0, rows // chunk, emit, 0)


def _lru(z, layer, conv_w, conv_b, wa, wx, ba, bx, lam, h0, *, row0, rows, ngroups,
         seq, nseg, seg_len, col_x, col_y, bw, into=None):
    cb = wa.shape[-1]
    ncb = bw // cb
    chained = h0 is not None
    chunk = max(min(256, rows), seg_len)
    assert chunk % seg_len == 0 and rows % chunk == 0 and seg_len % 8 == 0
    assert chained or seg_len == seq
    rb0 = row0 // rows
    cx0, cy0 = col_x // cb, col_y // cb
    in_specs = [
        pl.BlockSpec((rows, cb), lambda g, c: (rb0 + g, cx0 + c)),
        pl.BlockSpec((rows, cb), lambda g, c: (rb0 + g, cy0 + c)),
        pl.BlockSpec((None, conv_w.shape[1], cb), lambda g, c: (layer, 0, c)),
        pl.BlockSpec((None, 1, cb), lambda g, c: (layer, 0, c)),
        pl.BlockSpec((None, 2, None, cb, cb), lambda g, c: (layer, 0, c, 0, 0)),
        pl.BlockSpec((None, 2, None, cb, cb), lambda g, c: (layer, 0, c, 0, 0)),
        pl.BlockSpec((None, 2, cb), lambda g, c: (layer, 0, c)),
        pl.BlockSpec((None, 2, cb), lambda g, c: (layer, 0, c)),
        pl.BlockSpec((None, 2, cb), lambda g, c: (layer, 0, c)),
    ]
    args = [z, z, conv_w, conv_b.reshape(conv_b.shape[0], 1, bw), wa, wx, ba, bx, lam]
    out_shape = [jax.ShapeDtypeStruct((z.shape[0], bw), BF16)]
    out_specs = [pl.BlockSpec((rows, cb), lambda g, c: (rb0 + g, c))]
    if chained:
        in_specs.append(pl.BlockSpec((None, 2, cb), lambda g, c: (g, 0, c)))
        args.append(h0)
    else:
        assert ngroups == 1
        out_shape.append(jax.ShapeDtypeStruct((2, rows // seq, bw), F32))
        out_specs.append(pl.BlockSpec((2, rows // seq, cb), lambda g, c: (0, 0, c)))
    kern = functools.partial(_lru_kernel, seq=seq, rows=rows, nseg=nseg, seg_len=seg_len,
                             chained=chained, chunk=chunk)
    body, aliases = _fill(kern, in_specs, args, into)
    res = pl.pallas_call(
        body,
        out_shape=out_shape,
        grid=(ngroups, ncb),
        in_specs=in_specs,
        out_specs=out_specs,
        scratch_shapes=[pltpu.VMEM((rows + 2 * HALO, cb), F32)]
                       + [pltpu.VMEM((nseg * (seg_len + SEG_PAD), cb), F32)] * 4,
        input_output_aliases=aliases,
        compiler_params=_params(("arbitrary", "arbitrary"), 48),
        name="rglru",
    )(*args)
    return res if not chained else res[0]


def _rope_tables(seq, hd):
    rows = seq // GRID_W
    row = jnp.repeat(jnp.arange(rows), GRID_W).astype(F32)
    col = jnp.tile(jnp.arange(GRID_W), rows).astype(F32)
    ax = hd // 2
    inv = ROPE_THETA ** (-jnp.arange(0, ax, 2, dtype=F32) / ax)
    ang_r = row[:, None] * inv[None, :]
    ang_c = col[:, None] * inv[None, :]
    cos = jnp.concatenate([jnp.cos(ang_r)] * 2 + [jnp.cos(ang_c)] * 2, axis=-1)
    sin = jnp.concatenate([-jnp.sin(ang_r), jnp.sin(ang_r),
                           -jnp.sin(ang_c), jnp.sin(ang_c)], axis=-1)
    return cos, sin


def _rotate(x, cos, sin):
    hd = x.shape[1]
    q = hd // 4
    lane = lax.broadcasted_iota(jnp.int32, x.shape, 1)
    ahead = pltpu.roll(x, hd - q, axis=1)
    behind = pltpu.roll(x, q, axis=1)
    partner = jnp.where((lane & q) == 0, ahead, behind)
    return x * cos + partner * sin


def _attn_kernel(*refs, seq, cache_len, tq, grp, hd, rope, emit_kv):
    refs = list(refs)
    zq_ref, zk_ref, zv_ref, qn_ref, kn_ref = refs[:5]
    pos = 5
    if rope:
        cq_ref, sq_ref, ck_ref, sk_ref = refs[pos:pos + 4]
        pos += 4
    if cache_len:
        kc_ref, vc_ref = refs[pos:pos + 2]
        pos += 2
    o_ref = refs[pos]
    pos += 1
    if emit_kv:
        ko_ref, vo_ref = refs[pos:pos + 2]
        pos += 2
    ks_ref, vt_ref = refs[pos:pos + 2]

    @pl.when(pl.program_id(2) == 0)
    def _():
        k = _rms(zk_ref[...], kn_ref[...])
        v = zv_ref[...]
        if emit_kv:
            ko_ref[...] = k
            vo_ref[...] = v
        if rope:
            k = _rotate(k, ck_ref[...], sk_ref[...])
        ks_ref[0:seq, :] = k.astype(BF16)
        vt_ref[:, 0:seq] = v.T.astype(BF16)
        if cache_len:
            ks_ref[seq:seq + cache_len, :] = kc_ref[...].astype(BF16)
            vt_ref[:, seq:seq + cache_len] = vc_ref[...].T.astype(BF16)

    scale = hd ** -0.5 * math.log2(math.e)
    qs = []
    for hh in range(grp):
        q = _rms(zq_ref[:, hh * hd:(hh + 1) * hd], qn_ref[...])
        if rope:
            q = _rotate(q, cq_ref[...], sq_ref[...])
        qs.append((q * scale).astype(BF16))
    q = jnp.concatenate(qs, axis=0)
    s = lax.dot_general(ks_ref[...], q, (((1,), (1,)), ((), ())),
                        preferred_element_type=F32)
    p = jnp.exp2(s - jnp.max(s, axis=0, keepdims=True))
    acc = jnp.dot(vt_ref[...], p.astype(BF16), preferred_element_type=F32)
    out = (acc / jnp.sum(p, axis=0, keepdims=True)).T
    for hh in range(grp):
        o_ref[:, hh * hd:(hh + 1) * hd] = out[hh * tq:(hh + 1) * tq].astype(o_ref.dtype)


def _attention(z, layer, q_norm, k_norm, *, row0, nb, seq, n_kv, grp, hd, col_q, col_k, col_v,
               tq, rope_tabs=None, cache=None, emit_kv=False, into=None):
    qw = grp * hd
    cache_len = 0 if cache is None else cache[0].shape[2]
    nq = seq // tq
    rbq0, rbk0 = row0 // tq, row0 // seq
    cq0, ck0, cv0 = col_q // qw, col_k // hd, col_v // hd
    in_specs = [
        pl.BlockSpec((tq, qw), lambda b, h, i: (rbq0 + b * nq + i, cq0 + h)),
        pl.BlockSpec((seq, hd), lambda b, h, i: (rbk0 + b, ck0 + h)),
        pl.BlockSpec((seq, hd), lambda b, h, i: (rbk0 + b, cv0 + h)),
        pl.BlockSpec((None, 1, hd), lambda b, h, i: (layer, 0, 0)),
        pl.BlockSpec((None, 1, hd), lambda b, h, i: (layer, 0, 0)),
    ]
    args = [z, z, z, q_norm.reshape(-1, 1, hd), k_norm.reshape(-1, 1, hd)]
    if rope_tabs is not None:
        cos, sin = rope_tabs
        in_specs += [pl.BlockSpec((tq, hd), lambda b, h, i: (i, 0)),
                     pl.BlockSpec((tq, hd), lambda b, h, i: (i, 0)),
                     pl.BlockSpec((seq, hd), lambda b, h, i: (0, 0)),
                     pl.BlockSpec((seq, hd), lambda b, h, i: (0, 0))]
        args += [cos, sin, cos, sin]
    if cache is not None:
        in_specs += [pl.BlockSpec((None, None, cache_len, hd), lambda b, h, i: (b, h, 0, 0)),
                     pl.BlockSpec((None, None, cache_len, hd), lambda b, h, i: (b, h, 0, 0))]
        args += list(cache)
    out_shape = [jax.ShapeDtypeStruct((z.shape[0], n_kv * qw), BF16)]
    out_specs = [pl.BlockSpec((tq, qw), lambda b, h, i: (rbq0 + b * nq + i, h))]
    if emit_kv:
        out_shape += [jax.ShapeDtypeStruct((nb * seq, n_kv * hd), F32)] * 2
        out_specs += [pl.BlockSpec((seq, hd), lambda b, h, i: (b, h))] * 2
    kern = functools.partial(_attn_kernel, seq=seq, cache_len=cache_len, tq=tq, grp=grp,
                             hd=hd, rope=rope_tabs is not None, emit_kv=emit_kv)
    body, aliases = _fill(kern, in_specs, args, into)
    res = pl.pallas_call(
        body,
        out_shape=out_shape,
        grid=(nb, n_kv, nq),
        in_specs=in_specs,
        out_specs=out_specs,
        scratch_shapes=[pltpu.VMEM((seq + cache_len, hd), BF16),
                        pltpu.VMEM((hd, seq + cache_len), BF16)],
        input_output_aliases=aliases,
        compiler_params=_params(("arbitrary", "arbitrary", "arbitrary"), 48),
        name="attention",
    )(*args)
    return res if emit_kv else res[0]


def _pool_kernel(u_ref, w_ref, sc_ref, o_ref, pad_ref, *, rows, chunk, seq_p, seq_s, groups_p):
    gw = u_ref.shape[1]
    ext = chunk + 2 * HALO
    seq = jnp.where(pl.program_id(0) < groups_p, seq_p, seq_s)
    pad_ref[0:HALO, :] = jnp.zeros((HALO, gw), F32)
    pad_ref[HALO + rows:, :] = jnp.zeros((HALO, gw), F32)
    pad_ref[HALO:HALO + rows, :] = u_ref[...]

    for gi, win in enumerate(POOL_WINDOWS):
        hw = win // 2

        def masked_sum(xe, t, hw=hw):
            acc = xe[HALO:HALO + chunk]
            for d in range(-hw, hw):
                if d == 0:
                    continue
                xs = pltpu.roll(xe, (-d) % ext, axis=0)[HALO:HALO + chunk]
                acc = acc + jnp.where((t >= -d) if d < 0 else (t < seq - d), xs, 0.0)
            return acc

        def doubled_sum(xe, t, hw=hw):
            w = xe + pltpu.roll(xe, 1, axis=0)
            span = 1
            while span < hw:
                w = pltpu.roll(w, span, axis=0) + pltpu.roll(w, ext - span, axis=0)
                span *= 2
            return w[HALO:HALO + chunk]

        def run(window_sum, hw=hw):
            def body(ci, carry):
                r0 = pl.multiple_of(ci * chunk, chunk)
                xe = pad_ref[pl.ds(r0, ext), :]
                t = (r0 + lax.broadcasted_iota(jnp.int32, (chunk, gw), 0)) & (seq - 1)
                cnt = (jnp.minimum(t + hw, seq) - jnp.maximum(t - hw, 0)).astype(F32)
                pooled = window_sum(xe, t) / cnt - xe[HALO:HALO + chunk]
                y = jnp.dot(pooled.astype(BF16), w_ref[...], preferred_element_type=F32)
                o_ref[pl.ds(r0, chunk), :] = (y * sc_ref[...]).astype(o_ref.dtype)
                return carry

            lax.fori_loop(0, rows // chunk, body, 0)

        this_group = pl.program_id(1) == gi
        whole_seq = pl.program_id(0) >= groups_p
        pl.when(jnp.logical_and(this_group, whole_seq))(functools.partial(run, doubled_sum))
        pl.when(jnp.logical_and(this_group, jnp.logical_not(whole_seq)))(
            functools.partial(run, masked_sum))


def _pool(z, layer, pool_w, pool_scale, *, rows, groups_p, seq_p, seq_s, col0, bw):
    r = z.shape[0]
    g = pool_w.shape[1]
    gw = bw // g
    cb0 = col0 // gw
    chunk = min(256, rows)
    return pl.pallas_call(
        functools.partial(_pool_kernel, rows=rows, chunk=chunk, seq_p=seq_p, seq_s=seq_s,
                          groups_p=groups_p),
        out_shape=jax.ShapeDtypeStruct((r, bw), BF16),
        grid=(r // rows, g),
        in_specs=[pl.BlockSpec((rows, gw), lambda i, j: (i, cb0 + j)),
                  pl.BlockSpec((None, None, gw, gw), lambda i, j: (layer, j, 0, 0)),
                  pl.BlockSpec((None, 1, gw), lambda i, j: (layer, 0, j))],
        out_specs=pl.BlockSpec((rows, gw), lambda i, j: (i, j)),
        scratch_shapes=[pltpu.VMEM((rows + 2 * HALO, gw), F32)],
        compiler_params=_params(("arbitrary", "arbitrary"), 40),
        name="pool",
    )(z, pool_w, pool_scale.reshape(pool_scale.shape[0], 1, bw))


def _merge_kernel(*refs, n_cast, plan):
    h_ref, g_ref = refs[:2]
    o_refs = refs[2:2 + N_BRANCH]
    wb_ref = refs[2 + N_BRANCH]
    n_in = 3 + N_BRANCH
    out_ref = refs[n_in + n_cast]
    _run_side_casts(refs[n_in:n_in + n_cast], refs[n_in + n_cast + 1:], plan)
    tn = out_ref.shape[1]
    gates = jnp.dot(h_ref[...], g_ref[...], preferred_element_type=F32)
    acc = None
    for n, o_ref in enumerate(o_refs):
        gate = _sigmoid(gates[:, n * tn:(n + 1) * tn])
        y = jnp.dot(o_ref[...], wb_ref[n], preferred_element_type=F32)
        acc = gate * y if acc is None else acc + gate * y
    out_ref[...] = acc.astype(out_ref.dtype)


def _gate_merge(h, w_gates, branches, w_branch, side, tm=512):
    m, d = h.shape
    bw = w_branch.shape[1]
    dm = w_branch.shape[2]
    tm, tn = min(tm, m), GATE_TN
    nm = m // tm
    ospecs = [pl.BlockSpec((tm, bw), lambda j, i: (i, 0)) for _ in range(N_BRANCH)]
    c_in, c_args, c_out, c_shapes, plan = _side_casts(side, (dm // tn) * nm,
                                                      lambda j, i: j * nm + i)
    res = pl.pallas_call(
        functools.partial(_merge_kernel, n_cast=len(side), plan=plan),
        out_shape=[jax.ShapeDtypeStruct((m, dm), BF16)] + c_shapes,
        grid=(dm // tn, nm),
        in_specs=[pl.BlockSpec((tm, d), lambda j, i: (i, 0)),
                  pl.BlockSpec((d, N_BRANCH * tn), lambda j, i: (0, j))] + ospecs
                 + [pl.BlockSpec((N_BRANCH, bw, tn), lambda j, i: (0, 0, j))] + c_in,
        out_specs=[pl.BlockSpec((tm, tn), lambda j, i: (i, j))] + c_out,
        compiler_params=_params(("arbitrary", "arbitrary"), 58),
        name="gate_merge",
    )(h, w_gates, *branches, w_branch, *c_args)
    return res[0], res[1:]


def _out_residual_kernel(m_ref, w_ref, x_ref, g_ref, gate_ref, o_ref, stat_ref):
    tm, d = x_ref.shape
    slab = math.gcd(16, tm)
    o_ref[...] = jnp.dot(m_ref[...], w_ref[...], preferred_element_type=F32)
    _row_rinv(o_ref, stat_ref)
    scale = g_ref[...] * gate_ref[...]

    def body(i, carry):
        sl = pl.ds(pl.multiple_of(i * slab, slab), slab)
        o_ref[sl, :] = x_ref[sl, :] + o_ref[sl, :] * _rinv_rows(stat_ref, sl, d) * scale
        return carry
    lax.fori_loop(0, tm // slab, body, 0, unroll=2)


def _out_residual(rng, merged, w_out, gain, mod_l, k_gate, p_rows, s_rows, tm, into=None):
    src, src_row0, uni_row0, nrows = rng
    r, d = merged.shape
    st0, ut0 = src_row0 // tm, uni_row0 // tm
    in_specs = [pl.BlockSpec((tm, d), lambda i: (ut0 + i, 0)),
                pl.BlockSpec((d, d), lambda i: (0, 0), pipeline_mode=pl.Buffered(1)),
                pl.BlockSpec((tm, d), lambda i: (st0 + i, 0)),
                pl.BlockSpec((1, d), lambda i: (0, 0)),
                _mod_spec(k_gate, d, tm, p_rows, s_rows, ut0)]
    args = [merged, w_out, src, gain.reshape(1, d), mod_l]
    body, aliases = _fill(_out_residual_kernel, in_specs, args, into)
    return pl.pallas_call(
        body,
        out_shape=jax.ShapeDtypeStruct((r, d), F32),
        grid=(nrows // tm,),
        in_specs=in_specs,
        out_specs=pl.BlockSpec((tm, d), lambda i: (ut0 + i, 0)),
        scratch_shapes=[pltpu.VMEM((tm, 128), F32)],
        input_output_aliases=aliases,
        compiler_params=_params(("arbitrary",), 60),
        name="out_residual",
    )(*args)


def _ffn_kernel(x_ref, gpre_ref, sh_ref, sc_ref, gate_ref, gpost_ref, wg_ref, wu_ref, wd_ref,
                o_ref, h_ref, stat_ref, cst_ref, *, slab, overlap):
    k = pl.program_id(1)
    tm, d = x_ref.shape
    nslab = tm // slab

    def rows(i):
        return pl.ds(pl.multiple_of(i * slab, slab), slab)

    @pl.when(k == 0)
    def _():
        _row_rinv(x_ref, stat_ref)
        cst_ref[0] = jnp.broadcast_to(gpre_ref[...] * (1.0 + sc_ref[...]), (slab, d))
        cst_ref[1] = jnp.broadcast_to(sh_ref[...], (slab, d))

        def body(i, carry):
            h = x_ref[rows(i), :] * _rinv_rows(stat_ref, rows(i), d) * cst_ref[0] + cst_ref[1]
            h_ref[rows(i), :] = h.astype(BF16)
            o_ref[rows(i), :] = jnp.zeros((slab, d), F32)
            return carry
        lax.fori_loop(0, nslab, body, 0, unroll=2)

    last = pl.num_programs(1) - 1

    def chunk(lo):
        h = h_ref[...]
        a = jnp.dot(h, wg_ref[:, lo:], preferred_element_type=F32)
        b = jnp.dot(h, wu_ref[:, lo:], preferred_element_type=F32)
        act = (a * _sigmoid(a) * b).astype(BF16)
        o_ref[...] += jnp.dot(act, wd_ref[lo:, :], preferred_element_type=F32)

    if overlap:
        pl.when(k < last)(lambda: chunk(0))
        pl.when(k == last)(lambda: chunk(overlap))
    else:
        chunk(0)

    @pl.when(k == last)
    def _():
        _row_rinv(o_ref, stat_ref)
        cst_ref[0] = jnp.broadcast_to(gpost_ref[...] * gate_ref[...], (slab, d))

        def body(i, carry):
            y = o_ref[rows(i), :] * _rinv_rows(stat_ref, rows(i), d) * cst_ref[0]
            o_ref[rows(i), :] = x_ref[rows(i), :] + y
            return carry
        lax.fori_loop(0, nslab, body, 0, unroll=2)


def _ffn(x, row0, nrows, g_pre, g_post, mod_l, wg, wu, wd, p_rows, s_rows, tm=512):
    d = x.shape[1]
    dff = wg.shape[1]
    tm, tc = min(tm, nrows), min(FFN_CHUNK, dff)
    nk = pl.cdiv(dff, tc)
    overlap = nk * tc - dff

    def chunk_start(k):
        return pl.multiple_of(jnp.minimum(k * tc, dff - tc), math.gcd(tc, dff))

    t0 = row0 // tm
    return pl.pallas_call(
        functools.partial(_ffn_kernel, slab=math.gcd(16, tm), overlap=overlap),
        out_shape=jax.ShapeDtypeStruct((nrows, d), F32),
        grid=(nrows // tm, nk),
        in_specs=[pl.BlockSpec((tm, d), lambda i, k: (t0 + i, 0), pipeline_mode=pl.Buffered(1)),
                  pl.BlockSpec((1, d), lambda i, k: (0, 0)),
                  _mod_spec(3, d, tm, p_rows, s_rows, t0),
                  _mod_spec(4, d, tm, p_rows, s_rows, t0),
                  _mod_spec(5, d, tm, p_rows, s_rows, t0),
                  pl.BlockSpec((1, d), lambda i, k: (0, 0)),
                  pl.BlockSpec((pl.Element(d), pl.Element(tc)),
                               lambda i, k: (0, chunk_start(k))),
                  pl.BlockSpec((pl.Element(d), pl.Element(tc)),
                               lambda i, k: (0, chunk_start(k))),
                  pl.BlockSpec((pl.Element(tc), pl.Element(d)),
                               lambda i, k: (chunk_start(k), 0))],
        out_specs=pl.BlockSpec((tm, d), lambda i, k: (i, 0)),
        scratch_shapes=[pltpu.VMEM((tm, d), BF16), pltpu.VMEM((tm, 128), F32),
                        pltpu.VMEM((2, math.gcd(16, tm), d), F32)],
        compiler_params=_params(("arbitrary", "arbitrary"), 60),
        name="ffn",
    )(x, g_pre.reshape(1, d), mod_l, mod_l, mod_l, g_post.reshape(1, d), wg, wu, wd)


def kernel(x_prompt, x_sample, c, cache_k, cache_v, state_lru, c_ctx, w_mod, b_mod, g_pre_mix, g_post_mix, g_pre_ffn, g_post_ffn, w_in, conv_w, conv_b, lru_wa, lru_ba, lru_wx, lru_bx, lru_lambda, q_norm, k_norm, pool_w, pool_scale, w_branch, w_out, w_ffn_gate, w_ffn_up, w_ffn_down):
    nbp, seq_p, d = x_prompt.shape
    nbs, seq_s, _ = x_sample.shape
    depth = w_in.shape[0]
    bw = w_branch.shape[2]
    cache_len, n_kv, hd = cache_k.shape[2:]
    grp = bw // hd // n_kv
    kvw = n_kv * hd
    p_rows, s_rows = nbp * seq_p, nbs * seq_s
    rows = p_rows + s_rows
    assert seq_p & (seq_p - 1) == 0 and seq_s & (seq_s - 1) == 0
    assert p_rows % seq_s == 0, "row groups of seq_s rows must tile the prompt rows"
    col_a, col_bx, col_by, col_q = 0, bw, 2 * bw, 3 * bw
    col_k = col_q + bw
    col_v = col_k + kvw
    col_d = col_v + kvw
    col_g = col_d + bw
    zw = col_g

    def rows_tile(pref):
        return math.gcd(pref, math.gcd(p_rows, seq_s))

    w_mix_l = w_in[0, :, :zw].astype(BF16)
    w_gates_l = w_branch_l = w_out_l = None
    in_layout = _w_in_layout(zw, d, min(GATE_TN, d))
    w_branch_2d = w_branch.reshape(depth, N_BRANCH * bw, d)
    wa_b = lru_wa.astype(BF16)
    wx_b = lru_wx.astype(BF16)
    pool_w_b = pool_w.astype(BF16)

    gw = bw // FOURIER_GROUPS
    tab_c = _dft_matrix(gw, gw, -1.0, False).astype(BF16)
    dft_p = _fourier_matrix(seq_p)
    dft_s = _fourier_matrix(seq_s)
    rope_tabs = _rope_tables(seq_s, hd)

    cvec = jnp.concatenate([c_ctx[None, :], c, jnp.zeros((8 - 1 - nbs, d), c.dtype)], axis=0)
    mod = _mod_all(cvec.astype(F32), w_mod, b_mod)

    cache_kt = jnp.transpose(cache_k, (0, 1, 3, 2, 4)).astype(F32)
    cache_vt = jnp.transpose(cache_v, (0, 1, 3, 2, 4)).astype(F32)

    x_ranges = [(x_prompt.reshape(p_rows, d).astype(F32), 0, 0, p_rows),
                (x_sample.reshape(s_rows, d).astype(F32), 0, p_rows, s_rows)]
    tm_norm = rows_tile(256)
    k_list, v_list, s_list = [], [], []
    y_p = y_s = None
    for l in range(depth):
        mod_l = mod[l].reshape(8, 1, 6 * d)
        h = None
        for rng in x_ranges:
            h = _prenorm(rng, rows, g_pre_mix[l], mod_l, 0, p_rows, seq_s, tm_norm, into=h)
        if l == 0:
            z, (_, w_gates_l, w_branch_l, w_out_l) = _matmul_cols(
                h, w_mix_l, zw, rows_tile(1024), 512, F32, 56, "in_proj",
                side=[(w_in, 0, in_layout), (w_branch_2d, 0, None), (w_out, 0, None)])
            w_branch_l = w_branch_l.reshape(N_BRANCH, bw, d)
        else:
            z, _ = _matmul_cols(h, w_mix_l, zw, rows_tile(1024), 512, F32, 48, "in_proj")

        pq_p = _fourier_stage1(z, tab_c, 0, nbp, seq_p, bw, col_a)
        o_a = _matmul_batched_rhs(dft_p, pq_p, 512, 1024, BF16, 48, "fourier2p", rows, 0)
        pq_s = _fourier_stage1(z, tab_c, p_rows, nbs, seq_s, bw, col_a)
        o_a = _matmul_batched_rhs(dft_s, pq_s, 512, 1024, BF16, 48, "fourier2s", rows, p_rows,
                                  into=o_a)

        lru_args = (conv_w, conv_b, wa_b, wx_b, lru_ba, lru_bx, lru_lambda)
        o_b, st = _lru(z, l, *lru_args, None, row0=0, rows=p_rows, ngroups=1, seq=seq_p,
                       nseg=nbp, seg_len=seq_p, col_x=col_bx, col_y=col_by, bw=bw)
        seg = min(128, seq_s)
        o_b = _lru(z, l, *lru_args, state_lru[:, l].astype(F32), row0=p_rows, rows=seq_s,
                   ngroups=nbs, seq=seq_s, nseg=seq_s // seg, seg_len=seg,
                   col_x=col_bx, col_y=col_by, bw=bw, into=o_b)
        s_list.append(jnp.transpose(st, (1, 0, 2)))

        att = dict(n_kv=n_kv, grp=grp, hd=hd, col_q=col_q, col_k=col_k, col_v=col_v)
        o_c, k_p, v_p = _attention(z, l, q_norm, k_norm, row0=0, nb=nbp, seq=seq_p, tq=seq_p,
                                   emit_kv=True, **att)
        o_c = _attention(z, l, q_norm, k_norm, row0=p_rows, nb=nbs, seq=seq_s,
                         tq=min(256, seq_s), rope_tabs=rope_tabs,
                         cache=(cache_kt[:, l], cache_vt[:, l]), into=o_c, **att)
        k_list.append(k_p.reshape(nbp, seq_p, n_kv, hd))
        v_list.append(v_p.reshape(nbp, seq_p, n_kv, hd))

        o_d = _pool(z, l, pool_w_b, pool_scale, rows=seq_s, groups_p=p_rows // seq_s,
                    seq_p=seq_p, seq_s=seq_s, col0=col_d, bw=bw)

        side = [(w_ffn_gate, l, None), (w_ffn_up, l, None), (w_ffn_down, l, None)]
        if l + 1 < depth:
            side += [(w_in, l + 1, in_layout), (w_branch_2d, l + 1, None), (w_out, l + 1, None)]
        merged, cast = _gate_merge(h, w_gates_l, (o_a, o_b, o_c, o_d), w_branch_l,
                                   side=side, tm=rows_tile(512))
        x = None
        for rng in x_ranges:
            x = _out_residual(rng, merged, w_out_l, g_post_mix[l], mod_l, 2, p_rows, seq_s,
                              tm_norm, into=x)
        ffn = functools.partial(_ffn, x, g_pre=g_pre_ffn[l], g_post=g_post_ffn[l],
                                mod_l=mod_l, wg=cast[0], wu=cast[1], wd=cast[2], p_rows=p_rows,
                                s_rows=seq_s, tm=rows_tile(512))
        if l + 1 < depth:
            w_mix_l, w_gates_l = cast[3], cast[4]
            w_branch_l, w_out_l = cast[5].reshape(N_BRANCH, bw, d), cast[6]
            x_ranges = [(ffn(0, rows), 0, 0, rows)]
        else:
            y_p, y_s = ffn(0, p_rows), ffn(p_rows, s_rows)

    y_p = y_p.reshape(nbp, seq_p, d).astype(x_prompt.dtype)
    y_s = y_s.reshape(nbs, seq_s, d).astype(x_sample.dtype)
    new_k = jnp.stack(k_list, axis=1).astype(cache_k.dtype)
    new_v = jnp.stack(v_list, axis=1).astype(cache_v.dtype)
    new_s = jnp.stack(s_list, axis=1).astype(state_lru.dtype)
    return y_p, y_s, new_k, new_v, new_s
```

```python
import functools
import math

import jax
import jax.numpy as jnp
from jax import lax
from jax.experimental import pallas as pl
from jax.experimental.pallas import tpu as pltpu

F32 = jnp.float32
BF16 = jnp.bfloat16
MIB = 1024 * 1024

EPS = 1e-6
GRID_W = 64
ROPE_THETA = 10000.0
LRU_C = 8.0
FOURIER_GROUPS = 4
POOL_WINDOWS = (2, 4, 8, 16)
N_BRANCH = 4
HALO = 8
CONV_LEFT = 2
GATE_TN = 256
FFN_CHUNK = 512
SEG_PAD = 8


def _params(semantics, vmem_mib):
    return pltpu.CompilerParams(dimension_semantics=semantics,
                                vmem_limit_bytes=vmem_mib * MIB)


def _rms(x, gain):
    ms = jnp.mean(x * x, axis=-1, keepdims=True)
    return x * lax.rsqrt(ms + EPS) * gain


def _sigmoid(x):
    return 0.5 * jnp.tanh(0.5 * x) + 0.5


def _row_rinv(src_ref, stat_ref):
    n, d = src_ref.shape

    rows = math.gcd(32, n)

    def body(i, carry):
        sl = pl.ds(pl.multiple_of(i * rows, rows), rows)
        x = src_ref[sl, :]
        ms = jnp.sum(x * x, axis=-1, keepdims=True) * (1.0 / d)
        stat_ref[sl, :] = jnp.broadcast_to(lax.rsqrt(ms + EPS), (rows, stat_ref.shape[1]))
        return carry
    lax.fori_loop(0, n // rows, body, 0, unroll=8)


def _rinv_rows(stat_ref, sl, d):
    return jnp.tile(stat_ref[sl, :], (1, d // stat_ref.shape[1]))


def _softplus(x):
    return jnp.maximum(x, 0.0) + jnp.log1p(jnp.exp(-jnp.abs(x)))


def _group_of_tile(i, tm, p_rows, s_rows):
    r = i * tm
    return jnp.where(r < p_rows, 0, 1 + (r - p_rows) // s_rows)


def _mod_spec(k, d, tm, p_rows, s_rows, tile0=0):
    return pl.BlockSpec((None, 1, d),
                        lambda i, *_: (_group_of_tile(tile0 + i, tm, p_rows, s_rows), 0, k))


def _fill(kernel, in_specs, args, into):
    if into is None:
        return kernel, {}
    n = len(args)
    in_specs.append(pl.BlockSpec(memory_space=pl.ANY))
    args.append(into)

    def body(*refs):
        return kernel(*refs[:n], *refs[n + 1:])
    return body, {n: 0}


def _side_casts(weights, nsteps, step_of):
    in_specs, args, out_specs, out_shapes, plan = [], [], [], [], []
    for w, layer, layout in weights:
        _, rows, cols = w.shape
        br = next(b for b in range(16, rows + 1, 16) if rows % b == 0 and rows // b <= nsteps)
        last = rows // br - 1
        in_specs.append(pl.BlockSpec(
            (None, br, cols),
            functools.partial(lambda *g, layer, last: (layer, jnp.minimum(step_of(*g), last), 0),
                              layer=layer, last=last)))
        args.append(w)
        outs = []
        for ncols, moves in (layout or [(cols, None)]):
            outs.append((len(out_specs), moves))
            out_specs.append(pl.BlockSpec(
                (br, ncols),
                functools.partial(lambda *g, last: (jnp.minimum(step_of(*g), last), 0),
                                  last=last)))
            out_shapes.append(jax.ShapeDtypeStruct((rows, ncols), BF16))
        plan.append(outs)
    return in_specs, args, out_specs, out_shapes, plan


def _run_side_casts(src_refs, dst_refs, plan):
    for src, outs in zip(src_refs, plan):
        for dst_idx, moves in outs:
            dst = dst_refs[dst_idx]
            if moves is None:
                dst[...] = src[...].astype(BF16)
            else:
                for dst_col, src_col, width in moves:
                    dst[:, dst_col:dst_col + width] = src[:, src_col:src_col + width].astype(BF16)


def _w_in_layout(zw, dm, tn):
    gates = [((j * N_BRANCH + n) * tn, zw + n * dm + j * tn, tn)
             for j in range(dm // tn) for n in range(N_BRANCH)]
    return [(zw, [(0, 0, zw)]), (N_BRANCH * dm, gates)]


def _mod_kernel(c_ref, w_ref, b_ref, o_ref):
    c = c_ref[...]
    s = (c * jax.nn.sigmoid(c)).astype(BF16)
    o_ref[...] = jnp.dot(s, w_ref[...].astype(BF16),
                         preferred_element_type=F32) + b_ref[...]


def _mod_all(cvec, w_mod, b_mod, tn=512):
    depth, d, n = w_mod.shape
    return pl.pallas_call(
        _mod_kernel,
        out_shape=jax.ShapeDtypeStruct((depth, 8, n), F32),
        grid=(depth, n // tn),
        in_specs=[pl.BlockSpec((8, d), lambda l, j: (0, 0)),
                  pl.BlockSpec((None, d, tn), lambda l, j: (l, 0, j)),
                  pl.BlockSpec((None, 1, tn), lambda l, j: (l, 0, j))],
        out_specs=pl.BlockSpec((None, 8, tn), lambda l, j: (l, 0, j)),
        compiler_params=_params(("arbitrary", "arbitrary"), 40),
        name="mod",
    )(cvec, w_mod, b_mod.reshape(depth, 1, n))


def _prenorm_kernel(x_ref, g_ref, sh_ref, sc_ref, o_ref, stat_ref):
    tm, d = x_ref.shape
    slab = math.gcd(16, tm)
    _row_rinv(x_ref, stat_ref)

    def body(i, carry):
        sl = pl.ds(pl.multiple_of(i * slab, slab), slab)
        h = x_ref[sl, :] * _rinv_rows(stat_ref, sl, d) * g_ref[...]
        o_ref[sl, :] = (h * (1.0 + sc_ref[...]) + sh_ref[...]).astype(o_ref.dtype)
        return carry
    lax.fori_loop(0, tm // slab, body, 0, unroll=2)


def _prenorm(rng, total_rows, gain, mod_l, k_shift, p_rows, s_rows, tm, into=None):
    src, src_row0, uni_row0, nrows = rng
    d = src.shape[1]
    st0, ut0 = src_row0 // tm, uni_row0 // tm
    in_specs = [pl.BlockSpec((tm, d), lambda i: (st0 + i, 0)),
                pl.BlockSpec((1, d), lambda i: (0, 0)),
                _mod_spec(k_shift, d, tm, p_rows, s_rows, ut0),
                _mod_spec(k_shift + 1, d, tm, p_rows, s_rows, ut0)]
    args = [src, gain.reshape(1, d), mod_l, mod_l]
    body, aliases = _fill(_prenorm_kernel, in_specs, args, into)
    return pl.pallas_call(
        body,
        out_shape=jax.ShapeDtypeStruct((total_rows, d), BF16),
        grid=(nrows // tm,),
        in_specs=in_specs,
        out_specs=pl.BlockSpec((tm, d), lambda i: (ut0 + i, 0)),
        scratch_shapes=[pltpu.VMEM((tm, 128), F32)],
        input_output_aliases=aliases,
        compiler_params=_params(("arbitrary",), 40),
        name="prenorm",
    )(*args)


def _mm_kernel(a_ref, b_ref, o_ref):
    o_ref[...] = jnp.dot(a_ref[...], b_ref[...],
                         preferred_element_type=F32).astype(o_ref.dtype)


def _mm_side_kernel(*refs, n_cast, plan):
    a_ref, b_ref = refs[:2]
    _run_side_casts(refs[2:2 + n_cast], refs[3 + n_cast:], plan)
    _mm_kernel(a_ref, b_ref, refs[2 + n_cast])


def _matmul_cols(a, w, n_cols, tm, tn, out_dtype, vmem_mib, name, side=()):
    m, k = a.shape
    tm, tn = min(tm, m), min(tn, n_cols)
    nn = n_cols // tn
    c_in, c_args, c_out, c_shapes, plan = _side_casts(side, (m // tm) * nn,
                                                      lambda i, j: i * nn + j)
    res = pl.pallas_call(
        functools.partial(_mm_side_kernel, n_cast=len(side), plan=plan),
        out_shape=[jax.ShapeDtypeStruct((m, n_cols), out_dtype)] + c_shapes,
        grid=(m // tm, nn),
        in_specs=[pl.BlockSpec((tm, k), lambda i, j: (i, 0)),
                  pl.BlockSpec((k, tn), lambda i, j: (0, j))] + c_in,
        out_specs=[pl.BlockSpec((tm, tn), lambda i, j: (i, j))] + c_out,
        compiler_params=_params(("arbitrary", "arbitrary"), vmem_mib),
        name=name,
    )(a, w, *c_args)
    return res[0], res[1:]


def _matmul_batched_rhs(a, b, tm, tn, out_dtype, vmem_mib, name, total_rows, row0, into=None):
    m, k = a.shape
    nb, _, n = b.shape
    tm, tn = min(tm, m), min(tn, n)
    t0, nt = row0 // tm, m // tm
    in_specs = [pl.BlockSpec((tm, k), lambda bi, i, j: (i, 0)),
                pl.BlockSpec((None, k, tn), lambda bi, i, j: (bi, 0, j))]
    args = [a, b]
    body, aliases = _fill(_mm_kernel, in_specs, args, into)
    return pl.pallas_call(
        body,
        out_shape=jax.ShapeDtypeStruct((total_rows, n), out_dtype),
        grid=(nb, nt, n // tn),
        in_specs=in_specs,
        out_specs=pl.BlockSpec((tm, tn), lambda bi, i, j: (t0 + bi * nt + i, j)),
        input_output_aliases=aliases,
        compiler_params=_params(("arbitrary", "arbitrary", "arbitrary"), vmem_mib),
        name=name,
    )(*args)


FOLD_MIN_SEQ = 1024
FLIP_BLOCK = 128
FOUR1_ROWS = 4096


def _fold_cols(seq):
    return -(-(seq // 2 + 1) // 128) * 128


def _dft_matrix(n, kcols, sign, fold, split=64):
    col = jnp.arange(2 * kcols, dtype=jnp.int32)
    t = col % kcols
    phase = (col // kcols).astype(F32) * (sign * 0.5 * math.pi)
    w = jnp.full((2 * kcols,), 1.0 / math.sqrt(n), F32)
    if fold:
        w = w * jnp.where((t == 0) | (t == n // 2), 0.5, jnp.where(t < n // 2, 1.0, 0.0))

    def angle(rows, period):
        r = jnp.arange(rows, dtype=jnp.int32)
        return ((r[:, None] * t[None, :]) % period).astype(F32) * (2.0 * math.pi / period)

    if n <= split or n % split:
        return jnp.cos(angle(n, n) + phase) * w
    ang_a = angle(n // split, n // split)
    ang_b = angle(split, n) + phase
    tab = (jnp.cos(ang_a)[:, None, :] * jnp.cos(ang_b)[None]
           - jnp.sin(ang_a)[:, None, :] * jnp.sin(ang_b)[None]) * w
    return tab.reshape(n, 2 * kcols)


def _four1_kernel(u_ref, t_ref, o_ref, *, seq, gw, kfold):
    u = u_ref[...].astype(BF16)
    pq = jnp.dot(u, t_ref[...], preferred_element_type=F32)
    if not kfold:
        for b in range(o_ref.shape[0]):
            o_ref[b, 0:seq, :] = pq[b * seq:(b + 1) * seq, :gw].astype(o_ref.dtype)
            o_ref[b, seq:2 * seq, :] = pq[b * seq:(b + 1) * seq, gw:].astype(o_ref.dtype)
        return
    pqb = pq.astype(BF16)
    nblk = seq // FLIP_BLOCK
    r = lax.broadcasted_iota(jnp.int32, (FLIP_BLOCK, FLIP_BLOCK), 0)
    c = lax.broadcasted_iota(jnp.int32, (FLIP_BLOCK, FLIP_BLOCK), 1)
    anti = jnp.where(r + c == FLIP_BLOCK - 1, 1.0, 0.0).astype(BF16)
    flipped = jnp.concatenate(
        [jnp.dot(anti, pqb[(nblk - 1 - i) * FLIP_BLOCK:(nblk - i) * FLIP_BLOCK],
                 preferred_element_type=F32) for i in range(kfold // FLIP_BLOCK)], axis=0)
    row = lax.broadcasted_iota(jnp.int32, flipped.shape, 0)
    head = pqb[0:kfold].astype(F32)
    rev = jnp.where(row == 0, head[0:1], pltpu.roll(flipped, 1, axis=0))
    o_ref[0, 0:kfold, :] = (head[:, :gw] + rev[:, :gw]).astype(o_ref.dtype)
    o_ref[0, kfold:2 * kfold, :] = (head[:, gw:] - rev[:, gw:]).astype(o_ref.dtype)


def _fourier_stage1(z, tab, row0, nb, seq, bw, col0):
    gw = bw // FOURIER_GROUPS
    kfold = _fold_cols(seq) if seq >= FOLD_MIN_SEQ else 0
    krows = 2 * (kfold or seq)
    per = 1 if kfold else math.gcd(math.gcd(nb, max(1, FOUR1_ROWS // seq)), row0 // seq)
    rb0, cb0 = row0 // (per * seq), col0 // gw
    return pl.pallas_call(
        functools.partial(_four1_kernel, seq=seq, gw=gw, kfold=kfold),
        out_shape=jax.ShapeDtypeStruct((nb, krows, bw), BF16),
        grid=(nb // per, FOURIER_GROUPS),
        in_specs=[pl.BlockSpec((per * seq, gw), lambda b, g: (rb0 + b, cb0 + g)),
                  pl.BlockSpec((gw, 2 * gw), lambda b, g: (0, 0))],
        out_specs=pl.BlockSpec((per, krows, gw), lambda b, g: (b, 0, g)),
        compiler_params=_params(("arbitrary", "arbitrary"), 48),
        name="fourier1",
    )(z, tab)


def _fourier_matrix(seq):
    if seq >= FOLD_MIN_SEQ:
        return _dft_matrix(seq, _fold_cols(seq), 1.0, True).astype(BF16)
    return _dft_matrix(seq, seq, 1.0, False).astype(BF16)


def _lru_kernel(*refs, seq, rows, nseg, seg_len, chained, chunk):
    if chained:
        (zx_ref, zy_ref, cw_ref, cb_ref, wa_ref, wx_ref, ba_ref, bx_ref, lam_ref,
         h0_ref, o_ref, pad_ref, a0_ref, a1_ref, b0_ref, b1_ref) = refs
        st_ref = None
    else:
        (zx_ref, zy_ref, cw_ref, cb_ref, wa_ref, wx_ref, ba_ref, bx_ref, lam_ref,
         o_ref, st_ref, pad_ref, a0_ref, a1_ref, b0_ref, b1_ref) = refs
        h0_ref = None
    a_refs, b_refs = (a0_ref, a1_ref), (b0_ref, b1_ref)
    cw_ch = zx_ref.shape[1]
    ext = chunk + 2 * HALO
    n_taps = cw_ref.shape[0]
    pitch = seg_len + SEG_PAD
    segs = chunk // seg_len

    pad_ref[0:HALO, :] = jnp.zeros((HALO, cw_ch), F32)
    pad_ref[HALO + rows:, :] = jnp.zeros((HALO, cw_ch), F32)
    pad_ref[HALO:HALO + rows, :] = zx_ref[...]

    neg_sp = [-LRU_C * _softplus(-lam_ref[d:d + 1, :]) for d in range(2)]

    def seg_rows(ci, j):
        return pl.ds(pl.multiple_of(ci * (segs * pitch), 8) + j * pitch, seg_len)

    def gates(ci, carry):
        r0 = pl.multiple_of(ci * chunk, chunk)
        xe = pad_ref[pl.ds(r0, ext), :]
        t = (r0 + lax.broadcasted_iota(jnp.int32, (chunk, cw_ch), 0)) & (seq - 1)
        xc = jnp.zeros((chunk, cw_ch), F32) + cb_ref[...]
        for j in range(n_taps):
            d = j - CONV_LEFT
            if d == 0:
                xs = xe[HALO:HALO + chunk]
            else:
                xs = pltpu.roll(xe, (-d) % ext, axis=0)[HALO:HALO + chunk]
                xs = jnp.where((t >= -d) if d < 0 else (t < seq - d), xs, 0.0)
            xc = xc + xs * cw_ref[j:j + 1, :]
        xcb = xc.astype(BF16)
        for d in range(2):
            r = _sigmoid(jnp.dot(xcb, wa_ref[d], preferred_element_type=F32)
                         + ba_ref[d:d + 1, :])
            i = _sigmoid(jnp.dot(xcb, wx_ref[d], preferred_element_type=F32)
                         + bx_ref[d:d + 1, :])
            a = jnp.exp(r * neg_sp[d])
            bterm = jnp.sqrt(1.0 - a * a) * (i * xc)
            for j in range(segs):
                a_refs[d][seg_rows(ci, j), :] = a[j * seg_len:(j + 1) * seg_len]
                b_refs[d][seg_rows(ci, j), :] = bterm[j * seg_len:(j + 1) * seg_len]
        return carry

    lax.fori_loop(0, rows // chunk, gates, 0)

    def scan(direction):
        def body(s, carry):
            h, p = carry
            pos = s if direction == 0 else seg_len - 1 - s
            idx = pl.ds(pos, nseg, stride=pitch)
            a = a_refs[direction][idx, :]
            h = a * h + b_refs[direction][idx, :]
            b_refs[direction][idx, :] = h
            if chained:
                p = a * p
                a_refs[direction][idx, :] = p
            return h, p
        init = (jnp.zeros((nseg, cw_ch), F32), jnp.ones((nseg, cw_ch), F32))
        lax.fori_loop(0, seg_len, body, init, unroll=8)

    scan(0)
    scan(1)

    if chained:
        carry = h0_ref[0:1, :]
        for q in range(nseg):
            sl = slice(q * pitch, q * pitch + seg_len)
            hq = b_refs[0][sl, :] + a_refs[0][sl, :] * carry
            b_refs[0][sl, :] = hq
            carry = hq[seg_len - 1:seg_len, :]
        carry = h0_ref[1:2, :]
        for q in reversed(range(nseg)):
            sl = slice(q * pitch, q * pitch + seg_len)
            hq = b_refs[1][sl, :] + a_refs[1][sl, :] * carry
            b_refs[1][sl, :] = hq
            carry = hq[0:1, :]
    else:
        st_ref[0] = b_refs[0][pl.ds(seq - 1, nseg, stride=pitch), :]
        st_ref[1] = b_refs[1][pl.ds(0, nseg, stride=pitch), :]

    def emit(ci, carry):
        r0 = pl.multiple_of(ci * chunk, chunk)
        for j in range(segs):
            sl = pl.ds(r0 + j * seg_len, seg_len)
            hsum = b_refs[0][seg_rows(ci, j), :] + b_refs[1][seg_rows(ci, j), :]
            o_ref[sl, :] = (hsum * jax.nn.gelu(zy_ref[sl, :])).astype(o_ref.dtype)
        return carry

    lax.fori_loop(0, rows // chunk, emit, 0)


def _lru(z, layer, conv_w, conv_b, wa, wx, ba, bx, lam, h0, *, row0, rows, ngroups,
         seq, nseg, seg_len, col_x, col_y, bw, into=None):
    cb = wa.shape[-1]
    ncb = bw // cb
    chained = h0 is not None
    chunk = max(min(256, rows), seg_len)
    assert chunk % seg_len == 0 and rows % chunk == 0 and seg_len % 8 == 0
    assert chained or seg_len == seq
    rb0 = row0 // rows
    cx0, cy0 = col_x // cb, col_y // cb
    in_specs = [
        pl.BlockSpec((rows, cb), lambda g, c: (rb0 + g, cx0 + c)),
        pl.BlockSpec((rows, cb), lambda g, c: (rb0 + g, cy0 + c)),
        pl.BlockSpec((None, conv_w.shape[1], cb), lambda g, c: (layer, 0, c)),
        pl.BlockSpec((None, 1, cb), lambda g, c: (layer, 0, c)),
        pl.BlockSpec((None, 2, None, cb, cb), lambda g, c: (layer, 0, c, 0, 0)),
        pl.BlockSpec((None, 2, None, cb, cb), lambda g, c: (layer, 0, c, 0, 0)),
        pl.BlockSpec((None, 2, cb), lambda g, c: (layer, 0, c)),
        pl.BlockSpec((None, 2, cb), lambda g, c: (layer, 0, c)),
        pl.BlockSpec((None, 2, cb), lambda g, c: (layer, 0, c)),
    ]
    args = [z, z, conv_w, conv_b.reshape(conv_b.shape[0], 1, bw), wa, wx, ba, bx, lam]
    out_shape = [jax.ShapeDtypeStruct((z.shape[0], bw), BF16)]
    out_specs = [pl.BlockSpec((rows, cb), lambda g, c: (rb0 + g, c))]
    if chained:
        in_specs.append(pl.BlockSpec((None, 2, cb), lambda g, c: (g, 0, c)))
        args.append(h0)
    else:
        assert ngroups == 1
        out_shape.append(jax.ShapeDtypeStruct((2, rows // seq, bw), F32))
        out_specs.append(pl.BlockSpec((2, rows // seq, cb), lambda g, c: (0, 0, c)))
    kern = functools.partial(_lru_kernel, seq=seq, rows=rows, nseg=nseg, seg_len=seg_len,
                             chained=chained, chunk=chunk)
    body, aliases = _fill(kern, in_specs, args, into)
    res = pl.pallas_call(
        body,
        out_shape=out_shape,
        grid=(ngroups, ncb),
        in_specs=in_specs,
        out_specs=out_specs,
        scratch_shapes=[pltpu.VMEM((rows + 2 * HALO, cb), F32)]
                       + [pltpu.VMEM((nseg * (seg_len + SEG_PAD), cb), F32)] * 4,
        input_output_aliases=aliases,
        compiler_params=_params(("arbitrary", "arbitrary"), 48),
        name="rglru",
    )(*args)
    return res if not chained else res[0]


def _rope_tables(seq, hd):
    rows = seq // GRID_W
    row = jnp.repeat(jnp.arange(rows), GRID_W).astype(F32)
    col = jnp.tile(jnp.arange(GRID_W), rows).astype(F32)
    ax = hd // 2
    inv = ROPE_THETA ** (-jnp.arange(0, ax, 2, dtype=F32) / ax)
    ang_r = row[:, None] * inv[None, :]
    ang_c = col[:, None] * inv[None, :]
    cos = jnp.concatenate([jnp.cos(ang_r)] * 2 + [jnp.cos(ang_c)] * 2, axis=-1)
    sin = jnp.concatenate([-jnp.sin(ang_r), jnp.sin(ang_r),
                           -jnp.sin(ang_c), jnp.sin(ang_c)], axis=-1)
    return cos, sin


def _rotate(x, cos, sin):
    hd = x.shape[1]
    q = hd // 4
    lane = lax.broadcasted_iota(jnp.int32, x.shape, 1)
    ahead = pltpu.roll(x, hd - q, axis=1)
    behind = pltpu.roll(x, q, axis=1)
    partner = jnp.where((lane & q) == 0, ahead, behind)
    return x * cos + partner * sin


def _attn_kernel(*refs, seq, cache_len, tq, grp, hd, rope, emit_kv):
    refs = list(refs)
    zq_ref, zk_ref, zv_ref, qn_ref, kn_ref = refs[:5]
    pos = 5
    if rope:
        cq_ref, sq_ref, ck_ref, sk_ref = refs[pos:pos + 4]
        pos += 4
    if cache_len:
        kc_ref, vc_ref = refs[pos:pos + 2]
        pos += 2
    o_ref = refs[pos]
    pos += 1
    if emit_kv:
        ko_ref, vo_ref = refs[pos:pos + 2]
        pos += 2
    ks_ref, vt_ref = refs[pos:pos + 2]

    @pl.when(pl.program_id(2) == 0)
    def _():
        k = _rms(zk_ref[...], kn_ref[...])
        v = zv_ref[...]
        if emit_kv:
            ko_ref[...] = k
            vo_ref[...] = v
        if rope:
            k = _rotate(k, ck_ref[...], sk_ref[...])
        ks_ref[0:seq, :] = k.astype(BF16)
        vt_ref[:, 0:seq] = v.T.astype(BF16)
        if cache_len:
            ks_ref[seq:seq + cache_len, :] = kc_ref[...].astype(BF16)
            vt_ref[:, seq:seq + cache_len] = vc_ref[...].T.astype(BF16)

    scale = hd ** -0.5 * math.log2(math.e)
    qs = []
    for hh in range(grp):
        q = _rms(zq_ref[:, hh * hd:(hh + 1) * hd], qn_ref[...])
        if rope:
            q = _rotate(q, cq_ref[...], sq_ref[...])
        qs.append((q * scale).astype(BF16))
    q = jnp.concatenate(qs, axis=0)
    s = lax.dot_general(ks_ref[...], q, (((1,), (1,)), ((), ())),
                        preferred_element_type=F32)
    p = jnp.exp2(s - jnp.max(s, axis=0, keepdims=True))
    acc = jnp.dot(vt_ref[...], p.astype(BF16), preferred_element_type=F32)
    out = (acc / jnp.sum(p, axis=0, keepdims=True)).T
    for hh in range(grp):
        o_ref[:, hh * hd:(hh + 1) * hd] = out[hh * tq:(hh + 1) * tq].astype(o_ref.dtype)


def _attention(z, layer, q_norm, k_norm, *, row0, nb, seq, n_kv, grp, hd, col_q, col_k, col_v,
               tq, rope_tabs=None, cache=None, emit_kv=False, into=None):
    qw = grp * hd
    cache_len = 0 if cache is None else cache[0].shape[2]
    nq = seq // tq
    rbq0, rbk0 = row0 // tq, row0 // seq
    cq0, ck0, cv0 = col_q // qw, col_k // hd, col_v // hd
    in_specs = [
        pl.BlockSpec((tq, qw), lambda b, h, i: (rbq0 + b * nq + i, cq0 + h)),
        pl.BlockSpec((seq, hd), lambda b, h, i: (rbk0 + b, ck0 + h)),
        pl.BlockSpec((seq, hd), lambda b, h, i: (rbk0 + b, cv0 + h)),
        pl.BlockSpec((None, 1, hd), lambda b, h, i: (layer, 0, 0)),
        pl.BlockSpec((None, 1, hd), lambda b, h, i: (layer, 0, 0)),
    ]
    args = [z, z, z, q_norm.reshape(-1, 1, hd), k_norm.reshape(-1, 1, hd)]
    if rope_tabs is not None:
        cos, sin = rope_tabs
        in_specs += [pl.BlockSpec((tq, hd), lambda b, h, i: (i, 0)),
                     pl.BlockSpec((tq, hd), lambda b, h, i: (i, 0)),
                     pl.BlockSpec((seq, hd), lambda b, h, i: (0, 0)),
                     pl.BlockSpec((seq, hd), lambda b, h, i: (0, 0))]
        args += [cos, sin, cos, sin]
    if cache is not None:
        in_specs += [pl.BlockSpec((None, None, cache_len, hd), lambda b, h, i: (b, h, 0, 0)),
                     pl.BlockSpec((None, None, cache_len, hd), lambda b, h, i: (b, h, 0, 0))]
        args += list(cache)
    out_shape = [jax.ShapeDtypeStruct((z.shape[0], n_kv * qw), BF16)]
    out_specs = [pl.BlockSpec((tq, qw), lambda b, h, i: (rbq0 + b * nq + i, h))]
    if emit_kv:
        out_shape += [jax.ShapeDtypeStruct((nb * seq, n_kv * hd), F32)] * 2
        out_specs += [pl.BlockSpec((seq, hd), lambda b, h, i: (b, h))] * 2
    kern = functools.partial(_attn_kernel, seq=seq, cache_len=cache_len, tq=tq, grp=grp,
                             hd=hd, rope=rope_tabs is not None, emit_kv=emit_kv)
    body, aliases = _fill(kern, in_specs, args, into)
    res = pl.pallas_call(
        body,
        out_shape=out_shape,
        grid=(nb, n_kv, nq),
        in_specs=in_specs,
        out_specs=out_specs,
        scratch_shapes=[pltpu.VMEM((seq + cache_len, hd), BF16),
                        pltpu.VMEM((hd, seq + cache_len), BF16)],
        input_output_aliases=aliases,
        compiler_params=_params(("arbitrary", "arbitrary", "arbitrary"), 48),
        name="attention",
    )(*args)
    return res if emit_kv else res[0]


def _pool_kernel(u_ref, w_ref, sc_ref, o_ref, pad_ref, *, rows, chunk, seq_p, seq_s, groups_p):
    gw = u_ref.shape[1]
    ext = chunk + 2 * HALO
    seq = jnp.where(pl.program_id(0) < groups_p, seq_p, seq_s)
    pad_ref[0:HALO, :] = jnp.zeros((HALO, gw), F32)
    pad_ref[HALO + rows:, :] = jnp.zeros((HALO, gw), F32)
    pad_ref[HALO:HALO + rows, :] = u_ref[...]

    for gi, win in enumerate(POOL_WINDOWS):
        hw = win // 2

        def masked_sum(xe, t, hw=hw):
            acc = xe[HALO:HALO + chunk]
            for d in range(-hw, hw):
                if d == 0:
                    continue
                xs = pltpu.roll(xe, (-d) % ext, axis=0)[HALO:HALO + chunk]
                acc = acc + jnp.where((t >= -d) if d < 0 else (t < seq - d), xs, 0.0)
            return acc

        def doubled_sum(xe, t, hw=hw):
            w = xe + pltpu.roll(xe, 1, axis=0)
            span = 1
            while span < hw:
                w = pltpu.roll(w, span, axis=0) + pltpu.roll(w, ext - span, axis=0)
                span *= 2
            return w[HALO:HALO + chunk]

        def run(window_sum, hw=hw):
            def body(ci, carry):
                r0 = pl.multiple_of(ci * chunk, chunk)
                xe = pad_ref[pl.ds(r0, ext), :]
                t = (r0 + lax.broadcasted_iota(jnp.int32, (chunk, gw), 0)) & (seq - 1)
                cnt = (jnp.minimum(t + hw, seq) - jnp.maximum(t - hw, 0)).astype(F32)
                pooled = window_sum(xe, t) / cnt - xe[HALO:HALO + chunk]
                y = jnp.dot(pooled.astype(BF16), w_ref[...], preferred_element_type=F32)
                o_ref[pl.ds(r0, chunk), :] = (y * sc_ref[...]).astype(o_ref.dtype)
                return carry

            lax.fori_loop(0, rows // chunk, body, 0)

        this_group = pl.program_id(1) == gi
        whole_seq = pl.program_id(0) >= groups_p
        pl.when(jnp.logical_and(this_group, whole_seq))(functools.partial(run, doubled_sum))
        pl.when(jnp.logical_and(this_group, jnp.logical_not(whole_seq)))(
            functools.partial(run, masked_sum))


def _pool(z, layer, pool_w, pool_scale, *, rows, groups_p, seq_p, seq_s, col0, bw):
    r = z.shape[0]
    g = pool_w.shape[1]
    gw = bw // g
    cb0 = col0 // gw
    chunk = min(256, rows)
    return pl.pallas_call(
        functools.partial(_pool_kernel, rows=rows, chunk=chunk, seq_p=seq_p, seq_s=seq_s,
                          groups_p=groups_p),
        out_shape=jax.ShapeDtypeStruct((r, bw), BF16),
        grid=(r // rows, g),
        in_specs=[pl.BlockSpec((rows, gw), lambda i, j: (i, cb0 + j)),
                  pl.BlockSpec((None, None, gw, gw), lambda i, j: (layer, j, 0, 0)),
                  pl.BlockSpec((None, 1, gw), lambda i, j: (layer, 0, j))],
        out_specs=pl.BlockSpec((rows, gw), lambda i, j: (i, j)),
        scratch_shapes=[pltpu.VMEM((rows + 2 * HALO, gw), F32)],
        compiler_params=_params(("arbitrary", "arbitrary"), 40),
        name="pool",
    )(z, pool_w, pool_scale.reshape(pool_scale.shape[0], 1, bw))


def _merge_kernel(*refs, n_cast, plan):
    h_ref, g_ref = refs[:2]
    o_refs = refs[2:2 + N_BRANCH]
    wb_ref = refs[2 + N_BRANCH]
    n_in = 3 + N_BRANCH
    out_ref = refs[n_in + n_cast]
    _run_side_casts(refs[n_in:n_in + n_cast], refs[n_in + n_cast + 1:], plan)
    tn = out_ref.shape[1]
    gates = jnp.dot(h_ref[...], g_ref[...], preferred_element_type=F32)
    acc = None
    for n, o_ref in enumerate(o_refs):
        gate = _sigmoid(gates[:, n * tn:(n + 1) * tn])
        y = jnp.dot(o_ref[...], wb_ref[n], preferred_element_type=F32)
        acc = gate * y if acc is None else acc + gate * y
    out_ref[...] = acc.astype(out_ref.dtype)


def _gate_merge(h, w_gates, branches, w_branch, side, tm=512):
    m, d = h.shape
    bw = w_branch.shape[1]
    dm = w_branch.shape[2]
    tm, tn = min(tm, m), GATE_TN
    nm = m // tm
    ospecs = [pl.BlockSpec((tm, bw), lambda j, i: (i, 0)) for _ in range(N_BRANCH)]
    c_in, c_args, c_out, c_shapes, plan = _side_casts(side, (dm // tn) * nm,
                                                      lambda j, i: j * nm + i)
    res = pl.pallas_call(
        functools.partial(_merge_kernel, n_cast=len(side), plan=plan),
        out_shape=[jax.ShapeDtypeStruct((m, dm), BF16)] + c_shapes,
        grid=(dm // tn, nm),
        in_specs=[pl.BlockSpec((tm, d), lambda j, i: (i, 0)),
                  pl.BlockSpec((d, N_BRANCH * tn), lambda j, i: (0, j))] + ospecs
                 + [pl.BlockSpec((N_BRANCH, bw, tn), lambda j, i: (0, 0, j))] + c_in,
        out_specs=[pl.BlockSpec((tm, tn), lambda j, i: (i, j))] + c_out,
        compiler_params=_params(("arbitrary", "arbitrary"), 58),
        name="gate_merge",
    )(h, w_gates, *branches, w_branch, *c_args)
    return res[0], res[1:]


def _out_residual_kernel(m_ref, w_ref, x_ref, g_ref, gate_ref, o_ref, stat_ref):
    tm, d = x_ref.shape
    slab = math.gcd(16, tm)
    o_ref[...] = jnp.dot(m_ref[...], w_ref[...], preferred_element_type=F32)
    _row_rinv(o_ref, stat_ref)
    scale = g_ref[...] * gate_ref[...]

    def body(i, carry):
        sl = pl.ds(pl.multiple_of(i * slab, slab), slab)
        o_ref[sl, :] = x_ref[sl, :] + o_ref[sl, :] * _rinv_rows(stat_ref, sl, d) * scale
        return carry
    lax.fori_loop(0, tm // slab, body, 0, unroll=2)


def _out_residual(rng, merged, w_out, gain, mod_l, k_gate, p_rows, s_rows, tm, into=None):
    src, src_row0, uni_row0, nrows = rng
    r, d = merged.shape
    st0, ut0 = src_row0 // tm, uni_row0 // tm
    in_specs = [pl.BlockSpec((tm, d), lambda i: (ut0 + i, 0)),
                pl.BlockSpec((d, d), lambda i: (0, 0), pipeline_mode=pl.Buffered(1)),
                pl.BlockSpec((tm, d), lambda i: (st0 + i, 0)),
                pl.BlockSpec((1, d), lambda i: (0, 0)),
                _mod_spec(k_gate, d, tm, p_rows, s_rows, ut0)]
    args = [merged, w_out, src, gain.reshape(1, d), mod_l]
    body, aliases = _fill(_out_residual_kernel, in_specs, args, into)
    return pl.pallas_call(
        body,
        out_shape=jax.ShapeDtypeStruct((r, d), F32),
        grid=(nrows // tm,),
        in_specs=in_specs,
        out_specs=pl.BlockSpec((tm, d), lambda i: (ut0 + i, 0)),
        scratch_shapes=[pltpu.VMEM((tm, 128), F32)],
        input_output_aliases=aliases,
        compiler_params=_params(("arbitrary",), 60),
        name="out_residual",
    )(*args)


def _ffn_kernel(x_ref, gpre_ref, sh_ref, sc_ref, gate_ref, gpost_ref, wg_ref, wu_ref, wd_ref,
                o_ref, h_ref, stat_ref, cst_ref, *, slab, overlap):
    k = pl.program_id(1)
    tm, d = x_ref.shape
    nslab = tm // slab

    def rows(i):
        return pl.ds(pl.multiple_of(i * slab, slab), slab)

    @pl.when(k == 0)
    def _():
        _row_rinv(x_ref, stat_ref)
        cst_ref[0] = jnp.broadcast_to(gpre_ref[...] * (1.0 + sc_ref[...]), (slab, d))
        cst_ref[1] = jnp.broadcast_to(sh_ref[...], (slab, d))

        def body(i, carry):
            h = x_ref[rows(i), :] * _rinv_rows(stat_ref, rows(i), d) * cst_ref[0] + cst_ref[1]
            h_ref[rows(i), :] = h.astype(BF16)
            o_ref[rows(i), :] = jnp.zeros((slab, d), F32)
            return carry
        lax.fori_loop(0, nslab, body, 0, unroll=2)

    last = pl.num_programs(1) - 1

    def chunk(lo):
        h = h_ref[...]
        a = jnp.dot(h, wg_ref[:, lo:], preferred_element_type=F32)
        b = jnp.dot(h, wu_ref[:, lo:], preferred_element_type=F32)
        act = (a * _sigmoid(a) * b).astype(BF16)
        o_ref[...] += jnp.dot(act, wd_ref[lo:, :], preferred_element_type=F32)

    if overlap:
        pl.when(k < last)(lambda: chunk(0))
        pl.when(k == last)(lambda: chunk(overlap))
    else:
        chunk(0)

    @pl.when(k == last)
    def _():
        _row_rinv(o_ref, stat_ref)
        cst_ref[0] = jnp.broadcast_to(gpost_ref[...] * gate_ref[...], (slab, d))

        def body(i, carry):
            y = o_ref[rows(i), :] * _rinv_rows(stat_ref, rows(i), d) * cst_ref[0]
            o_ref[rows(i), :] = x_ref[rows(i), :] + y
            return carry
        lax.fori_loop(0, nslab, body, 0, unroll=2)


def _ffn(x, row0, nrows, g_pre, g_post, mod_l, wg, wu, wd, p_rows, s_rows, tm=512):
    d = x.shape[1]
    dff = wg.shape[1]
    tm, tc = min(tm, nrows), min(FFN_CHUNK, dff)
    nk = pl.cdiv(dff, tc)
    overlap = nk * tc - dff

    def chunk_start(k):
        return pl.multiple_of(jnp.minimum(k * tc, dff - tc), math.gcd(tc, dff))

    t0 = row0 // tm
    return pl.pallas_call(
        functools.partial(_ffn_kernel, slab=math.gcd(16, tm), overlap=overlap),
        out_shape=jax.ShapeDtypeStruct((nrows, d), F32),
        grid=(nrows // tm, nk),
        in_specs=[pl.BlockSpec((tm, d), lambda i, k: (t0 + i, 0), pipeline_mode=pl.Buffered(1)),
                  pl.BlockSpec((1, d), lambda i, k: (0, 0)),
                  _mod_spec(3, d, tm, p_rows, s_rows, t0),
                  _mod_spec(4, d, tm, p_rows, s_rows, t0),
                  _mod_spec(5, d, tm, p_rows, s_rows, t0),
                  pl.BlockSpec((1, d), lambda i, k: (0, 0)),
                  pl.BlockSpec((pl.Element(d), pl.Element(tc)),
                               lambda i, k: (0, chunk_start(k))),
                  pl.BlockSpec((pl.Element(d), pl.Element(tc)),
                               lambda i, k: (0, chunk_start(k))),
                  pl.BlockSpec((pl.Element(tc), pl.Element(d)),
                               lambda i, k: (chunk_start(k), 0))],
        out_specs=pl.BlockSpec((tm, d), lambda i, k: (i, 0)),
        scratch_shapes=[pltpu.VMEM((tm, d), BF16), pltpu.VMEM((tm, 128), F32),
                        pltpu.VMEM((2, math.gcd(16, tm), d), F32)],
        compiler_params=_params(("arbitrary", "arbitrary"), 60),
        name="ffn",
    )(x, g_pre.reshape(1, d), mod_l, mod_l, mod_l, g_post.reshape(1, d), wg, wu, wd)


def kernel(x_prompt, x_sample, c, cache_k, cache_v, state_lru, c_ctx, w_mod, b_mod, g_pre_mix, g_post_mix, g_pre_ffn, g_post_ffn, w_in, conv_w, conv_b, lru_wa, lru_ba, lru_wx, lru_bx, lru_lambda, q_norm, k_norm, pool_w, pool_scale, w_branch, w_out, w_ffn_gate, w_ffn_up, w_ffn_down):
    nbp, seq_p, d = x_prompt.shape
    nbs, seq_s, _ = x_sample.shape
    depth = w_in.shape[0]
    bw = w_branch.shape[2]
    cache_len, n_kv, hd = cache_k.shape[2:]
    grp = bw // hd // n_kv
    kvw = n_kv * hd
    p_rows, s_rows = nbp * seq_p, nbs * seq_s
    rows = p_rows + s_rows
    assert seq_p & (seq_p - 1) == 0 and seq_s & (seq_s - 1) == 0
    assert p_rows % seq_s == 0, "row groups of seq_s rows must tile the prompt rows"
    col_a, col_bx, col_by, col_q = 0, bw, 2 * bw, 3 * bw
    col_k = col_q + bw
    col_v = col_k + kvw
    col_d = col_v + kvw
    col_g = col_d + bw
    zw = col_g

    def rows_tile(pref):
        return math.gcd(pref, math.gcd(p_rows, seq_s))

    w_mix_l = w_in[0, :, :zw].astype(BF16)
    w_gates_l = w_branch_l = w_out_l = None
    in_layout = _w_in_layout(zw, d, min(GATE_TN, d))
    w_branch_2d = w_branch.reshape(depth, N_BRANCH * bw, d)
    wa_b = lru_wa.astype(BF16)
    wx_b = lru_wx.astype(BF16)
    pool_w_b = pool_w.astype(BF16)

    gw = bw // FOURIER_GROUPS
    tab_c = _dft_matrix(gw, gw, -1.0, False).astype(BF16)
    dft_p = _fourier_matrix(seq_p)
    dft_s = _fourier_matrix(seq_s)
    rope_tabs = _rope_tables(seq_s, hd)

    cvec = jnp.concatenate([c_ctx[None, :], c, jnp.zeros((8 - 1 - nbs, d), c.dtype)], axis=0)
    mod = _mod_all(cvec.astype(F32), w_mod, b_mod)

    cache_kt = jnp.transpose(cache_k, (0, 1, 3, 2, 4)).astype(F32)
    cache_vt = jnp.transpose(cache_v, (0, 1, 3, 2, 4)).astype(F32)

    x_ranges = [(x_prompt.reshape(p_rows, d).astype(F32), 0, 0, p_rows),
                (x_sample.reshape(s_rows, d).astype(F32), 0, p_rows, s_rows)]
    tm_norm = rows_tile(256)
    k_list, v_list, s_list = [], [], []
    y_p = y_s = None
    for l in range(depth):
        mod_l = mod[l].reshape(8, 1, 6 * d)
        h = None
        for rng in x_ranges:
            h = _prenorm(rng, rows, g_pre_mix[l], mod_l, 0, p_rows, seq_s, tm_norm, into=h)
        if l == 0:
            z, (_, w_gates_l, w_branch_l, w_out_l) = _matmul_cols(
                h, w_mix_l, zw, rows_tile(1024), 512, F32, 56, "in_proj",
                side=[(w_in, 0, in_layout), (w_branch_2d, 0, None), (w_out, 0, None)])
            w_branch_l = w_branch_l.reshape(N_BRANCH, bw, d)
        else:
            z, _ = _matmul_cols(h, w_mix_l, zw, rows_tile(1024), 512, F32, 48, "in_proj")

        pq_p = _fourier_stage1(z, tab_c, 0, nbp, seq_p, bw, col_a)
        o_a = _matmul_batched_rhs(dft_p, pq_p, 512, 1024, BF16, 48, "fourier2p", rows, 0)
        pq_s = _fourier_stage1(z, tab_c, p_rows, nbs, seq_s, bw, col_a)
        o_a = _matmul_batched_rhs(dft_s, pq_s, 512, 1024, BF16, 48, "fourier2s", rows, p_rows,
                                  into=o_a)

        lru_args = (conv_w, conv_b, wa_b, wx_b, lru_ba, lru_bx, lru_lambda)
        o_b, st = _lru(z, l, *lru_args, None, row0=0, rows=p_rows, ngroups=1, seq=seq_p,
                       nseg=nbp, seg_len=seq_p, col_x=col_bx, col_y=col_by, bw=bw)
        seg = min(128, seq_s)
        o_b = _lru(z, l, *lru_args, state_lru[:, l].astype(F32), row0=p_rows, rows=seq_s,
                   ngroups=nbs, seq=seq_s, nseg=seq_s // seg, seg_len=seg,
                   col_x=col_bx, col_y=col_by, bw=bw, into=o_b)
        s_list.append(jnp.transpose(st, (1, 0, 2)))

        att = dict(n_kv=n_kv, grp=grp, hd=hd, col_q=col_q, col_k=col_k, col_v=col_v)
        o_c, k_p, v_p = _attention(z, l, q_norm, k_norm, row0=0, nb=nbp, seq=seq_p, tq=seq_p,
                                   emit_kv=True, **att)
        o_c = _attention(z, l, q_norm, k_norm, row0=p_rows, nb=nbs, seq=seq_s,
                         tq=min(256, seq_s), rope_tabs=rope_tabs,
                         cache=(cache_kt[:, l], cache_vt[:, l]), into=o_c, **att)
        k_list.append(k_p.reshape(nbp, seq_p, n_kv, hd))
        v_list.append(v_p.reshape(nbp, seq_p, n_kv, hd))

        o_d = _pool(z, l, pool_w_b, pool_scale, rows=seq_s, groups_p=p_rows // seq_s,
                    seq_p=seq_p, seq_s=seq_s, col0=col_d, bw=bw)

        side = [(w_ffn_gate, l, None), (w_ffn_up, l, None), (w_ffn_down, l, None)]
        if l + 1 < depth:
            side += [(w_in, l + 1, in_layout), (w_branch_2d, l + 1, None), (w_out, l + 1, None)]
        merged, cast = _gate_merge(h, w_gates_l, (o_a, o_b, o_c, o_d), w_branch_l,
                                   side=side, tm=rows_tile(512))
        x = None
        for rng in x_ranges:
            x = _out_residual(rng, merged, w_out_l, g_post_mix[l], mod_l, 2, p_rows, seq_s,
                              tm_norm, into=x)
        ffn = functools.partial(_ffn, x, g_pre=g_pre_ffn[l], g_post=g_post_ffn[l],
                                mod_l=mod_l, wg=cast[0], wu=cast[1], wd=cast[2], p_rows=p_rows,
                                s_rows=seq_s, tm=rows_tile(512))
        if l + 1 < depth:
            w_mix_l, w_gates_l = cast[3], cast[4]
            w_branch_l, w_out_l = cast[5].reshape(N_BRANCH, bw, d), cast[6]
            x_ranges = [(ffn(0, rows), 0, 0, rows)]
        else:
            y_p, y_s = ffn(0, p_rows), ffn(p_rows, s_rows)

    y_p = y_p.reshape(nbp, seq_p, d).astype(x_prompt.dtype)
    y_s = y_s.reshape(nbs, seq_s, d).astype(x_sample.dtype)
    new_k = jnp.stack(k_list, axis=1).astype(cache_k.dtype)
    new_v = jnp.stack(v_list, axis=1).astype(cache_v.dtype)
    new_s = jnp.stack(s_list, axis=1).astype(state_lru.dtype)
    return y_p, y_s, new_k, new_v, new_s
```
